```python
import jax, jax.numpy as jnp
from jax import lax
import numpy as np

D_MODEL = 2048
BATCH = 2
SEQ = 8192
DEPTH = 4
DEC_BATCH = 16
DEC_SEQ = 64
PAST_LEN = 2048

CHUNK = 64
N_PAIRS = DEPTH // 2
ROPE_THETA = 500000.0
EPS = 1e-6
NEG = -1e30
Q_BLOCK = 128

HA = 16
KVA = 2
DA = 64
ROT_A = DA // 4
WINDOW = 128
WIN_CHUNKS = WINDOW // CHUNK
HB = 4
DKB = 128
DVB = 256
HC = 8
KVC = 2
DC = 128
ROT_C = DC // 4
HI = 16
DI = 64
ROT_I = DI // 4
INDEX_TOPK = 256
HD = 8
DNOPE = 128
DROPE = 64
DVD = 128
KV_LORA = 512
D_FF = ((8 * D_MODEL + 767) // 768) * 256

EVEN_SPLITS = (HA * DA, KVA * DA, KVA * DA, HB * DKB, HB * DKB, HB * DVB, HB, HB, HB * DVB)
ODD_SPLITS = (HC * DC, KVC * DC, KVC * DC, HI * DI, DI, HI, HD * (DNOPE + DROPE), KV_LORA, DROPE)
EVEN_IN = sum(EVEN_SPLITS)
ODD_IN = sum(ODD_SPLITS)
EVEN_MIX = HA * DA + HB * DVB
ODD_MIX = HC * DC + HD * DVD

kernel_name = 'hybrid_stream_swa_mlstm_dsa_mla_step'


def split_cols(z, sizes):
    return jnp.split(z, np.cumsum(sizes)[:-1].tolist(), axis=-1)


def rms_norm(x, g):
    xf = x.astype(jnp.float32)
    y = xf * lax.rsqrt(jnp.mean(xf * xf, axis=-1, keepdims=True) + EPS)
    return (y * g.astype(jnp.float32)).astype(x.dtype)


def modulate(x, g, shift, scale):
    return rms_norm(x, g) * (1 + scale[:, None, :]) + shift[:, None, :]


def apply_rope(x, pos, rot_dim):
    half = rot_dim // 2
    inv = ROPE_THETA ** (-jnp.arange(half, dtype=jnp.float32) * 2.0 / rot_dim)
    ang = pos.astype(jnp.float32)[:, None] * inv[None, :]
    cos, sin = jnp.cos(ang)[:, None, :], jnp.sin(ang)[:, None, :]
    x1 = x[..., :half].astype(jnp.float32)
    x2 = x[..., half:rot_dim].astype(jnp.float32)
    rot = jnp.concatenate([x1 * cos - x2 * sin, x2 * cos + x1 * sin], axis=-1).astype(x.dtype)
    return jnp.concatenate([rot, x[..., rot_dim:]], axis=-1)


def swiglu(h, wg, wu, wd):
    return (jax.nn.silu(h @ wg) * (h @ wu)) @ wd


def sink_attention(q, k, v, sinks, valid):
    s = jnp.einsum('...qhgd,...khd->...hgqk', q, k).astype(jnp.float32) * (DA ** -0.5)
    if valid is not None:
        s = jnp.where(valid, s, NEG)
    sink = sinks.astype(jnp.float32).reshape(KVA, HA // KVA)[..., None, None]
    m = jnp.maximum(s.max(axis=-1, keepdims=True), sink)
    p = jnp.exp(s - m)
    p = (p / (p.sum(axis=-1, keepdims=True) + jnp.exp(sink - m))).astype(v.dtype)
    return jnp.einsum('...hgqk,...khd->...qhgd', p, v)


def swa_prompt(q, k, v, sinks):
    B, S = q.shape[:2]
    n_chunks = S // CHUNK
    qc = q.reshape(B, n_chunks, CHUNK, KVA, HA // KVA, DA)

    def band(t):
        tc = t.reshape(B, n_chunks, CHUNK, KVA, DA)
        tp = jnp.pad(tc, ((0, 0), (WIN_CHUNKS, 0), (0, 0), (0, 0), (0, 0)))
        return jnp.concatenate([tp[:, j:j + n_chunks] for j in range(WIN_CHUNKS + 1)], axis=2)

    kb, vb = band(k), band(v)
    key_chunk = (jnp.arange(n_chunks)[:, None] - WIN_CHUNKS
                 + jnp.repeat(jnp.arange(WIN_CHUNKS + 1), CHUNK)[None, :])
    valid = (key_chunk >= 0)[:, None, None, None, :]
    return sink_attention(qc, kb, vb, sinks, valid).reshape(B, S, HA * DA)


def mlstm_chunk(state, inp):
    c_prev, n_prev, m_prev = (t.astype(jnp.float32) for t in state)
    q, k, v, ig, lf = (t.astype(jnp.float32) for t in inp)
    L = q.shape[1]
    q = q * (DKB ** -0.5)
    b = jnp.cumsum(lf, axis=1)
    causal = jnp.tril(jnp.ones((L, L), bool))
    d = b[:, :, None, :] - b[:, None, :, :] + ig[:, None, :, :]
    d = jnp.where(causal[None, :, :, None], d, NEG)
    inter = b + m_prev[:, None, :]
    m_t = jnp.maximum(inter, d.max(axis=2))
    w_intra = jnp.exp(d - m_t[:, :, None, :])
    w_inter = jnp.exp(inter - m_t)
    a = w_intra * jnp.einsum('bthd,bshd->btsh', q, k)
    num = (jnp.einsum('btsh,bshv->bthv', a, v)
           + w_inter[..., None] * jnp.einsum('bthd,bhdv->bthv', q, c_prev))
    den = a.sum(axis=2) + w_inter * jnp.einsum('bthd,bhd->bth', q, n_prev)
    h = num / jnp.maximum(jnp.abs(den), jnp.exp(-m_t))[..., None]
    b_end = b[:, -1, :]
    g = b_end[:, None, :] - b + ig
    m_new = jnp.maximum(b_end + m_prev, g.max(axis=1))
    w_s = jnp.exp(g - m_new[:, None, :])
    w_c = jnp.exp(b_end + m_prev - m_new)
    c_new = w_c[..., None, None] * c_prev + jnp.einsum('bsh,bshd,bshv->bhdv', w_s, k, v)
    n_new = w_c[..., None] * n_prev + jnp.einsum('bsh,bshd->bhd', w_s, k)
    return (c_new, n_new, m_new), h


def mlstm_prompt(q, k, v, ig, lf):
    B, S = q.shape[:2]
    n_chunks = S // CHUNK

    def to_chunks(t):
        return jnp.moveaxis(t.reshape((B, n_chunks, CHUNK) + t.shape[2:]), 1, 0)

    init = (jnp.zeros((B, HB, DKB, DVB), jnp.float32),
            jnp.zeros((B, HB, DKB), jnp.float32),
            jnp.zeros((B, HB), jnp.float32))
    state, h = lax.scan(mlstm_chunk, init,
                        (to_chunks(q), to_chunks(k), to_chunks(v), to_chunks(ig), to_chunks(lf)))
    return jnp.moveaxis(h, 0, 1).reshape(B, S, HB, DVB), state


def head_rms_norm(h, g):
    y = h * lax.rsqrt(jnp.mean(h * h, axis=-1, keepdims=True) + EPS)
    return y * g.astype(jnp.float32).reshape(HB, DVB)


def even_mixer(h, pos, w_in, w_out, sinks, b_i, b_f, g_mh, past):
    B, S = h.shape[:2]
    qa, ka, va, qb, kb, vb, ib, fb, ob = split_cols(h @ w_in, EVEN_SPLITS)
    qa = apply_rope(qa.reshape(B, S, HA, DA), pos, ROT_A)
    ka = apply_rope(ka.reshape(B, S, KVA, DA), pos, ROT_A)
    va = va.reshape(B, S, KVA, DA)
    qb = qb.reshape(B, S, HB, DKB)
    kb = kb.reshape(B, S, HB, DKB)
    vb = vb.reshape(B, S, HB, DVB)
    ig = (ib + b_i).astype(jnp.float32)
    lf = jax.nn.log_sigmoid((fb + b_f).astype(jnp.float32))
    if past is None:
        oa = swa_prompt(qa, ka, va, sinks)
        new_k, new_v = ka[:, -WINDOW:], va[:, -WINDOW:]
        hb, (c_new, n_new, m_new) = mlstm_prompt(qb, kb, vb, ig, lf)
    else:
        ck, cv, c0, n0, m0 = past
        kk = jnp.concatenate([ck, ka], axis=1)
        vv = jnp.concatenate([cv, va], axis=1)
        oa = sink_attention(qa.reshape(B, S, KVA, HA // KVA, DA), kk, vv, sinks, None).reshape(B, S, HA * DA)
        new_k, new_v = kk[:, -WINDOW:], vv[:, -WINDOW:]
        (c_new, n_new, m_new), hb = mlstm_chunk((c0, n0, m0), (qb, kb, vb, ig, lf))
    o_gate = jax.nn.sigmoid(ob.astype(jnp.float32)).reshape(B, S, HB, DVB)
    hb = (head_rms_norm(hb, g_mh) * o_gate).astype(h.dtype).reshape(B, S, HB * DVB)
    out = jnp.concatenate([oa, hb], axis=-1) @ w_out
    return out, (new_k, new_v, c_new, n_new, m_new)


def dsa_attend(q, qi, wi, lim, k, v, ki, topk):
    B, Q = q.shape[:2]
    L = k.shape[1]
    rel = jax.nn.relu(jnp.einsum('bqhd,bsd->bqhs', qi, ki).astype(jnp.float32) * (DI ** -0.5))
    score = jnp.einsum('bqhs,bqh->bqs', rel, wi.astype(jnp.float32))
    admissible = jnp.arange(L)[None, :] < lim[:, None]
    score = jnp.where(admissible[None], score, NEG)
    _, sel = lax.top_k(score, topk)
    sel_valid = sel < lim[None, :, None]
    ks = jax.vmap(lambda t, i: t[i])(k, sel)
    vs = jax.vmap(lambda t, i: t[i])(v, sel)
    qg = q.reshape(B, Q, KVC, HC // KVC, DC)
    s = jnp.einsum('bqhgd,bqkhd->bqhgk', qg, ks).astype(jnp.float32) * (DC ** -0.5)
    s = jnp.where(sel_valid[:, :, None, None, :], s, NEG)
    p = jax.nn.softmax(s, axis=-1).astype(v.dtype)
    return jnp.einsum('bqhgk,bqkhd->bqhgd', p, vs).reshape(B, Q, HC * DC)


def mla_attend(q, k, v, lim):
    B, Q = q.shape[:2]
    valid = jnp.arange(k.shape[1])[None, :] < lim[:, None]
    s = jnp.einsum('bqhd,bkhd->bhqk', q, k).astype(jnp.float32) * ((DNOPE + DROPE) ** -0.5)
    p = jax.nn.softmax(jnp.where(valid[None, None], s, NEG), axis=-1).astype(v.dtype)
    return jnp.einsum('bhqk,bkhd->bqhd', p, v).reshape(B, Q, HD * DVD)


def odd_mixer(h, pos, w_in, w_out, g_ckv, w_uk, w_uv, past):
    B, S = h.shape[:2]
    qc, kc, vc, qi, ki, wi, qd, ckv, kpe = split_cols(h @ w_in, ODD_SPLITS)
    qc = apply_rope(qc.reshape(B, S, HC, DC), pos, ROT_C)
    kc = apply_rope(kc.reshape(B, S, KVC, DC), pos, ROT_C)
    vc = vc.reshape(B, S, KVC, DC)
    qi = apply_rope(qi.reshape(B, S, HI, DI), pos, ROT_I)
    ki = apply_rope(ki[:, :, None, :], pos, ROT_I)[:, :, 0]
    wi = wi * (HI ** -0.5)
    qd = qd.reshape(B, S, HD, DNOPE + DROPE)
    qd = jnp.concatenate([qd[..., :DNOPE], apply_rope(qd[..., DNOPE:], pos, DROPE)], axis=-1)
    ckv = rms_norm(ckv, g_ckv)
    kpe = apply_rope(kpe[:, :, None, :], pos, DROPE)[:, :, 0]
    if past is None:
        kc_all, vc_all, ki_all, ckv_all, kpe_all = kc, vc, ki, ckv, kpe
        lim = (pos // CHUNK + 1) * CHUNK
    else:
        pk, pv, pidx, pckv, pkpe = past
        kc_all = jnp.concatenate([pk, kc], axis=1)
        vc_all = jnp.concatenate([pv, vc], axis=1)
        ki_all = jnp.concatenate([pidx, ki], axis=1)
        ckv_all = jnp.concatenate([pckv, ckv], axis=1)
        kpe_all = jnp.concatenate([pkpe, kpe], axis=1)
        lim = jnp.full((S,), kc_all.shape[1], jnp.int32)
    L = kc_all.shape[1]
    topk = min(INDEX_TOPK, L // 4)
    kd_nope = jnp.einsum('blc,chd->blhd', ckv_all, w_uk.reshape(KV_LORA, HD, DNOPE))
    vd = jnp.einsum('blc,chd->blhd', ckv_all, w_uv.reshape(KV_LORA, HD, DVD))
    kd = jnp.concatenate([kd_nope, jnp.broadcast_to(kpe_all[:, :, None, :], (B, L, HD, DROPE))], axis=-1)

    def attend(qc_b, qi_b, wi_b, qd_b, lim_b):
        oc = dsa_attend(qc_b, qi_b, wi_b, lim_b, kc_all, vc_all, ki_all, topk)
        od = mla_attend(qd_b, kd, vd, lim_b)
        return oc, od

    if past is None:
        n_blocks = S // Q_BLOCK

        def to_blocks(t):
            return jnp.moveaxis(t.reshape((B, n_blocks, Q_BLOCK) + t.shape[2:]), 1, 0)

        oc, od = lax.map(lambda a: attend(*a),
                         (to_blocks(qc), to_blocks(qi), to_blocks(wi), to_blocks(qd),
                          lim.reshape(n_blocks, Q_BLOCK)))
        oc = jnp.moveaxis(oc, 0, 1).reshape(B, S, HC * DC)
        od = jnp.moveaxis(od, 0, 1).reshape(B, S, HD * DVD)
    else:
        oc, od = attend(qc, qi, wi, qd, lim)
    out = jnp.concatenate([oc, od], axis=-1) @ w_out
    return out, (kc, vc, ki, ckv, kpe)


def run_trunk(x, c, pos, P, past):
    even_states, odd_states = [], []
    for l in range(DEPTH):
        i = l // 2
        mod = jax.nn.silu(c) @ P['w_ada'][l] + P['b_ada'][l]
        sh1, sc1, g1, sh2, sc2, g2 = jnp.split(mod, 6, axis=-1)
        hm = modulate(x, P['g_norm_mix'][l], sh1, sc1)
        if l % 2 == 0:
            lp = None if past is None else tuple(t[i] for t in past[:5])
            out, st = even_mixer(hm, pos, P['w_in_even'][i], P['w_out_even'][i], P['sinks_a'][i],
                                 P['b_igate'][i], P['b_fgate'][i], P['g_mlstm'][i], lp)
            even_states.append(st)
        else:
            lp = None if past is None else tuple(t[i] for t in past[5:])
            out, st = odd_mixer(hm, pos, P['w_in_odd'][i], P['w_out_odd'][i], P['g_ckv'][i],
                                P['w_uk'][i], P['w_uv'][i], lp)
            odd_states.append(st)
        x = x + g1[:, None, :] * out
        hf = modulate(x, P['g_norm_ffn'][l], sh2, sc2)
        x = x + g2[:, None, :] * swiglu(hf, P['w_ffn_gate'][l], P['w_ffn_up'][l], P['w_ffn_down'][l])
    y = rms_norm(x, P['g_final'])
    new_state = ([jnp.stack([s[j] for s in even_states]) for j in range(5)]
                 + [jnp.stack([s[j] for s in odd_states]) for j in range(5)])
    return y, new_state


def setup_inputs(seed: int = 0) -> dict:
    key = jax.random.key(seed)
    ks = jax.random.split(key, 33)
    D = D_MODEL

    def nrm(i, shape, scale=1.0):
        return jax.random.normal(ks[i], shape, jnp.float32) * scale

    def gain(i, shape):
        return 1.0 + nrm(i, shape, 0.1)

    return {
        'x_prompt': nrm(0, (BATCH, SEQ, D)),
        'x_sample': nrm(1, (DEC_BATCH, DEC_SEQ, D)),
        'c_prompt': nrm(2, (BATCH, D)),
        'c_sample': nrm(3, (DEC_BATCH, D)),
        'cache_a_k': nrm(4, (N_PAIRS, DEC_BATCH, WINDOW, KVA, DA)),
        'cache_a_v': nrm(5, (N_PAIRS, DEC_BATCH, WINDOW, KVA, DA)),
        'state_b_c': nrm(6, (N_PAIRS, DEC_BATCH, HB, DKB, DVB)),
        'state_b_n': nrm(7, (N_PAIRS, DEC_BATCH, HB, DKB)),
        'state_b_m': nrm(8, (N_PAIRS, DEC_BATCH, HB), 0.5),
        'cache_c_k': nrm(9, (N_PAIRS, DEC_BATCH, PAST_LEN, KVC, DC)),
        'cache_c_v': nrm(10, (N_PAIRS, DEC_BATCH, PAST_LEN, KVC, DC)),
        'cache_c_idx': nrm(11, (N_PAIRS, DEC_BATCH, PAST_LEN, DI)),
        'cache_d_ckv': nrm(12, (N_PAIRS, DEC_BATCH, PAST_LEN, KV_LORA)),
        'cache_d_kpe': nrm(13, (N_PAIRS, DEC_BATCH, PAST_LEN, DROPE)),
        'w_ada': nrm(14, (DEPTH, D, 6 * D), 0.5 * D ** -0.5),
        'b_ada': nrm(15, (DEPTH, 6 * D), 0.02),
        'g_norm_mix': gain(16, (DEPTH, D)),
        'g_norm_ffn': gain(17, (DEPTH, D)),
        'w_in_even': nrm(18, (N_PAIRS, D, EVEN_IN), D ** -0.5),
        'w_out_even': nrm(19, (N_PAIRS, EVEN_MIX, D), EVEN_MIX ** -0.5),
        'sinks_a': nrm(20, (N_PAIRS, HA)),
        'b_igate': nrm(21, (N_PAIRS, HB), 0.1),
        'b_fgate': 3.0 + nrm(22, (N_PAIRS, HB), 0.5),
        'g_mlstm': gain(23, (N_PAIRS, HB * DVB)),
        'w_in_odd': nrm(24, (N_PAIRS, D, ODD_IN), D ** -0.5),
        'w_out_odd': nrm(25, (N_PAIRS, ODD_MIX, D), ODD_MIX ** -0.5),
        'g_ckv': gain(26, (N_PAIRS, KV_LORA)),
        'w_uk': nrm(27, (N_PAIRS, KV_LORA, HD * DNOPE), KV_LORA ** -0.5),
        'w_uv': nrm(28, (N_PAIRS, KV_LORA, HD * DVD), KV_LORA ** -0.5),
        'w_ffn_gate': nrm(29, (DEPTH, D, D_FF), D ** -0.5),
        'w_ffn_up': nrm(30, (DEPTH, D, D_FF), D ** -0.5),
        'w_ffn_down': nrm(31, (DEPTH, D_FF, D), D_FF ** -0.5),
        'g_final': gain(32, (D,)),
    }


def reference(x_prompt, x_sample, c_prompt, c_sample,
              cache_a_k, cache_a_v, state_b_c, state_b_n, state_b_m,
              cache_c_k, cache_c_v, cache_c_idx, cache_d_ckv, cache_d_kpe,
              w_ada, b_ada, g_norm_mix, g_norm_ffn,
              w_in_even, w_out_even, sinks_a, b_igate, b_fgate, g_mlstm,
              w_in_odd, w_out_odd, g_ckv, w_uk, w_uv,
              w_ffn_gate, w_ffn_up, w_ffn_down, g_final):
    P = dict(w_ada=w_ada, b_ada=b_ada, g_norm_mix=g_norm_mix, g_norm_ffn=g_norm_ffn,
             w_in_even=w_in_even, w_out_even=w_out_even, sinks_a=sinks_a, b_igate=b_igate,
             b_fgate=b_fgate, g_mlstm=g_mlstm, w_in_odd=w_in_odd, w_out_odd=w_out_odd,
             g_ckv=g_ckv, w_uk=w_uk, w_uv=w_uv, w_ffn_gate=w_ffn_gate, w_ffn_up=w_ffn_up,
             w_ffn_down=w_ffn_down, g_final=g_final)
    pos_p = jnp.arange(x_prompt.shape[1], dtype=jnp.int32)
    past_len = cache_c_k.shape[2]
    pos_s = past_len + jnp.arange(x_sample.shape[1], dtype=jnp.int32)
    y_prompt, st_p = run_trunk(x_prompt, c_prompt, pos_p, P, None)
    past = (cache_a_k, cache_a_v, state_b_c, state_b_n, state_b_m,
            cache_c_k, cache_c_v, cache_c_idx, cache_d_ckv, cache_d_kpe)
    y_sample, st_s = run_trunk(x_sample, c_sample, pos_s, P, past)
    p_a_k, p_a_v, p_b_c, p_b_n, p_b_m, p_c_k, p_c_v, p_c_idx, p_d_ckv, p_d_kpe = st_p
    s_a_k, s_a_v, s_b_c, s_b_n, s_b_m, s_c_k, s_c_v, s_c_idx, s_d_ckv, s_d_kpe = st_s
    return (y_prompt, y_sample,
            p_a_k, p_a_v, p_b_c, p_b_n, p_b_m, p_c_k, p_c_v, p_c_idx, p_d_ckv, p_d_kpe,
            s_a_k, s_a_v, s_b_c, s_b_n, s_b_m, s_c_k, s_c_v, s_c_idx, s_d_ckv, s_d_kpe)
```

```python
import functools
import math

import jax
import jax.numpy as jnp
import numpy as np
from jax import lax
from jax.experimental import pallas as pl
from jax.experimental.pallas import tpu as pltpu

F32 = jnp.float32
BF16 = jnp.bfloat16
I32 = jnp.int32

CHUNK = 64
ROPE_THETA = 500000.0
EPS = 1e-6
NEG = -1e30
LANES = 128

HA, KVA, DA = 16, 2, 64
ROT_A = DA // 4
WINDOW = 128
WIN_CHUNKS = WINDOW // CHUNK
HB, DKB, DVB = 4, 128, 256
HC, KVC, DC = 8, 2, 128
ROT_C = DC // 4
HI, DI = 16, 64
ROT_I = DI // 4
INDEX_TOPK = 256
HD, DNOPE, DROPE, DVD, KV_LORA = 8, 128, 64, 128, 512

VMEM_LIMIT = 56 * 1024 * 1024


def _cparams(sem, vmem=VMEM_LIMIT):
    return pltpu.CompilerParams(dimension_semantics=sem, vmem_limit_bytes=vmem)


def _pick(n, options):
    for o in options:
        if n % o == 0:
            return o
    raise ValueError(f"no tile in {options} divides {n}")


def _resident(shape, index_map):
    return pl.BlockSpec(shape, index_map, pipeline_mode=pl.Buffered(1))


def _dot(a, b):
    return jnp.dot(a, b, preferred_element_type=F32)


def _dot_nt(a, b):
    return lax.dot_general(a, b, (((1,), (1,)), ((), ())), preferred_element_type=F32)


def _dot_tn(a, b):
    return lax.dot_general(a, b, (((0,), (0,)), ((), ())), preferred_element_type=F32)


def _modulate(x, g, sc, sh):
    tm, d = x.shape
    y = x * lax.rsqrt(jnp.mean(x * x, axis=-1, keepdims=True) + EPS) * g
    y = y.reshape(tm // CHUNK, CHUNK, d) * (1.0 + sc[:, None, :]) + sh[:, None, :]
    return y.reshape(tm, d)


def _gate_rows(y, g):
    tm, d = y.shape
    return (y.reshape(tm // CHUNK, CHUNK, d) * g[:, None, :]).reshape(tm, d)


def _rope(x, cos, sp, sm, half):
    blocks = []
    for j in range(x.shape[1] // LANES):
        xb = x[:, j * LANES:(j + 1) * LANES]
        blocks.append(xb * cos + pltpu.roll(xb, half, 1) * sp + pltpu.roll(xb, LANES - half, 1) * sm)
    return blocks[0] if len(blocks) == 1 else jnp.concatenate(blocks, axis=1)


def _rope_tables(pos, rot, head):
    half = rot // 2
    inv = ROPE_THETA ** (-jnp.arange(half, dtype=F32) * 2.0 / rot)
    ang = pos.astype(F32)[:, None] * inv[None, :]
    cos, sin = jnp.cos(ang), jnp.sin(ang)
    p = pos.shape[0]
    one = jnp.ones((p, head - rot), F32)
    zero = jnp.zeros((p, head - rot), F32)
    zh = jnp.zeros((p, half), F32)
    c = jnp.concatenate([cos, cos, one], axis=1)
    s_plus = jnp.concatenate([zh, sin, zero], axis=1)
    s_minus = jnp.concatenate([-sin, zh, zero], axis=1)
    rep = LANES // head
    return tuple(jnp.tile(t, (1, rep)) for t in (c, s_plus, s_minus))


def _ada_kernel(c_ref, w_ref, b_ref, o_ref):
    c = c_ref[...]
    h = (c * jax.nn.sigmoid(c)).astype(BF16)
    o_ref[...] = _dot(h, w_ref[...].astype(BF16)) + b_ref[...]


def _ada(c, w_ada, b_ada):
    depth, d, n = w_ada.shape
    ns = c.shape[0]
    tn = _pick(n, (1024, 512, 256, 128))
    return pl.pallas_call(
        _ada_kernel,
        grid=(depth, n // tn),
        in_specs=[pl.BlockSpec((ns, d), lambda l, j: (0, 0)),
                  pl.BlockSpec((None, d, tn), lambda l, j: (l, 0, j)),
                  pl.BlockSpec((None, 1, tn), lambda l, j: (l, 0, j))],
        out_specs=pl.BlockSpec((None, ns, tn), lambda l, j: (l, 0, j)),
        out_shape=jax.ShapeDtypeStruct((depth, ns, n), F32),
        compiler_params=_cparams(("arbitrary", "arbitrary")),
        name="ada",
    )(c, w_ada, b_ada.reshape(depth, 1, n))


def _col_tiles(lo, hi, step=512):
    return [(a, min(a + step, hi)) for a in range(lo, hi, step)]


E_QA, E_KA, E_VA, E_QB, E_KB, E_VB, E_GT, E_OB, E_END = 0, 1024, 1152, 1280, 1792, 2304, 3328, 3456, 4480


def _pad_cols(w, width):
    return jnp.pad(w, ((0, 0), (0, width - w.shape[1])))


def _even_weight(w):
    qa, ka, va, qb, kb, vb, ib, fb, ob = jnp.split(
        w, np.cumsum((HA * DA, KVA * DA, KVA * DA, HB * DKB, HB * DKB, HB * DVB, HB, HB))[:].tolist(), axis=1)
    gates = _pad_cols(jnp.concatenate([ib, fb], axis=1), LANES)
    return jnp.concatenate([qa, ka, va, qb, kb, vb, gates, ob], axis=1).astype(BF16)


def _proj_even_kernel(x_ref, g_ref, sc_ref, sh_ref, w_ref, gb_ref, cos_ref, sp_ref, sm_ref,
                      qa_o, ka_o, va_o, qb_o, kb_o, vb_o, gt_o, og_o):
    h = _modulate(x_ref[...], g_ref[...], sc_ref[...], sh_ref[...]).astype(BF16)
    cos, sp, sm = cos_ref[...], sp_ref[...], sm_ref[...]
    half = ROT_A // 2

    def mm(lo, hi):
        return _dot(h, w_ref[:, lo:hi])

    for lo, hi in _col_tiles(E_QA, E_KA):
        qa_o[:, lo - E_QA:hi - E_QA] = _rope(mm(lo, hi), cos, sp, sm, half).astype(qa_o.dtype)
    ka_o[...] = _rope(mm(E_KA, E_VA), cos, sp, sm, half)
    va_o[...] = mm(E_VA, E_QB)
    for lo, hi in _col_tiles(E_QB, E_KB):
        qb_o[:, lo - E_QB:hi - E_QB] = (mm(lo, hi) * (DKB ** -0.5)).astype(qb_o.dtype)
    for lo, hi in _col_tiles(E_KB, E_VB):
        kb_o[:, lo - E_KB:hi - E_KB] = mm(lo, hi).astype(kb_o.dtype)
    for lo, hi in _col_tiles(E_VB, E_GT):
        vb_o[:, lo - E_VB:hi - E_VB] = mm(lo, hi).astype(vb_o.dtype)
    t = mm(E_GT, E_OB) + gb_ref[...]
    lane = lax.broadcasted_iota(I32, t.shape, 1)
    gt_o[...] = jnp.where(lane < HB, t, jax.nn.log_sigmoid(t))
    for lo, hi in _col_tiles(E_OB, E_END):
        og_o[:, lo - E_OB:hi - E_OB] = jax.nn.sigmoid(mm(lo, hi)).astype(og_o.dtype)


(O_QC, O_KC, O_VC, O_QI, O_KI, O_WI, O_QDN, O_QDP, O_CKV, O_KPE, O_END) = (
    0, 1024, 1280, 1536, 2560, 2688, 2816, 3840, 4352, 4864, 4992)


def _odd_weight(w):
    qc, kc, vc, qi, ki, wi, qd, ckv, kpe = jnp.split(
        w, np.cumsum((HC * DC, KVC * DC, KVC * DC, HI * DI, DI, HI, HD * (DNOPE + DROPE), KV_LORA)).tolist(), axis=1)
    d = w.shape[0]
    qd = qd.reshape(d, HD, DNOPE + DROPE)
    qdn = qd[:, :, :DNOPE].reshape(d, HD * DNOPE)
    qdp = qd[:, :, DNOPE:].reshape(d, HD * DROPE)
    return jnp.concatenate([qc, kc, vc, qi, _pad_cols(ki, LANES), _pad_cols(wi, LANES), qdn, qdp, ckv,
                            _pad_cols(kpe, LANES)], axis=1).astype(BF16)


def _proj_odd_kernel(x_ref, g_ref, sc_ref, sh_ref, w_ref, gckv_ref,
                     ci_ref, spi_ref, smi_ref, cc_ref, spc_ref, smc_ref, cd_ref, spd_ref, smd_ref,
                     qc_o, kc_o, vc_o, qi_o, ki_o, wi_o, qdn_o, qdp_o, ckv_o, kpe_o):
    h = _modulate(x_ref[...], g_ref[...], sc_ref[...], sh_ref[...]).astype(BF16)
    tab_i = (ci_ref[...], spi_ref[...], smi_ref[...], ROT_I // 2)
    tab_c = (cc_ref[...], spc_ref[...], smc_ref[...], ROT_C // 2)
    tab_d = (cd_ref[...], spd_ref[...], smd_ref[...], DROPE // 2)

    def mm(lo, hi):
        return _dot(h, w_ref[:, lo:hi])

    for lo, hi in _col_tiles(O_QC, O_KC):
        qc_o[:, lo - O_QC:hi - O_QC] = _rope(mm(lo, hi), *tab_c).astype(qc_o.dtype)
    kc_o[...] = _rope(mm(O_KC, O_VC), *tab_c)
    vc_o[...] = mm(O_VC, O_QI)
    for lo, hi in _col_tiles(O_QI, O_KI):
        qi_o[:, lo - O_QI:hi - O_QI] = _rope(mm(lo, hi), *tab_i).astype(qi_o.dtype)
    ki_o[...] = _rope(mm(O_KI, O_WI), *tab_i)[:, :DI]
    wi_o[...] = mm(O_WI, O_QDN) * (HI ** -0.5 * DI ** -0.5)
    for lo, hi in _col_tiles(O_QDN, O_QDP):
        qdn_o[:, lo - O_QDN:hi - O_QDN] = mm(lo, hi).astype(qdn_o.dtype)
    qdp_o[...] = _rope(mm(O_QDP, O_CKV), *tab_d).astype(qdp_o.dtype)
    c = mm(O_CKV, O_KPE)
    ckv_o[...] = c * lax.rsqrt(jnp.mean(c * c, axis=-1, keepdims=True) + EPS) * gckv_ref[...]
    kpe_o[...] = _rope(mm(O_KPE, O_END), *tab_d)[:, :DROPE]


def _token_tile(t):
    return _pick(t, (512,))


def _proj_in(kernel_fn, name, x, modc, layer, g, w, extra, tables, tab_map, outs):
    t, d = x.shape
    tm = _token_tile(t)
    rows = tm // CHUNK
    in_specs = [pl.BlockSpec((tm, d), lambda i: (i, 0)),
                _resident((1, d), lambda i: (0, 0)),
                pl.BlockSpec((None, rows, d), lambda i: (layer, i, 1)),
                pl.BlockSpec((None, rows, d), lambda i: (layer, i, 0)),
                _resident(w.shape, lambda i: (0, 0)),
                _resident(extra.shape, lambda i: (0, 0))]
    in_specs += [pl.BlockSpec((tm, LANES), lambda i: (tab_map(i), 0)) for _ in tables]
    return pl.pallas_call(
        kernel_fn,
        grid=(t // tm,),
        in_specs=in_specs,
        out_specs=[pl.BlockSpec((tm, wd), lambda i: (i, 0)) for wd, _ in outs],
        out_shape=[jax.ShapeDtypeStruct((t, wd), dt) for wd, dt in outs],
        compiler_params=_cparams(("arbitrary",)),
        name=name,
    )(x, g, modc, modc, w, extra, *tables)


def _swa_kernel(n_prompt_chunks, seq_chunks, sink_ref, q_ref, k0_ref, k1_ref, k2_ref,
                v0_ref, v1_ref, v2_ref, o_ref):
    g = pl.program_id(0)
    c = g % seq_chunks
    is_prompt = g < n_prompt_chunks
    kb = jnp.concatenate([k0_ref[...], k1_ref[...], k2_ref[...]], axis=0).astype(BF16)
    vb = jnp.concatenate([v0_ref[...], v1_ref[...], v2_ref[...]], axis=0).astype(BF16)
    first_valid = jnp.where(is_prompt, (WIN_CHUNKS - jnp.minimum(c, WIN_CHUNKS)) * CHUNK, 0)
    col = lax.broadcasted_iota(I32, (CHUNK, (WIN_CHUNKS + 1) * CHUNK), 1)
    valid = col >= first_valid
    group = HA // KVA
    for hq in range(HA):
        kv = hq // group
        q = q_ref[:, hq * DA:(hq + 1) * DA]
        s = _dot_nt(q, kb[:, kv * DA:(kv + 1) * DA]) * (DA ** -0.5)
        s = jnp.where(valid, s, NEG)
        sink = sink_ref[hq]
        m = jnp.maximum(jnp.max(s, axis=-1, keepdims=True), sink)
        p = jnp.exp(s - m)
        den = jnp.sum(p, axis=-1, keepdims=True) + jnp.exp(sink - m)
        o = _dot((p / den).astype(BF16), vb[:, kv * DA:(kv + 1) * DA])
        o_ref[:, hq * DA:(hq + 1) * DA] = o.astype(o_ref.dtype)


def _swa(qa, k_all, v_all, sinks, n_prompt_chunks, seq_chunks):
    t = qa.shape[0]
    nch = t // CHUNK
    band = WIN_CHUNKS + 1

    def kmap(j):
        def f(g):
            prompt_id = jnp.maximum(g - WIN_CHUNKS + j, 0)
            sample_id = n_prompt_chunks + band * (g - n_prompt_chunks) + j
            return (jnp.where(g < n_prompt_chunks, prompt_id, sample_id), 0)
        return f

    kv_specs = [pl.BlockSpec((CHUNK, KVA * DA), kmap(j)) for j in range(band)]
    return pl.pallas_call(
        functools.partial(_swa_kernel, n_prompt_chunks, seq_chunks),
        grid=(nch,),
        in_specs=[pl.BlockSpec(memory_space=pltpu.SMEM),
                  pl.BlockSpec((CHUNK, HA * DA), lambda g: (g, 0))] + kv_specs + kv_specs,
        out_specs=pl.BlockSpec((CHUNK, HA * DA), lambda g: (g, 0)),
        out_shape=jax.ShapeDtypeStruct((t, HA * DA), BF16),
        compiler_params=_cparams(("arbitrary",)),
        name="swa",
    )(sinks, qa, k_all, k_all, k_all, v_all, v_all, v_all)


def _mlstm_kernel(n_prompt_chunks, seq_chunks, q_ref, k_ref, v_ref, gt_ref, og_ref, gmh_ref,
                  c0_ref, n0_ref, m0_ref, h_o, c_o, n_o, m_o, c_s, n_s, m_s):
    g = pl.program_id(0)
    is_prompt = g < n_prompt_chunks
    c_idx = g % seq_chunks
    first = jnp.logical_or(jnp.logical_not(is_prompt), c_idx == 0)
    last = jnp.logical_or(jnp.logical_not(is_prompt), c_idx == seq_chunks - 1)

    @pl.when(first)
    def _():
        c_s[...] = c0_ref[...]
        n_s[...] = n0_ref[...]
        m_s[...] = m0_ref[...]

    gt = gt_ref[...]
    gt_t = gt.T
    row = lax.broadcasted_iota(I32, (CHUNK, CHUNK), 0)
    col = lax.broadcasted_iota(I32, (CHUNK, CHUNK), 1)
    causal = col <= row
    for hh in range(HB):
        ig_row = gt_t[hh:hh + 1, :]
        lf_row = gt_t[HB + hh:HB + hh + 1, :]
        ig_col = gt[:, hh:hh + 1]
        lf_col = gt[:, HB + hh:HB + hh + 1]
        b_col = jnp.sum(jnp.where(causal, lf_row, 0.0), axis=1, keepdims=True)
        b_row = jnp.sum(jnp.where(row <= col, lf_col, 0.0), axis=0, keepdims=True)
        m_prev = m_s[:, hh:hh + 1]
        d = jnp.where(causal, b_col - b_row + ig_row, NEG)
        inter = b_col + m_prev
        m_t = jnp.maximum(inter, jnp.max(d, axis=1, keepdims=True))
        w_intra = jnp.exp(d - m_t)
        w_inter = jnp.exp(inter - m_t)
        q = q_ref[:, hh * DKB:(hh + 1) * DKB]
        k = k_ref[:, hh * DKB:(hh + 1) * DKB]
        v = v_ref[:, hh * DVB:(hh + 1) * DVB]
        c_prev = c_s[hh]
        n_prev = n_s[hh:hh + 1, :]
        a = w_intra * _dot_nt(q, k)
        num = _dot(a.astype(BF16), v) + w_inter * _dot(q, c_prev.astype(BF16))
        den = (jnp.sum(a, axis=1, keepdims=True)
               + w_inter * jnp.sum(q.astype(F32) * n_prev, axis=1, keepdims=True))
        hv = num / jnp.maximum(jnp.abs(den), jnp.exp(-m_t))
        b_end = b_col[CHUNK - 1:CHUNK, :]
        g_row = b_end - b_row + ig_row
        g_col = b_end - b_col + ig_col
        m_new = jnp.maximum(b_end + m_prev, jnp.max(g_row, axis=1, keepdims=True))
        w_s = jnp.exp(g_col - m_new)
        w_c = jnp.exp(b_end + m_prev - m_new)
        c_s[hh] = w_c * c_prev + _dot_tn(k, (w_s * v.astype(F32)).astype(BF16))
        n_s[hh:hh + 1, :] = w_c * n_prev + jnp.sum(w_s * k.astype(F32), axis=0, keepdims=True)
        m_s[:, hh:hh + 1] = m_new
        y = hv * lax.rsqrt(jnp.mean(hv * hv, axis=-1, keepdims=True) + EPS)
        y = y * gmh_ref[:, hh * DVB:(hh + 1) * DVB] * og_ref[:, hh * DVB:(hh + 1) * DVB].astype(F32)
        h_o[:, hh * DVB:(hh + 1) * DVB] = y.astype(h_o.dtype)

    @pl.when(last)
    def _():
        c_o[...] = c_s[...]
        n_o[...] = n_s[...]
        m_o[...] = m_s[...]


def _mlstm(qb, kb, vb, gates, og, g_mh, c0, n0, m0, n_prompt_chunks, seq_chunks, n_prompt_seqs):
    t = qb.shape[0]
    nch = t // CHUNK
    ns = c0.shape[0]

    def seq(g):
        return jnp.where(g < n_prompt_chunks, g // seq_chunks, n_prompt_seqs + g - n_prompt_chunks)

    tok = lambda w: pl.BlockSpec((CHUNK, w), lambda g: (g, 0))
    st_specs = [pl.BlockSpec((None, HB, DKB, DVB), lambda g: (seq(g), 0, 0, 0)),
                pl.BlockSpec((None, HB, DKB), lambda g: (seq(g), 0, 0)),
                pl.BlockSpec((None, 1, HB), lambda g: (seq(g), 0, 0))]
    return pl.pallas_call(
        functools.partial(_mlstm_kernel, n_prompt_chunks, seq_chunks),
        grid=(nch,),
        in_specs=[tok(HB * DKB), tok(HB * DKB), tok(HB * DVB), tok(LANES), tok(HB * DVB),
                  _resident((1, HB * DVB), lambda g: (0, 0))] + st_specs,
        out_specs=[tok(HB * DVB)] + st_specs,
        out_shape=[jax.ShapeDtypeStruct((t, HB * DVB), BF16),
                   jax.ShapeDtypeStruct((ns, HB, DKB, DVB), F32),
                   jax.ShapeDtypeStruct((ns, HB, DKB), F32),
                   jax.ShapeDtypeStruct((ns, 1, HB), F32)],
        scratch_shapes=[pltpu.VMEM((HB, DKB, DVB), F32), pltpu.VMEM((HB, DKB), F32),
                        pltpu.VMEM((1, HB), F32)],
        compiler_params=_cparams(("arbitrary",)),
        name="mlstm",
    )(qb, kb, vb, gates, og, g_mh, c0, n0, m0)


INT_MIN = -2 ** 31


def _order_key(x):
    b = pltpu.bitcast(x, I32)
    return jnp.where(b >= 0, b, b ^ jnp.int32(0x7FFFFFFF))


def _dsa_kernel(tq, kblk, l_valid, l_pad, topk, causal,
                qc_ref, qi_ref, wi_ref, k_ref, v_ref, ki_ref, o_ref, key_s):
    i = pl.program_id(1)
    q0 = i * tq
    rowq = lax.broadcasted_iota(I32, (tq, 1), 0)
    if causal:
        lim = ((q0 + rowq) // CHUNK + 1) * CHUNK
        nkb = (q0 + tq + kblk - 1) // kblk
    else:
        lim = jnp.full((tq, 1), l_valid, I32)
        nkb = l_pad // kblk
    lane = lax.broadcasted_iota(I32, (tq, kblk), 1)

    wi = wi_ref[...]

    def score_block(kb, carry):
        start = pl.multiple_of(kb * kblk, kblk)
        kib = ki_ref[pl.ds(start, kblk), :]
        acc = jnp.zeros((tq, kblk), F32)
        for hh in range(HI):
            s = _dot_nt(qi_ref[:, hh * DI:(hh + 1) * DI], kib)
            acc = acc + wi[:, hh:hh + 1] * jnp.maximum(s, 0.0)
        acc = acc + 0.0
        acc = jnp.where(start + lane < lim, acc, NEG)
        key_s[kb] = _order_key(acc)
        return carry

    lax.fori_loop(0, nkb, score_block, 0)

    def count(pred_fn):
        def blk(kb, cnt):
            start = pl.multiple_of(kb * kblk, kblk)
            keys = key_s[kb]
            return cnt + jnp.sum(pred_fn(keys, start + lane).astype(I32), axis=1, keepdims=True)
        return lax.fori_loop(0, nkb, blk, jnp.zeros((tq, 1), I32))

    def thr_bit(b, thr):
        cand = thr + lax.shift_left(jnp.int32(1), 31 - b)
        cnt = count(lambda keys, idx: keys >= cand)
        return jnp.where(cnt >= topk, cand, thr)

    thr = lax.fori_loop(0, 32, thr_bit, jnp.full((tq, 1), INT_MIN, I32))

    n_above = count(lambda keys, idx: keys > thr)
    need = topk - n_above
    nbits = max(1, (l_pad - 1).bit_length())

    def cut_bit(b, cut):
        cand = cut + lax.shift_left(jnp.int32(1), nbits - 1 - b)
        cnt = count(lambda keys, idx: jnp.logical_and(keys == thr, idx < cand))
        return jnp.where(cnt < need, cand, cut)

    cut = lax.fori_loop(0, nbits, cut_bit, jnp.zeros((tq, 1), I32))

    group = HC // KVC
    qs = [jnp.concatenate([qc_ref[:, (kv * group + j) * DC:(kv * group + j + 1) * DC]
                           for j in range(group)], axis=0) for kv in range(KVC)]

    def attend_block(kb, carry):
        start = pl.multiple_of(kb * kblk, kblk)
        keys = key_s[kb]
        idx = start + lane
        sel = jnp.logical_or(keys > thr, jnp.logical_and(keys == thr, idx <= cut))
        sel = jnp.logical_and(sel, idx < lim)
        sel = jnp.concatenate([sel] * group, axis=0)
        new = []
        for kv in range(KVC):
            m, l, acc = carry[kv]
            kk = k_ref[pl.ds(start, kblk), kv * DC:(kv + 1) * DC]
            vv = v_ref[pl.ds(start, kblk), kv * DC:(kv + 1) * DC]
            s = jnp.where(sel, _dot_nt(qs[kv], kk) * (DC ** -0.5), NEG)
            m_new = jnp.maximum(m, jnp.max(s, axis=1, keepdims=True))
            alpha = jnp.exp(m - m_new)
            p = jnp.where(sel, jnp.exp(s - m_new), 0.0)
            l = alpha * l + jnp.sum(p, axis=1, keepdims=True)
            acc = alpha * acc + _dot(p.astype(BF16), vv)
            new.append((m_new, l, acc))
        return tuple(new)

    init = tuple((jnp.full((group * tq, 1), NEG, F32), jnp.zeros((group * tq, 1), F32),
                  jnp.zeros((group * tq, DC), F32)) for _ in range(KVC))
    res = lax.fori_loop(0, nkb, attend_block, init)
    for kv in range(KVC):
        _, l, acc = res[kv]
        o = acc / l
        for j in range(group):
            hq = kv * group + j
            o_ref[:, hq * DC:(hq + 1) * DC] = o[j * tq:(j + 1) * tq, :].astype(o_ref.dtype)


def _dsa(qc, qi, wi, k, v, ki, nseq, sq, l_valid, l_pad, causal, tq, name):
    topk = min(INDEX_TOPK, l_valid // 4)
    kblk = _pick(l_pad, (512, 256, 128))
    assert kblk >= topk and sq % tq == 0
    nq = sq // tq
    qmap = lambda b, i: (b * nq + i, 0)
    kmap = lambda b, i: (b, 0)
    return pl.pallas_call(
        functools.partial(_dsa_kernel, tq, kblk, l_valid, l_pad, topk, causal),
        grid=(nseq, nq),
        in_specs=[pl.BlockSpec((tq, HC * DC), qmap), pl.BlockSpec((tq, HI * DI), qmap),
                  pl.BlockSpec((tq, LANES), qmap),
                  _resident((l_pad, KVC * DC), kmap), _resident((l_pad, KVC * DC), kmap),
                  _resident((l_pad, DI), kmap)],
        out_specs=pl.BlockSpec((tq, HC * DC), qmap),
        out_shape=jax.ShapeDtypeStruct((nseq * sq, HC * DC), BF16),
        scratch_shapes=[pltpu.VMEM((l_pad // kblk, tq, kblk), I32)],
        compiler_params=_cparams(("arbitrary", "arbitrary")),
        name=name,
    )(qc, qi, wi, k, v, ki)


def _matmul_kernel(a_ref, w_ref, o_ref):
    o_ref[...] = _dot(a_ref[...], w_ref[...]).astype(o_ref.dtype)


def _latent_up(ckv, w_ukv):
    r = ckv.shape[0]
    tm = _pick(r, (1024, 512, 256, 128, 64))
    n = w_ukv.shape[1]
    return pl.pallas_call(
        _matmul_kernel,
        grid=(r // tm,),
        in_specs=[pl.BlockSpec((tm, KV_LORA), lambda i: (i, 0)), _resident(w_ukv.shape, lambda i: (0, 0))],
        out_specs=pl.BlockSpec((tm, n), lambda i: (i, 0)),
        out_shape=jax.ShapeDtypeStruct((r, n), BF16),
        compiler_params=_cparams(("arbitrary",)),
        name="latent_up",
    )(ckv, w_ukv)


def _mla_last_block(i, tq, kblk):
    return (i * tq + tq - 1) // kblk


def _mla_kernel(tq, kblk, l_valid, causal, qn_ref, qp_ref, kv_ref, kpe_ref, o_ref, m_s, l_s, acc_s):
    i = pl.program_id(1)
    j = pl.program_id(2)
    nk = pl.num_programs(2)

    @pl.when(j == 0)
    def _():
        m_s[...] = jnp.full(m_s.shape, NEG, F32)
        l_s[...] = jnp.zeros(l_s.shape, F32)
        acc_s[...] = jnp.zeros(acc_s.shape, F32)

    last = _mla_last_block(i, tq, kblk) if causal else nk - 1

    @pl.when(j <= last)
    def _():
        idx = j * kblk + lax.broadcasted_iota(I32, (tq, kblk), 1)
        if causal:
            rowq = i * tq + lax.broadcasted_iota(I32, (tq, kblk), 0)
            valid = idx < (rowq // CHUNK + 1) * CHUNK
        else:
            valid = idx < l_valid
        kpe = kpe_ref[...]
        scale = (DNOPE + DROPE) ** -0.5
        for hh in range(HD):
            kn = kv_ref[:, hh * DNOPE:(hh + 1) * DNOPE]
            vv = kv_ref[:, HD * DNOPE + hh * DVD:HD * DNOPE + (hh + 1) * DVD]
            s = (_dot_nt(qn_ref[:, hh * DNOPE:(hh + 1) * DNOPE], kn)
                 + _dot_nt(qp_ref[:, hh * DROPE:(hh + 1) * DROPE], kpe)) * scale
            s = jnp.where(valid, s, NEG)
            m = m_s[hh]
            m_new = jnp.maximum(m, jnp.max(s, axis=1, keepdims=True))
            alpha = jnp.exp(m - m_new)
            p = jnp.where(valid, jnp.exp(s - m_new), 0.0)
            l_s[hh] = alpha * l_s[hh] + jnp.sum(p, axis=1, keepdims=True)
            acc_s[hh] = alpha * acc_s[hh] + _dot(p.astype(BF16), vv)
            m_s[hh] = m_new

    @pl.when(j == nk - 1)
    def _():
        for hh in range(HD):
            o_ref[:, hh * DVD:(hh + 1) * DVD] = (acc_s[hh] / l_s[hh]).astype(o_ref.dtype)


def _mla(qn, qp, kv_up, kpe, nseq, sq, l_valid, l_pad, causal, tq, name):
    kblk = _pick(l_pad, (512, 256, 128))
    nq, nk = sq // tq, l_pad // kblk
    qmap = lambda b, i, j: (b * nq + i, 0)
    if causal:
        kmap = lambda b, i, j: (b * nk + jnp.minimum(j, _mla_last_block(i, tq, kblk)), 0)
    else:
        kmap = lambda b, i, j: (b * nk + j, 0)
    return pl.pallas_call(
        functools.partial(_mla_kernel, tq, kblk, l_valid, causal),
        grid=(nseq, nq, nk),
        in_specs=[pl.BlockSpec((tq, HD * DNOPE), qmap), pl.BlockSpec((tq, HD * DROPE), qmap),
                  pl.BlockSpec((kblk, HD * (DNOPE + DVD)), kmap), pl.BlockSpec((kblk, DROPE), kmap)],
        out_specs=pl.BlockSpec((tq, HD * DVD), qmap),
        out_shape=jax.ShapeDtypeStruct((nseq * sq, HD * DVD), BF16),
        scratch_shapes=[pltpu.VMEM((HD, tq, 1), F32), pltpu.VMEM((HD, tq, 1), F32),
                        pltpu.VMEM((HD, tq, DVD), F32)],
        compiler_params=_cparams(("arbitrary", "arbitrary", "arbitrary")),
        name=name,
    )(qn, qp, kv_up, kpe)


def _proj_out_kernel(a1_ref, a2_ref, w1_ref, w2_ref, x_ref, gate_ref, o_ref):
    y = _dot(a1_ref[...], w1_ref[...]) + _dot(a2_ref[...], w2_ref[...])
    o_ref[...] = x_ref[...] + _gate_rows(y, gate_ref[...])


def _proj_out(a1, a2, w_out, x, modc, layer):
    t, d = x.shape
    tm = _token_tile(t)
    rows = tm // CHUNK
    k1, k2 = a1.shape[1], a2.shape[1]
    w1 = w_out[:k1].astype(BF16)
    w2 = w_out[k1:].astype(BF16)
    return pl.pallas_call(
        _proj_out_kernel,
        grid=(t // tm,),
        in_specs=[pl.BlockSpec((tm, k1), lambda i: (i, 0)), pl.BlockSpec((tm, k2), lambda i: (i, 0)),
                  _resident((k1, d), lambda i: (0, 0)), _resident((k2, d), lambda i: (0, 0)),
                  pl.BlockSpec((tm, d), lambda i: (i, 0)),
                  pl.BlockSpec((None, rows, d), lambda i: (layer, i, 2))],
        out_specs=pl.BlockSpec((tm, d), lambda i: (i, 0)),
        out_shape=jax.ShapeDtypeStruct((t, d), F32),
        compiler_params=_cparams(("arbitrary",)),
        name="proj_out",
    )(a1, a2, w1, w2, x, modc)


def _ffn_kernel(final_norm, x_ref, g_ref, sc_ref, sh_ref, gate_ref, wg_ref, wu_ref, wd_ref, gf_ref,
                o_ref, h_s, acc_s):
    f = pl.program_id(1)

    @pl.when(f == 0)
    def _():
        h_s[...] = _modulate(x_ref[...], g_ref[...], sc_ref[...], sh_ref[...]).astype(BF16)
        acc_s[...] = jnp.zeros(acc_s.shape, F32)

    h = h_s[...]
    a = _dot(h, wg_ref[...])
    u = _dot(h, wu_ref[...])
    acc_s[...] += _dot((a * jax.nn.sigmoid(a) * u).astype(BF16), wd_ref[...])

    @pl.when(f == pl.num_programs(1) - 1)
    def _():
        y = x_ref[...] + _gate_rows(acc_s[...], gate_ref[...])
        if final_norm:
            y = y * lax.rsqrt(jnp.mean(y * y, axis=-1, keepdims=True) + EPS) * gf_ref[...]
        o_ref[...] = y


def _ffn(x, modc, layer, g, wg, wu, wd, g_final, final_norm):
    t, d = x.shape
    f = wg.shape[1]
    tm = _token_tile(t)
    tf = _pick(f, (512, 256, 128))
    rows = tm // CHUNK
    modspec = lambda comp: pl.BlockSpec((None, rows, d), lambda i, j: (layer, i, comp))
    return pl.pallas_call(
        functools.partial(_ffn_kernel, final_norm),
        grid=(t // tm, f // tf),
        in_specs=[pl.BlockSpec((tm, d), lambda i, j: (i, 0)),
                  _resident((1, d), lambda i, j: (0, 0)),
                  modspec(4), modspec(3), modspec(5),
                  pl.BlockSpec((d, tf), lambda i, j: (0, j)),
                  pl.BlockSpec((d, tf), lambda i, j: (0, j)),
                  pl.BlockSpec((tf, d), lambda i, j: (j, 0)),
                  _resident((1, d), lambda i, j: (0, 0))],
        out_specs=pl.BlockSpec((tm, d), lambda i, j: (i, 0)),
        out_shape=jax.ShapeDtypeStruct((t, d), F32),
        scratch_shapes=[pltpu.VMEM((tm, d), BF16), pltpu.VMEM((tm, d), F32)],
        compiler_params=_cparams(("arbitrary", "arbitrary")),
        name="ffn",
    )(x, g, modc, modc, modc, wg.astype(BF16), wu.astype(BF16), wd.astype(BF16), g_final)


def _round_up(n, m):
    return (n + m - 1) // m * m


def kernel(x_prompt, x_sample, c_prompt, c_sample, cache_a_k, cache_a_v, state_b_c, state_b_n, state_b_m,
           cache_c_k, cache_c_v, cache_c_idx, cache_d_ckv, cache_d_kpe, w_ada, b_ada, g_norm_mix, g_norm_ffn,
           w_in_even, w_out_even, sinks_a, b_igate, b_fgate, g_mlstm, w_in_odd, w_out_odd, g_ckv, w_uk, w_uv,
           w_ffn_gate, w_ffn_up, w_ffn_down, g_final):
    nb, s, d = x_prompt.shape
    db, ds, _ = x_sample.shape
    assert ds == CHUNK and s % CHUNK == 0
    depth = w_ada.shape[0]
    past = cache_c_k.shape[2]
    tp, ts = nb * s, db * ds
    t = tp + ts
    tm = _token_tile(t)
    assert s % tm == 0 and tp % tm == 0
    seq_chunks = s // CHUNK
    npc = tp // CHUNK

    x = jnp.concatenate([x_prompt.reshape(tp, d), x_sample.reshape(ts, d)], axis=0)
    mod = _ada(jnp.concatenate([c_prompt, c_sample], axis=0), w_ada, b_ada)
    mod_p = jnp.broadcast_to(mod[:, :nb, None, :], (depth, nb, seq_chunks, 6 * d)).reshape(depth, npc, 6 * d)
    modc = jnp.concatenate([mod_p, mod[:, nb:]], axis=1)

    pos = jnp.concatenate([jnp.arange(s, dtype=I32), jnp.tile(past + jnp.arange(ds, dtype=I32), tm // ds)])
    tab_a = _rope_tables(pos, ROT_A, DA)
    tab_c = _rope_tables(pos, ROT_C, DC)
    tab_d = _rope_tables(pos, DROPE, DROPE)
    prompt_tiles, seq_tiles = tp // tm, s // tm
    tab_map = lambda i: jnp.where(i < prompt_tiles, i % seq_tiles, seq_tiles)

    w_ukv_all = jnp.concatenate([w_uk, w_uv], axis=2).astype(BF16)
    l_s = past + ds
    lp_s = _round_up(l_s, 512)

    def cat_past(cache, new, width):
        full = jnp.concatenate([cache.reshape(db, past, width), new.reshape(db, ds, width)], axis=1)
        full = jnp.pad(full, ((0, 0), (0, lp_s - l_s), (0, 0)))
        return full.astype(BF16).reshape(db * lp_s, width)

    even_p, even_s, odd_p, odd_s = [], [], [], []
    for l in range(depth):
        i = l // 2
        g_mix = g_norm_mix[l].reshape(1, d)
        if l % 2 == 0:
            gate_bias = _pad_cols(jnp.concatenate([b_igate[i], b_fgate[i]]).reshape(1, 2 * HB), LANES)
            qa, ka, va, qb, kb, vb, gates, og = _proj_in(
                _proj_even_kernel, "proj_even", x, modc, l, g_mix, _even_weight(w_in_even[i]), gate_bias,
                list(tab_a), tab_map,
                [(HA * DA, BF16), (KVA * DA, F32), (KVA * DA, F32), (HB * DKB, BF16), (HB * DKB, BF16),
                 (HB * DVB, BF16), (LANES, F32), (HB * DVB, BF16)])
            ka_s = ka[tp:].reshape(db, ds, KVA * DA)
            va_s = va[tp:].reshape(db, ds, KVA * DA)
            kband = jnp.concatenate([cache_a_k[i].reshape(db, WINDOW, KVA * DA), ka_s], axis=1)
            vband = jnp.concatenate([cache_a_v[i].reshape(db, WINDOW, KVA * DA), va_s], axis=1)
            k_all = jnp.concatenate([ka[:tp], kband.reshape(-1, KVA * DA)], axis=0)
            v_all = jnp.concatenate([va[:tp], vband.reshape(-1, KVA * DA)], axis=0)
            oa = _swa(qa, k_all, v_all, sinks_a[i], npc, seq_chunks)
            c0 = jnp.concatenate([jnp.zeros((nb, HB, DKB, DVB), F32), state_b_c[i]], axis=0)
            n0 = jnp.concatenate([jnp.zeros((nb, HB, DKB), F32), state_b_n[i]], axis=0)
            m0 = jnp.concatenate([jnp.zeros((nb, HB), F32), state_b_m[i]], axis=0).reshape(nb + db, 1, HB)
            hb, c_new, n_new, m_new = _mlstm(qb, kb, vb, gates, og, g_mlstm[i].reshape(1, HB * DVB),
                                             c0, n0, m0, npc, seq_chunks, nb)
            m_new = m_new.reshape(nb + db, HB)
            ka_p = ka[:tp].reshape(nb, s, KVA, DA)
            va_p = va[:tp].reshape(nb, s, KVA, DA)
            even_p.append((ka_p[:, -WINDOW:], va_p[:, -WINDOW:], c_new[:nb], n_new[:nb], m_new[:nb]))
            even_s.append((kband[:, -WINDOW:].reshape(db, WINDOW, KVA, DA),
                           vband[:, -WINDOW:].reshape(db, WINDOW, KVA, DA), c_new[nb:], n_new[nb:], m_new[nb:]))
            x = _proj_out(oa, hb, w_out_even[i], x, modc, l)
        else:
            qc, kc, vc, qi, ki, wi, qdn, qdp, ckv, kpe = _proj_in(
                _proj_odd_kernel, "proj_odd", x, modc, l, g_mix, _odd_weight(w_in_odd[i]),
                g_ckv[i].reshape(1, KV_LORA), list(tab_a) + list(tab_c) + list(tab_d), tab_map,
                [(HC * DC, BF16), (KVC * DC, F32), (KVC * DC, F32), (HI * DI, BF16), (DI, F32), (LANES, F32),
                 (HD * DNOPE, BF16), (HD * DROPE, BF16), (KV_LORA, F32), (DROPE, F32)])
            oc_p = _dsa(qc[:tp], qi[:tp], wi[:tp], kc[:tp].astype(BF16), vc[:tp].astype(BF16),
                        ki[:tp].astype(BF16), nb, s, s, s, True, 128, "dsa_prompt")
            oc_s = _dsa(qc[tp:], qi[tp:], wi[tp:],
                        cat_past(cache_c_k[i], kc[tp:], KVC * DC), cat_past(cache_c_v[i], vc[tp:], KVC * DC),
                        cat_past(cache_c_idx[i], ki[tp:], DI), db, ds, l_s, lp_s, False, ds, "dsa_sample")
            ckv_rows = jnp.concatenate([ckv[:tp].astype(BF16), cat_past(cache_d_ckv[i], ckv[tp:], KV_LORA)], axis=0)
            kv_up = _latent_up(ckv_rows, w_ukv_all[i])
            od_p = _mla(qdn[:tp], qdp[:tp], kv_up[:tp], kpe[:tp].astype(BF16), nb, s, s, s, True,
                        _pick(s, (256, 128, 64)), "mla_prompt")
            od_s = _mla(qdn[tp:], qdp[tp:], kv_up[tp:], cat_past(cache_d_kpe[i], kpe[tp:], DROPE),
                        db, ds, l_s, lp_s, False, ds, "mla_sample")
            oc = jnp.concatenate([oc_p, oc_s], axis=0)
            od = jnp.concatenate([od_p, od_s], axis=0)
            odd_p.append((kc[:tp].reshape(nb, s, KVC, DC), vc[:tp].reshape(nb, s, KVC, DC),
                          ki[:tp].reshape(nb, s, DI), ckv[:tp].reshape(nb, s, KV_LORA),
                          kpe[:tp].reshape(nb, s, DROPE)))
            odd_s.append((kc[tp:].reshape(db, ds, KVC, DC), vc[tp:].reshape(db, ds, KVC, DC),
                          ki[tp:].reshape(db, ds, DI), ckv[tp:].reshape(db, ds, KV_LORA),
                          kpe[tp:].reshape(db, ds, DROPE)))
            x = _proj_out(oc, od, w_out_odd[i], x, modc, l)
        x = _ffn(x, modc, l, g_norm_ffn[l].reshape(1, d), w_ffn_gate[l], w_ffn_up[l], w_ffn_down[l],
                 g_final.reshape(1, d), l == depth - 1)

    y_prompt = x[:tp].reshape(nb, s, d)
    y_sample = x[tp:].reshape(db, ds, d)
    st_p = [jnp.stack([e[j] for e in even_p]) for j in range(5)] + [jnp.stack([o[j] for o in odd_p]) for j in range(5)]
    st_s = [jnp.stack([e[j] for e in even_s]) for j in range(5)] + [jnp.stack([o[j] for o in odd_s]) for j in range(5)]
    return (y_prompt, y_sample, *st_p, *st_s)
```

```python
import functools
import math

import jax
import jax.numpy as jnp
import numpy as np
from jax import lax
from jax.experimental import pallas as pl
from jax.experimental.pallas import tpu as pltpu

F32 = jnp.float32
BF16 = jnp.bfloat16
I32 = jnp.int32

CHUNK = 64
ROPE_THETA = 500000.0
EPS = 1e-6
NEG = -1e30
LANES = 128

HA, KVA, DA = 16, 2, 64
ROT_A = DA // 4
WINDOW = 128
WIN_CHUNKS = WINDOW // CHUNK
HB, DKB, DVB = 4, 128, 256
HC, KVC, DC = 8, 2, 128
ROT_C = DC // 4
HI, DI = 16, 64
ROT_I = DI // 4
INDEX_TOPK = 256
HD, DNOPE, DROPE, DVD, KV_LORA = 8, 128, 64, 128, 512
MLA_SCALE = (DNOPE + DROPE) ** -0.5

VMEM_LIMIT = 56 * 1024 * 1024


def _cparams(sem, vmem=VMEM_LIMIT):
    return pltpu.CompilerParams(dimension_semantics=sem, vmem_limit_bytes=vmem)


def _pick(n, options):
    for o in options:
        if n % o == 0:
            return o
    raise ValueError(f"no tile in {options} divides {n}")


def _resident(shape, index_map):
    return pl.BlockSpec(shape, index_map, pipeline_mode=pl.Buffered(1))


def _dot(a, b):
    return jnp.dot(a, b, preferred_element_type=F32)


def _dot_nt(a, b):
    return lax.dot_general(a, b, (((1,), (1,)), ((), ())), preferred_element_type=F32)


def _dot_tn(a, b):
    return lax.dot_general(a, b, (((0,), (0,)), ((), ())), preferred_element_type=F32)


def _modulate(x, g, sc, sh):
    tm, d = x.shape
    y = x * lax.rsqrt(jnp.mean(x * x, axis=-1, keepdims=True) + EPS) * g
    y = y.reshape(tm // CHUNK, CHUNK, d) * (1.0 + sc[:, None, :]) + sh[:, None, :]
    return y.reshape(tm, d)


def _gate_rows(y, g):
    tm, d = y.shape
    return (y.reshape(tm // CHUNK, CHUNK, d) * g[:, None, :]).reshape(tm, d)


def _rope(x, cos, sp, sm, half):
    blocks = []
    for j in range(x.shape[1] // LANES):
        xb = x[:, j * LANES:(j + 1) * LANES]
        blocks.append(xb * cos + pltpu.roll(xb, half, 1) * sp + pltpu.roll(xb, LANES - half, 1) * sm)
    return blocks[0] if len(blocks) == 1 else jnp.concatenate(blocks, axis=1)


def _rope_tables(pos, rot, head):
    half = rot // 2
    inv = ROPE_THETA ** (-jnp.arange(half, dtype=F32) * 2.0 / rot)
    ang = pos.astype(F32)[:, None] * inv[None, :]
    cos, sin = jnp.cos(ang), jnp.sin(ang)
    p = pos.shape[0]
    one = jnp.ones((p, head - rot), F32)
    zero = jnp.zeros((p, head - rot), F32)
    zh = jnp.zeros((p, half), F32)
    c = jnp.concatenate([cos, cos, one], axis=1)
    s_plus = jnp.concatenate([zh, sin, zero], axis=1)
    s_minus = jnp.concatenate([-sin, zh, zero], axis=1)
    rep = LANES // head
    return tuple(jnp.tile(t, (1, rep)) for t in (c, s_plus, s_minus))


def _ada_kernel(c_ref, w_ref, b_ref, o_ref):
    c = c_ref[...]
    h = (c * jax.nn.sigmoid(c)).astype(BF16)
    o_ref[...] = _dot(h, w_ref[...].astype(BF16)) + b_ref[...]


def _ada(c, w_ada, b_ada):
    depth, d, n = w_ada.shape
    ns = c.shape[0]
    tn = _pick(n, (1024, 512, 256, 128))
    return pl.pallas_call(
        _ada_kernel,
        grid=(depth, n // tn),
        in_specs=[pl.BlockSpec((ns, d), lambda l, j: (0, 0)),
                  pl.BlockSpec((None, d, tn), lambda l, j: (l, 0, j)),
                  pl.BlockSpec((None, 1, tn), lambda l, j: (l, 0, j))],
        out_specs=pl.BlockSpec((None, ns, tn), lambda l, j: (l, 0, j)),
        out_shape=jax.ShapeDtypeStruct((depth, ns, n), F32),
        compiler_params=_cparams(("arbitrary", "arbitrary")),
        name="ada",
    )(c, w_ada, b_ada.reshape(depth, 1, n))


def _col_tiles(lo, hi, step=512):
    return [(a, min(a + step, hi)) for a in range(lo, hi, step)]


E_QA, E_KA, E_VA, E_QB, E_KB, E_VB, E_GT, E_OB, E_END = 0, 1024, 1152, 1280, 1792, 2304, 3328, 3456, 4480


def _pad_cols(w, width):
    return jnp.pad(w, ((0, 0), (0, width - w.shape[1])))


def _even_weight(w):
    qa, ka, va, qb, kb, vb, ib, fb, ob = jnp.split(
        w, np.cumsum((HA * DA, KVA * DA, KVA * DA, HB * DKB, HB * DKB, HB * DVB, HB, HB))[:].tolist(), axis=1)
    gates = _pad_cols(jnp.concatenate([ib, fb], axis=1), LANES)
    return jnp.concatenate([qa, ka, va, qb, kb, vb, gates, ob], axis=1).astype(BF16)


def _proj_even_kernel(x_ref, g_ref, sc_ref, sh_ref, w_ref, gb_ref, cos_ref, sp_ref, sm_ref,
                      qa_o, ka_o, va_o, qb_o, kb_o, vb_o, gt_o, og_o):
    h = _modulate(x_ref[...], g_ref[...], sc_ref[...], sh_ref[...]).astype(BF16)
    cos, sp, sm = cos_ref[...], sp_ref[...], sm_ref[...]
    half = ROT_A // 2

    def mm(lo, hi):
        return _dot(h, w_ref[:, lo:hi])

    for lo, hi in _col_tiles(E_QA, E_KA):
        qa_o[:, lo - E_QA:hi - E_QA] = _rope(mm(lo, hi), cos, sp, sm, half).astype(qa_o.dtype)
    ka_o[...] = _rope(mm(E_KA, E_VA), cos, sp, sm, half)
    va_o[...] = mm(E_VA, E_QB)
    for lo, hi in _col_tiles(E_QB, E_KB):
        qb_o[:, lo - E_QB:hi - E_QB] = (mm(lo, hi) * (DKB ** -0.5)).astype(qb_o.dtype)
    for lo, hi in _col_tiles(E_KB, E_VB):
        kb_o[:, lo - E_KB:hi - E_KB] = mm(lo, hi).astype(kb_o.dtype)
    for lo, hi in _col_tiles(E_VB, E_GT):
        vb_o[:, lo - E_VB:hi - E_VB] = mm(lo, hi).astype(vb_o.dtype)
    t = mm(E_GT, E_OB) + gb_ref[...]
    lane = lax.broadcasted_iota(I32, t.shape, 1)
    gt_o[...] = jnp.where(lane < HB, t, jax.nn.log_sigmoid(t))
    for lo, hi in _col_tiles(E_OB, E_END):
        og_o[:, lo - E_OB:hi - E_OB] = jax.nn.sigmoid(mm(lo, hi)).astype(og_o.dtype)


(O_QC, O_KC, O_VC, O_QI, O_KI, O_WI, O_QDN, O_QDP, O_CKV, O_KPE, O_END) = (
    0, 1024, 1280, 1536, 2560, 2688, 2816, 3840, 4352, 4864, 4992)


def _odd_weight(w):
    qc, kc, vc, qi, ki, wi, qd, ckv, kpe = jnp.split(
        w, np.cumsum((HC * DC, KVC * DC, KVC * DC, HI * DI, DI, HI, HD * (DNOPE + DROPE), KV_LORA)).tolist(), axis=1)
    d = w.shape[0]
    qd = qd.reshape(d, HD, DNOPE + DROPE)
    qdn = qd[:, :, :DNOPE].reshape(d, HD * DNOPE)
    qdp = qd[:, :, DNOPE:].reshape(d, HD * DROPE)
    return jnp.concatenate([qc, kc, vc, qi, _pad_cols(ki, LANES), _pad_cols(wi, LANES), qdn, qdp, ckv,
                            _pad_cols(kpe, LANES)], axis=1).astype(BF16)


def _proj_odd_kernel(x_ref, g_ref, sc_ref, sh_ref, w_ref, gckv_ref,
                     ci_ref, spi_ref, smi_ref, cc_ref, spc_ref, smc_ref, cd_ref, spd_ref, smd_ref,
                     qc_o, kc_o, vc_o, qi_o, ki_o, wi_o, qdn_o, qdp_o, ckv_o, kpe_o,
                     kcb_o, vcb_o, kib_o, ckvb_o, kpeb_o):
    h = _modulate(x_ref[...], g_ref[...], sc_ref[...], sh_ref[...]).astype(BF16)
    tab_i = (ci_ref[...], spi_ref[...], smi_ref[...], ROT_I // 2)
    tab_c = (cc_ref[...], spc_ref[...], smc_ref[...], ROT_C // 2)
    tab_d = (cd_ref[...], spd_ref[...], smd_ref[...], DROPE // 2)

    def mm(lo, hi):
        return _dot(h, w_ref[:, lo:hi])

    for lo, hi in _col_tiles(O_QC, O_KC):
        qc_o[:, lo - O_QC:hi - O_QC] = (_rope(mm(lo, hi), *tab_c) * (DC ** -0.5)).astype(qc_o.dtype)
    kc = _rope(mm(O_KC, O_VC), *tab_c)
    kc_o[...] = kc
    kcb_o[...] = kc.astype(BF16)
    vc = mm(O_VC, O_QI)
    vc_o[...] = vc
    vcb_o[...] = vc.astype(BF16)
    for lo, hi in _col_tiles(O_QI, O_KI):
        qi_o[:, lo - O_QI:hi - O_QI] = _rope(mm(lo, hi), *tab_i).astype(qi_o.dtype)
    ki = _rope(mm(O_KI, O_WI), *tab_i)[:, :DI]
    ki_o[...] = ki
    kib_o[...] = ki.astype(BF16)
    wi_o[...] = mm(O_WI, O_QDN) * (HI ** -0.5 * DI ** -0.5)
    for lo, hi in _col_tiles(O_QDN, O_QDP):
        qdn_o[:, lo - O_QDN:hi - O_QDN] = (mm(lo, hi) * MLA_SCALE).astype(qdn_o.dtype)
    qdp_o[...] = (_rope(mm(O_QDP, O_CKV), *tab_d) * MLA_SCALE).astype(qdp_o.dtype)
    c = mm(O_CKV, O_KPE)
    c = c * lax.rsqrt(jnp.mean(c * c, axis=-1, keepdims=True) + EPS) * gckv_ref[...]
    ckv_o[...] = c
    ckvb_o[...] = c.astype(BF16)
    kpe = _rope(mm(O_KPE, O_END), *tab_d)[:, :DROPE]
    kpe_o[...] = kpe
    kpeb_o[...] = kpe.astype(BF16)


def _token_tile(t):
    return _pick(t, (512,))


def _proj_in(kernel_fn, name, x, modc, layer, g, w, extra, tables, tab_map, outs):
    t, d = x.shape
    tm = _token_tile(t)
    rows = tm // CHUNK
    in_specs = [pl.BlockSpec((tm, d), lambda i: (i, 0)),
                _resident((1, d), lambda i: (0, 0)),
                pl.BlockSpec((None, rows, d), lambda i: (layer, i, 1)),
                pl.BlockSpec((None, rows, d), lambda i: (layer, i, 0)),
                _resident(w.shape, lambda i: (0, 0)),
                _resident(extra.shape, lambda i: (0, 0))]
    in_specs += [pl.BlockSpec((tm, LANES), lambda i: (tab_map(i), 0)) for _ in tables]
    return pl.pallas_call(
        kernel_fn,
        grid=(t // tm,),
        in_specs=in_specs,
        out_specs=[pl.BlockSpec((tm, wd), lambda i: (i, 0)) for wd, _ in outs],
        out_shape=[jax.ShapeDtypeStruct((t, wd), dt) for wd, dt in outs],
        compiler_params=_cparams(("arbitrary",)),
        name=name,
    )(x, g, modc, modc, w, extra, *tables)


def _swa_kernel(tq, tiles_per_seq, sink_ref, q_ref, kw_ref, kc_ref, vw_ref, vc_ref, *rest):
    o_ref = rest[-1]
    kb = jnp.concatenate([kw_ref[...], kc_ref[...]], axis=0).astype(BF16)
    vb = jnp.concatenate([vw_ref[...], vc_ref[...]], axis=0).astype(BF16)
    qchunk = lax.broadcasted_iota(I32, (tq, WINDOW + tq), 0) // CHUNK
    kchunk = lax.broadcasted_iota(I32, (tq, WINDOW + tq), 1) // CHUNK
    valid = jnp.logical_and(kchunk >= qchunk, kchunk <= qchunk + WIN_CHUNKS)
    if tiles_per_seq:
        has_window = pl.program_id(0) % tiles_per_seq != 0
        valid = jnp.logical_and(valid, jnp.logical_or(kchunk >= WIN_CHUNKS, has_window))
    bias = jnp.where(valid, 0.0, NEG)
    group = HA // KVA
    for hq in range(HA):
        kv = hq // group
        q = q_ref[:, hq * DA:(hq + 1) * DA]
        s = _dot_nt(q, kb[:, kv * DA:(kv + 1) * DA]) * (DA ** -0.5) + bias
        sink = sink_ref[hq]
        m = jnp.maximum(jnp.max(s, axis=-1, keepdims=True), sink)
        p = jnp.exp(s - m)
        den = jnp.sum(p, axis=-1, keepdims=True) + jnp.exp(sink - m)
        o = _dot(p.astype(BF16), vb[:, kv * DA:(kv + 1) * DA]) / den
        o_ref[:, hq * DA:(hq + 1) * DA] = o.astype(o_ref.dtype)


def _swa(qa, ka, va, win_k, win_v, sinks, tp, s):
    t = qa.shape[0]
    tq = _pick(s, (2 * WINDOW, WINDOW))
    wpt = tq // WINDOW
    qspec = lambda rows, off: pl.BlockSpec((rows, HA * DA), lambda g: (g + off, 0))
    kvspec = lambda rows, off: pl.BlockSpec((rows, KVA * DA), lambda g: (g + off, 0))
    wspec_p = pl.BlockSpec((WINDOW, KVA * DA), lambda g: (jnp.maximum(g * wpt - 1, 0), 0))
    smem = pl.BlockSpec(memory_space=pltpu.SMEM)
    out_shape = jax.ShapeDtypeStruct((t, HA * DA), BF16)
    oa = pl.pallas_call(
        functools.partial(_swa_kernel, tq, s // tq),
        grid=(tp // tq,),
        in_specs=[smem, qspec(tq, 0), wspec_p, kvspec(tq, 0), wspec_p, kvspec(tq, 0)],
        out_specs=qspec(tq, 0),
        out_shape=out_shape,
        compiler_params=_cparams(("arbitrary",)),
        name="swa_prompt",
    )(sinks, qa, ka, ka, va, va)
    off = tp // CHUNK
    return pl.pallas_call(
        functools.partial(_swa_kernel, CHUNK, 0),
        grid=((t - tp) // CHUNK,),
        in_specs=[smem, qspec(CHUNK, off), kvspec(WINDOW, 0), kvspec(CHUNK, off), kvspec(WINDOW, 0),
                  kvspec(CHUNK, off), pl.BlockSpec(memory_space=pl.ANY)],
        out_specs=qspec(CHUNK, off),
        out_shape=out_shape,
        input_output_aliases={6: 0},
        compiler_params=_cparams(("arbitrary",)),
        name="swa_sample",
    )(sinks, qa, win_k, ka, win_v, va, oa)


def _mlstm_kernel(n_prompt_chunks, seq_chunks, q_ref, k_ref, v_ref, gt_ref, og_ref, gmh_ref,
                  c0_ref, n0_ref, m0_ref, h_o, c_o, n_o, m_o, c_s, n_s, m_s):
    g = pl.program_id(0)
    is_prompt = g < n_prompt_chunks
    c_idx = g % seq_chunks
    first = jnp.logical_or(jnp.logical_not(is_prompt), c_idx == 0)
    last = jnp.logical_or(jnp.logical_not(is_prompt), c_idx == seq_chunks - 1)

    @pl.when(first)
    def _():
        c_s[...] = c0_ref[...]
        n_s[...] = n0_ref[...]
        m_s[...] = m0_ref[...]

    gt = gt_ref[...]
    gt_t = gt.T
    row = lax.broadcasted_iota(I32, (CHUNK, CHUNK), 0)
    col = lax.broadcasted_iota(I32, (CHUNK, CHUNK), 1)
    causal = col <= row
    for hh in range(HB):
        ig_row = gt_t[hh:hh + 1, :]
        lf_row = gt_t[HB + hh:HB + hh + 1, :]
        ig_col = gt[:, hh:hh + 1]
        lf_col = gt[:, HB + hh:HB + hh + 1]
        b_col = jnp.sum(jnp.where(causal, lf_row, 0.0), axis=1, keepdims=True)
        b_row = jnp.sum(jnp.where(row <= col, lf_col, 0.0), axis=0, keepdims=True)
        m_prev = m_s[:, hh:hh + 1]
        d = jnp.where(causal, b_col - b_row + ig_row, NEG)
        inter = b_col + m_prev
        m_t = jnp.maximum(inter, jnp.max(d, axis=1, keepdims=True))
        w_intra = jnp.exp(d - m_t)
        w_inter = jnp.exp(inter - m_t)
        q = q_ref[:, hh * DKB:(hh + 1) * DKB]
        k = k_ref[:, hh * DKB:(hh + 1) * DKB]
        v = v_ref[:, hh * DVB:(hh + 1) * DVB]
        c_prev = c_s[hh]
        n_prev = n_s[hh:hh + 1, :]
        a = w_intra * _dot_nt(q, k)
        num = _dot(a.astype(BF16), v) + w_inter * _dot(q, c_prev.astype(BF16))
        den = (jnp.sum(a, axis=1, keepdims=True)
               + w_inter * jnp.sum(q.astype(F32) * n_prev, axis=1, keepdims=True))
        hv = num / jnp.maximum(jnp.abs(den), jnp.exp(-m_t))
        b_end = b_col[CHUNK - 1:CHUNK, :]
        g_row = b_end - b_row + ig_row
        g_col = b_end - b_col + ig_col
        m_new = jnp.maximum(b_end + m_prev, jnp.max(g_row, axis=1, keepdims=True))
        w_s = jnp.exp(g_col - m_new)
        w_c = jnp.exp(b_end + m_prev - m_new)
        c_s[hh] = w_c * c_prev + _dot_tn(k, (w_s * v.astype(F32)).astype(BF16))
        n_s[hh:hh + 1, :] = w_c * n_prev + jnp.sum(w_s * k.astype(F32), axis=0, keepdims=True)
        m_s[:, hh:hh + 1] = m_new
        y = hv * lax.rsqrt(jnp.mean(hv * hv, axis=-1, keepdims=True) + EPS)
        y = y * gmh_ref[:, hh * DVB:(hh + 1) * DVB] * og_ref[:, hh * DVB:(hh + 1) * DVB].astype(F32)
        h_o[:, hh * DVB:(hh + 1) * DVB] = y.astype(h_o.dtype)

    @pl.when(last)
    def _():
        c_o[...] = c_s[...]
        n_o[...] = n_s[...]
        m_o[...] = m_s[...]


def _mlstm(qb, kb, vb, gates, og, g_mh, c0, n0, m0, n_prompt_chunks, seq_chunks, n_prompt_seqs):
    t = qb.shape[0]
    nch = t // CHUNK
    ns = c0.shape[0]

    def seq(g):
        return jnp.where(g < n_prompt_chunks, g // seq_chunks, n_prompt_seqs + g - n_prompt_chunks)

    tok = lambda w: pl.BlockSpec((CHUNK, w), lambda g: (g, 0))
    st_specs = [pl.BlockSpec((None, HB, DKB, DVB), lambda g: (seq(g), 0, 0, 0)),
                pl.BlockSpec((None, HB, DKB), lambda g: (seq(g), 0, 0)),
                pl.BlockSpec((None, 1, HB), lambda g: (seq(g), 0, 0))]
    return pl.pallas_call(
        functools.partial(_mlstm_kernel, n_prompt_chunks, seq_chunks),
        grid=(nch,),
        in_specs=[tok(HB * DKB), tok(HB * DKB), tok(HB * DVB), tok(LANES), tok(HB * DVB),
                  _resident((1, HB * DVB), lambda g: (0, 0))] + st_specs,
        out_specs=[tok(HB * DVB)] + st_specs,
        out_shape=[jax.ShapeDtypeStruct((t, HB * DVB), BF16),
                   jax.ShapeDtypeStruct((ns, HB, DKB, DVB), F32),
                   jax.ShapeDtypeStruct((ns, HB, DKB), F32),
                   jax.ShapeDtypeStruct((ns, 1, HB), F32)],
        scratch_shapes=[pltpu.VMEM((HB, DKB, DVB), F32), pltpu.VMEM((HB, DKB), F32),
                        pltpu.VMEM((1, HB), F32)],
        compiler_params=_cparams(("arbitrary",)),
        name="mlstm",
    )(qb, kb, vb, gates, og, g_mh, c0, n0, m0)


INT_MIN = -2 ** 31


def _order_key(x):
    b = pltpu.bitcast(x, I32)
    return jnp.where(b >= 0, b, b ^ jnp.int32(0x7FFFFFFF))


def _dsa_kernel(tq, kblk, l_valid, l_pad, topk, causal,
                qc_ref, qi_ref, wi_ref, k_ref, v_ref, ki_ref, *rest):
    o_ref, key_s = rest[-2:]
    i = pl.program_id(1)
    q0 = i * tq
    rowq = lax.broadcasted_iota(I32, (tq, 1), 0)
    if causal:
        lim = ((q0 + rowq) // CHUNK + 1) * CHUNK
        nkb = (q0 + tq + kblk - 1) // kblk
    else:
        lim = jnp.full((tq, 1), l_valid, I32)
        nkb = l_pad // kblk
    lane = lax.broadcasted_iota(I32, (tq, kblk), 1)

    wi = wi_ref[...]

    def score_block(kb, carry):
        start = pl.multiple_of(kb * kblk, kblk)
        kib = ki_ref[pl.ds(start, kblk), :]
        acc = jnp.zeros((tq, kblk), F32)
        for hh in range(HI):
            s = _dot_nt(qi_ref[:, hh * DI:(hh + 1) * DI], kib)
            acc = acc + wi[:, hh:hh + 1] * jnp.maximum(s, 0.0)
        acc = acc + 0.0
        acc = jnp.where(start + lane < lim, acc, NEG)
        key_s[kb] = _order_key(acc)
        return carry

    lax.fori_loop(0, nkb, score_block, 0)

    lane1 = lax.broadcasted_iota(I32, (tq, LANES), 1)

    def count(pred_fn):
        def blk(kb, part):
            start = kb * kblk
            for c in range(kblk // LANES):
                keys = key_s[kb, :, c * LANES:(c + 1) * LANES]
                part = part + pred_fn(keys, start + c * LANES + lane1).astype(I32)
            return part
        part = lax.fori_loop(0, nkb, blk, jnp.zeros((tq, LANES), I32))
        return jnp.sum(part, axis=1, keepdims=True)

    def thr_bit(b, carry):
        thr, n_ge = carry
        cand = thr + lax.shift_left(jnp.int32(1), 31 - b)
        cand_b = jnp.broadcast_to(cand, (tq, LANES))
        cnt = count(lambda keys, idx: keys >= cand_b)
        keep = cnt >= topk
        return jnp.where(keep, cand, thr), jnp.where(keep, cnt, n_ge)

    thr, n_ge = lax.fori_loop(0, 32, thr_bit, (jnp.full((tq, 1), INT_MIN, I32),
                                               jnp.full((tq, 1), l_pad, I32)))

    nbits = max(1, (l_pad - 1).bit_length())

    def tie_cut():
        thr_b = jnp.broadcast_to(thr, (tq, LANES))
        need = topk - count(lambda keys, idx: keys > thr_b)

        def cut_bit(b, cut):
            cand = cut + lax.shift_left(jnp.int32(1), nbits - 1 - b)
            cand_b = jnp.broadcast_to(cand, (tq, LANES))
            cnt = count(lambda keys, idx: jnp.logical_and(keys == thr_b, idx < cand_b))
            return jnp.where(cnt < need, cand, cut)

        return lax.fori_loop(0, nbits, cut_bit, jnp.zeros((tq, 1), I32))

    cut = lax.cond(jnp.max(n_ge) > topk, tie_cut, lambda: jnp.full((tq, 1), l_pad, I32))

    group = HC // KVC
    qs = [jnp.concatenate([qc_ref[:, (kv * group + j) * DC:(kv * group + j + 1) * DC]
                           for j in range(group)], axis=0) for kv in range(KVC)]

    def attend_block(kb, carry):
        start = pl.multiple_of(kb * kblk, kblk)
        keys = key_s[kb]
        idx = start + lane
        sel = jnp.logical_or(keys > thr, jnp.logical_and(keys == thr, idx <= cut))
        sel = jnp.logical_and(sel, idx < lim)
        bias = jnp.where(sel, 0.0, NEG)
        new = []
        for kv in range(KVC):
            m, l, acc = carry[kv]
            kk = k_ref[pl.ds(start, kblk), kv * DC:(kv + 1) * DC]
            vv = v_ref[pl.ds(start, kblk), kv * DC:(kv + 1) * DC]
            s = _dot_nt(qs[kv], kk)
            s = jnp.concatenate([s[j * tq:(j + 1) * tq] + bias for j in range(group)], axis=0)
            m_new = jnp.maximum(m, jnp.max(s, axis=1, keepdims=True))
            alpha = jnp.exp(m - m_new)
            p = jnp.exp(s - m_new)
            l = alpha * l + jnp.sum(p, axis=1, keepdims=True)
            acc = alpha * acc + _dot(p.astype(BF16), vv)
            new.append((m_new, l, acc))
        return tuple(new)

    init = tuple((jnp.full((group * tq, 1), NEG, F32), jnp.zeros((group * tq, 1), F32),
                  jnp.zeros((group * tq, DC), F32)) for _ in range(KVC))
    res = lax.fori_loop(0, nkb, attend_block, init)
    for kv in range(KVC):
        _, l, acc = res[kv]
        o = acc / l
        for j in range(group):
            hq = kv * group + j
            o_ref[:, hq * DC:(hq + 1) * DC] = o[j * tq:(j + 1) * tq, :].astype(o_ref.dtype)


def _dsa(qc, qi, wi, k, v, ki, prev_out, nseq, sq, l_valid, l_pad, causal, tq, q_off, name):
    topk = min(INDEX_TOPK, l_valid // 4)
    kblk = _pick(l_pad, (512, 256, 128))
    assert kblk >= topk and sq % tq == 0 and q_off % tq == 0
    nq = sq // tq
    qmap = lambda b, i: (q_off // tq + b * nq + i, 0)
    kmap = lambda b, i: (b, 0)
    in_specs = [pl.BlockSpec((tq, HC * DC), qmap), pl.BlockSpec((tq, HI * DI), qmap),
                pl.BlockSpec((tq, LANES), qmap),
                _resident((l_pad, KVC * DC), kmap), _resident((l_pad, KVC * DC), kmap),
                _resident((l_pad, DI), kmap)]
    args = [qc, qi, wi, k, v, ki]
    aliases = {}
    if prev_out is not None:
        in_specs.append(pl.BlockSpec(memory_space=pl.ANY))
        aliases = {len(args): 0}
        args.append(prev_out)
    return pl.pallas_call(
        functools.partial(_dsa_kernel, tq, kblk, l_valid, l_pad, topk, causal),
        grid=(nseq, nq),
        in_specs=in_specs,
        out_specs=pl.BlockSpec((tq, HC * DC), qmap),
        out_shape=jax.ShapeDtypeStruct((qc.shape[0], HC * DC), BF16),
        scratch_shapes=[pltpu.VMEM((l_pad // kblk, tq, kblk), I32)],
        input_output_aliases=aliases,
        compiler_params=_cparams(("arbitrary", "arbitrary")),
        name=name,
    )(*args)


def _matmul_kernel(a_ref, w_ref, o_ref):
    o_ref[...] = _dot(a_ref[...], w_ref[...]).astype(o_ref.dtype)


def _latent_up(ckv, w_ukv):
    r = ckv.shape[0]
    tm = _pick(r, (1024, 512, 256, 128, 64))
    n = w_ukv.shape[1]
    return pl.pallas_call(
        _matmul_kernel,
        grid=(r // tm,),
        in_specs=[pl.BlockSpec((tm, KV_LORA), lambda i: (i, 0)), _resident(w_ukv.shape, lambda i: (0, 0))],
        out_specs=pl.BlockSpec((tm, n), lambda i: (i, 0)),
        out_shape=jax.ShapeDtypeStruct((r, n), BF16),
        compiler_params=_cparams(("arbitrary",)),
        name="latent_up",
    )(ckv, w_ukv)


def _mla_last_block(i, tq, kblk):
    return (i * tq + tq - 1) // kblk


def _mla_kernel(tq, kblk, l_valid, causal, it_ref, jt_ref, qn_ref, qp_ref, kv_ref, kpe_ref, *rest):
    o_ref, m_s, l_s, acc_s = rest[-4:]
    step_id = pl.program_id(1)
    i = it_ref[step_id]
    j = jt_ref[step_id]
    last = _mla_last_block(i, tq, kblk) if causal else pl.cdiv(l_valid, kblk) - 1

    @pl.when(j == 0)
    def _():
        m_s[...] = jnp.full(m_s.shape, NEG, F32)
        l_s[...] = jnp.zeros(l_s.shape, F32)
        acc_s[...] = jnp.zeros(acc_s.shape, F32)

    def step(masked):
        if masked:
            idx = j * kblk + lax.broadcasted_iota(I32, (tq, kblk), 1)
            if causal:
                rowq = i * tq + lax.broadcasted_iota(I32, (tq, kblk), 0)
                valid = idx < (rowq // CHUNK + 1) * CHUNK
            else:
                valid = idx < l_valid
            bias = jnp.where(valid, 0.0, NEG)
        kpe = kpe_ref[...]
        for hh in range(HD):
            kn = kv_ref[:, hh * DNOPE:(hh + 1) * DNOPE]
            vv = kv_ref[:, HD * DNOPE + hh * DVD:HD * DNOPE + (hh + 1) * DVD]
            s = (_dot_nt(qn_ref[:, hh * DNOPE:(hh + 1) * DNOPE], kn)
                 + _dot_nt(qp_ref[:, hh * DROPE:(hh + 1) * DROPE], kpe))
            if masked:
                s = s + bias
            m = m_s[hh]
            m_new = jnp.maximum(m, jnp.max(s, axis=1, keepdims=True))
            alpha = jnp.exp(m - m_new)
            p = jnp.exp(s - m_new)
            l_s[hh] = alpha * l_s[hh] + jnp.sum(p, axis=1, keepdims=True)
            acc_s[hh] = alpha * acc_s[hh] + _dot(p.astype(BF16), vv)
            m_s[hh] = m_new

    if causal:
        n_full = (i * tq + CHUNK) // kblk
    else:
        n_full = l_valid // kblk
    pl.when(j < n_full)(functools.partial(step, False))
    pl.when(j >= n_full)(functools.partial(step, True))

    @pl.when(j == last)
    def _():
        for hh in range(HD):
            o_ref[:, hh * DVD:(hh + 1) * DVD] = (acc_s[hh] / l_s[hh]).astype(o_ref.dtype)


def _mla(qn, qp, kv_up, kpe, prev_out, nseq, sq, l_valid, l_pad, causal, tq, q_off, kv_off, name):
    kblk = _pick(l_pad, (512, 256, 128))
    assert q_off % tq == 0 and kv_off % kblk == 0
    nq, nk = sq // tq, l_pad // kblk
    n_blocks = (lambda i: (i * tq + tq - 1) // kblk + 1) if causal else (lambda i: -(-l_valid // kblk))
    pairs = [(i, j) for i in range(nq) for j in range(n_blocks(i))]
    it = jnp.asarray([p[0] for p in pairs], I32)
    jt = jnp.asarray([p[1] for p in pairs], I32)
    qmap = lambda b, t, it, jt: (q_off // tq + b * nq + it[t], 0)
    kvmap = lambda b, t, it, jt: (kv_off // kblk + b * nk + jt[t], 0)
    kpemap = lambda b, t, it, jt: (b * nk + jt[t], 0)
    in_specs = [pl.BlockSpec((tq, HD * DNOPE), qmap), pl.BlockSpec((tq, HD * DROPE), qmap),
                pl.BlockSpec((kblk, HD * (DNOPE + DVD)), kvmap), pl.BlockSpec((kblk, DROPE), kpemap)]
    args = [it, jt, qn, qp, kv_up, kpe]
    aliases = {}
    if prev_out is not None:
        in_specs.append(pl.BlockSpec(memory_space=pl.ANY))
        aliases = {len(args): 0}
        args.append(prev_out)
    return pl.pallas_call(
        functools.partial(_mla_kernel, tq, kblk, l_valid, causal),
        grid_spec=pltpu.PrefetchScalarGridSpec(
            num_scalar_prefetch=2,
            grid=(nseq, len(pairs)),
            in_specs=in_specs,
            out_specs=pl.BlockSpec((tq, HD * DVD), qmap),
            scratch_shapes=[pltpu.VMEM((HD, tq, 1), F32), pltpu.VMEM((HD, tq, 1), F32),
                            pltpu.VMEM((HD, tq, DVD), F32)]),
        out_shape=jax.ShapeDtypeStruct((qn.shape[0], HD * DVD), BF16),
        input_output_aliases=aliases,
        compiler_params=_cparams(("arbitrary", "arbitrary")),
        name=name,
    )(*args)


def _proj_out_kernel(a1_ref, a2_ref, w1_ref, w2_ref, x_ref, gate_ref, o_ref):
    y = _dot(a1_ref[...], w1_ref[...]) + _dot(a2_ref[...], w2_ref[...])
    o_ref[...] = x_ref[...] + _gate_rows(y, gate_ref[...])


def _proj_out(a1, a2, w_out, x, modc, layer):
    t, d = x.shape
    tm = _token_tile(t)
    rows = tm // CHUNK
    k1, k2 = a1.shape[1], a2.shape[1]
    w1 = w_out[:k1].astype(BF16)
    w2 = w_out[k1:].astype(BF16)
    return pl.pallas_call(
        _proj_out_kernel,
        grid=(t // tm,),
        in_specs=[pl.BlockSpec((tm, k1), lambda i: (i, 0)), pl.BlockSpec((tm, k2), lambda i: (i, 0)),
                  _resident((k1, d), lambda i: (0, 0)), _resident((k2, d), lambda i: (0, 0)),
                  pl.BlockSpec((tm, d), lambda i: (i, 0)),
                  pl.BlockSpec((None, rows, d), lambda i: (layer, i, 2))],
        out_specs=pl.BlockSpec((tm, d), lambda i: (i, 0)),
        out_shape=jax.ShapeDtypeStruct((t, d), F32),
        compiler_params=_cparams(("arbitrary",)),
        name="proj_out",
    )(a1, a2, w1, w2, x, modc)


def _ffn_kernel(final_norm, x_ref, g_ref, sc_ref, sh_ref, gate_ref, wg_ref, wu_ref, wd_ref, gf_ref,
                o_ref, h_s, acc_s):
    f = pl.program_id(1)

    @pl.when(f == 0)
    def _():
        h_s[...] = _modulate(x_ref[...], g_ref[...], sc_ref[...], sh_ref[...]).astype(BF16)
        acc_s[...] = jnp.zeros(acc_s.shape, F32)

    h = h_s[...]
    a = _dot(h, wg_ref[...])
    u = _dot(h, wu_ref[...])
    acc_s[...] += _dot((a * jax.nn.sigmoid(a) * u).astype(BF16), wd_ref[...])

    @pl.when(f == pl.num_programs(1) - 1)
    def _():
        y = x_ref[...] + _gate_rows(acc_s[...], gate_ref[...])
        if final_norm:
            y = y * lax.rsqrt(jnp.mean(y * y, axis=-1, keepdims=True) + EPS) * gf_ref[...]
        o_ref[...] = y


def _ffn(x, modc, layer, g, wg, wu, wd, g_final, final_norm):
    t, d = x.shape
    f = wg.shape[1]
    tm = _token_tile(t)
    tf = _pick(f, (512, 256, 128))
    rows = tm // CHUNK
    modspec = lambda comp: pl.BlockSpec((None, rows, d), lambda i, j: (layer, i, comp))
    return pl.pallas_call(
        functools.partial(_ffn_kernel, final_norm),
        grid=(t // tm, f // tf),
        in_specs=[pl.BlockSpec((tm, d), lambda i, j: (i, 0)),
                  _resident((1, d), lambda i, j: (0, 0)),
                  modspec(4), modspec(3), modspec(5),
                  pl.BlockSpec((d, tf), lambda i, j: (0, j)),
                  pl.BlockSpec((d, tf), lambda i, j: (0, j)),
                  pl.BlockSpec((tf, d), lambda i, j: (j, 0)),
                  _resident((1, d), lambda i, j: (0, 0))],
        out_specs=pl.BlockSpec((tm, d), lambda i, j: (i, 0)),
        out_shape=jax.ShapeDtypeStruct((t, d), F32),
        scratch_shapes=[pltpu.VMEM((tm, d), BF16), pltpu.VMEM((tm, d), F32)],
        compiler_params=_cparams(("arbitrary", "arbitrary")),
        name="ffn",
    )(x, g, modc, modc, modc, wg.astype(BF16), wu.astype(BF16), wd.astype(BF16), g_final)


def _round_up(n, m):
    return (n + m - 1) // m * m


def kernel(x_prompt, x_sample, c_prompt, c_sample, cache_a_k, cache_a_v, state_b_c, state_b_n, state_b_m,
           cache_c_k, cache_c_v, cache_c_idx, cache_d_ckv, cache_d_kpe, w_ada, b_ada, g_norm_mix, g_norm_ffn,
           w_in_even, w_out_even, sinks_a, b_igate, b_fgate, g_mlstm, w_in_odd, w_out_odd, g_ckv, w_uk, w_uv,
           w_ffn_gate, w_ffn_up, w_ffn_down, g_final):
    nb, s, d = x_prompt.shape
    db, ds, _ = x_sample.shape
    assert ds == CHUNK and s % CHUNK == 0
    depth = w_ada.shape[0]
    past = cache_c_k.shape[2]
    tp, ts = nb * s, db * ds
    t = tp + ts
    tm = _token_tile(t)
    assert s % tm == 0 and tp % tm == 0
    seq_chunks = s // CHUNK
    npc = tp // CHUNK

    x = jnp.concatenate([x_prompt.reshape(tp, d), x_sample.reshape(ts, d)], axis=0)
    mod = _ada(jnp.concatenate([c_prompt, c_sample], axis=0), w_ada, b_ada)
    mod_p = jnp.broadcast_to(mod[:, :nb, None, :], (depth, nb, seq_chunks, 6 * d)).reshape(depth, npc, 6 * d)
    modc = jnp.concatenate([mod_p, mod[:, nb:]], axis=1)

    pos = jnp.concatenate([jnp.arange(s, dtype=I32), jnp.tile(past + jnp.arange(ds, dtype=I32), tm // ds)])
    tab_a = _rope_tables(pos, ROT_A, DA)
    tab_c = _rope_tables(pos, ROT_C, DC)
    tab_d = _rope_tables(pos, DROPE, DROPE)
    prompt_tiles, seq_tiles = tp // tm, s // tm
    tab_map = lambda i: jnp.where(i < prompt_tiles, i % seq_tiles, seq_tiles)

    w_ukv_all = jnp.concatenate([w_uk, w_uv], axis=2).astype(BF16)
    l_s = past + ds
    lp_s = _round_up(l_s, 512)

    def cat_past(cache, new, width):
        full = jnp.concatenate([cache.reshape(db, past, width), new.reshape(db, ds, width)], axis=1)
        full = jnp.pad(full, ((0, 0), (0, lp_s - l_s), (0, 0)))
        return full.astype(BF16).reshape(db * lp_s, width)

    even_p, even_s, odd_p, odd_s = [], [], [], []
    for l in range(depth):
        i = l // 2
        g_mix = g_norm_mix[l].reshape(1, d)
        if l % 2 == 0:
            gate_bias = _pad_cols(jnp.concatenate([b_igate[i], b_fgate[i]]).reshape(1, 2 * HB), LANES)
            qa, ka, va, qb, kb, vb, gates, og = _proj_in(
                _proj_even_kernel, "proj_even", x, modc, l, g_mix, _even_weight(w_in_even[i]), gate_bias,
                list(tab_a), tab_map,
                [(HA * DA, BF16), (KVA * DA, F32), (KVA * DA, F32), (HB * DKB, BF16), (HB * DKB, BF16),
                 (HB * DVB, BF16), (LANES, F32), (HB * DVB, BF16)])
            ka_s = ka[tp:].reshape(db, ds, KVA * DA)
            va_s = va[tp:].reshape(db, ds, KVA * DA)
            win_k = cache_a_k[i].reshape(db, WINDOW, KVA * DA)
            win_v = cache_a_v[i].reshape(db, WINDOW, KVA * DA)
            kband = jnp.concatenate([win_k, ka_s], axis=1)
            vband = jnp.concatenate([win_v, va_s], axis=1)
            oa = _swa(qa, ka, va, win_k.reshape(db * WINDOW, KVA * DA), win_v.reshape(db * WINDOW, KVA * DA),
                      sinks_a[i], tp, s)
            c0 = jnp.concatenate([jnp.zeros((nb, HB, DKB, DVB), F32), state_b_c[i]], axis=0)
            n0 = jnp.concatenate([jnp.zeros((nb, HB, DKB), F32), state_b_n[i]], axis=0)
            m0 = jnp.concatenate([jnp.zeros((nb, HB), F32), state_b_m[i]], axis=0).reshape(nb + db, 1, HB)
            hb, c_new, n_new, m_new = _mlstm(qb, kb, vb, gates, og, g_mlstm[i].reshape(1, HB * DVB),
                                             c0, n0, m0, npc, seq_chunks, nb)
            m_new = m_new.reshape(nb + db, HB)
            ka_p = ka[:tp].reshape(nb, s, KVA, DA)
            va_p = va[:tp].reshape(nb, s, KVA, DA)
            even_p.append((ka_p[:, -WINDOW:], va_p[:, -WINDOW:], c_new[:nb], n_new[:nb], m_new[:nb]))
            even_s.append((kband[:, -WINDOW:].reshape(db, WINDOW, KVA, DA),
                           vband[:, -WINDOW:].reshape(db, WINDOW, KVA, DA), c_new[nb:], n_new[nb:], m_new[nb:]))
            x = _proj_out(oa, hb, w_out_even[i], x, modc, l)
        else:
            qc, kc, vc, qi, ki, wi, qdn, qdp, ckv, kpe, kcb, vcb, kib, ckvb, kpeb = _proj_in(
                _proj_odd_kernel, "proj_odd", x, modc, l, g_mix, _odd_weight(w_in_odd[i]),
                g_ckv[i].reshape(1, KV_LORA), list(tab_a) + list(tab_c) + list(tab_d), tab_map,
                [(HC * DC, BF16), (KVC * DC, F32), (KVC * DC, F32), (HI * DI, BF16), (DI, F32), (LANES, F32),
                 (HD * DNOPE, BF16), (HD * DROPE, BF16), (KV_LORA, F32), (DROPE, F32),
                 (KVC * DC, BF16), (KVC * DC, BF16), (DI, BF16), (KV_LORA, BF16), (DROPE, BF16)])
            oc = _dsa(qc, qi, wi, kcb, vcb, kib, None, nb, s, s, s, True, 128, 0, "dsa_prompt")
            oc = _dsa(qc, qi, wi, cat_past(cache_c_k[i], kcb[tp:], KVC * DC),
                      cat_past(cache_c_v[i], vcb[tp:], KVC * DC), cat_past(cache_c_idx[i], kib[tp:], DI),
                      oc, db, ds, l_s, lp_s, False, ds, tp, "dsa_sample")
            ckv_rows = jnp.concatenate([ckvb[:tp], cat_past(cache_d_ckv[i], ckvb[tp:], KV_LORA)], axis=0)
            kv_up = _latent_up(ckv_rows, w_ukv_all[i])
            od = _mla(qdn, qdp, kv_up, kpeb, None, nb, s, s, s, True, _pick(s, (256, 128, 64)), 0, 0,
                      "mla_prompt")
            od = _mla(qdn, qdp, kv_up, cat_past(cache_d_kpe[i], kpeb[tp:], DROPE), od,
                      db, ds, l_s, lp_s, False, ds, tp, tp, "mla_sample")
            odd_p.append((kc[:tp].reshape(nb, s, KVC, DC), vc[:tp].reshape(nb, s, KVC, DC),
                          ki[:tp].reshape(nb, s, DI), ckv[:tp].reshape(nb, s, KV_LORA),
                          kpe[:tp].reshape(nb, s, DROPE)))
            odd_s.append((kc[tp:].reshape(db, ds, KVC, DC), vc[tp:].reshape(db, ds, KVC, DC),
                          ki[tp:].reshape(db, ds, DI), ckv[tp:].reshape(db, ds, KV_LORA),
                          kpe[tp:].reshape(db, ds, DROPE)))
            x = _proj_out(oc, od, w_out_odd[i], x, modc, l)
        x = _ffn(x, modc, l, g_norm_ffn[l].reshape(1, d), w_ffn_gate[l], w_ffn_up[l], w_ffn_down[l],
                 g_final.reshape(1, d), l == depth - 1)

    y_prompt = x[:tp].reshape(nb, s, d)
    y_sample = x[tp:].reshape(db, ds, d)
    st_p = [jnp.stack([e[j] for e in even_p]) for j in range(5)] + [jnp.stack([o[j] for o in odd_p]) for j in range(5)]
    st_s = [jnp.stack([e[j] for e in even_s]) for j in range(5)] + [jnp.stack([o[j] for o in odd_s]) for j in range(5)]
    return (y_prompt, y_sample, *st_p, *st_s)
```

```python
import functools
import math

import jax
import jax.numpy as jnp
import numpy as np
from jax import lax
from jax.experimental import pallas as pl
from jax.experimental.pallas import tpu as pltpu

F32 = jnp.float32
BF16 = jnp.bfloat16
I32 = jnp.int32

CHUNK = 64
ROPE_THETA = 500000.0
EPS = 1e-6
NEG = -1e30
LANES = 128

HA, KVA, DA = 16, 2, 64
ROT_A = DA // 4
WINDOW = 128
WIN_CHUNKS = WINDOW // CHUNK
HB, DKB, DVB = 4, 128, 256
HC, KVC, DC = 8, 2, 128
ROT_C = DC // 4
HI, DI = 16, 64
ROT_I = DI // 4
INDEX_TOPK = 256
HD, DNOPE, DROPE, DVD, KV_LORA = 8, 128, 64, 128, 512
MLA_SCALE = (DNOPE + DROPE) ** -0.5
QD_SLOT = 2 * LANES

VMEM_LIMIT = 56 * 1024 * 1024


def _cparams(sem, vmem=VMEM_LIMIT):
    return pltpu.CompilerParams(dimension_semantics=sem, vmem_limit_bytes=vmem)


def _pick(n, options):
    for o in options:
        if n % o == 0:
            return o
    raise ValueError(f"no tile in {options} divides {n}")


def _resident(shape, index_map):
    return pl.BlockSpec(shape, index_map, pipeline_mode=pl.Buffered(1))


def _dot(a, b):
    return jnp.dot(a, b, preferred_element_type=F32)


def _dot_nt(a, b):
    return lax.dot_general(a, b, (((1,), (1,)), ((), ())), preferred_element_type=F32)


def _dot_tn(a, b):
    return lax.dot_general(a, b, (((0,), (0,)), ((), ())), preferred_element_type=F32)


def _modulate(x, g, sc, sh):
    tm, d = x.shape
    y = x * lax.rsqrt(jnp.mean(x * x, axis=-1, keepdims=True) + EPS) * g
    y = y.reshape(tm // CHUNK, CHUNK, d) * (1.0 + sc[:, None, :]) + sh[:, None, :]
    return y.reshape(tm, d)


def _gate_rows(y, g):
    tm, d = y.shape
    return (y.reshape(tm // CHUNK, CHUNK, d) * g[:, None, :]).reshape(tm, d)


def _rope(x, cos, sp, sm, half):
    blocks = []
    for j in range(x.shape[1] // LANES):
        xb = x[:, j * LANES:(j + 1) * LANES]
        blocks.append(xb * cos + pltpu.roll(xb, half, 1) * sp + pltpu.roll(xb, LANES - half, 1) * sm)
    return blocks[0] if len(blocks) == 1 else jnp.concatenate(blocks, axis=1)


def _rope_tables(pos, rot, head):
    half = rot // 2
    inv = ROPE_THETA ** (-jnp.arange(half, dtype=F32) * 2.0 / rot)
    ang = pos.astype(F32)[:, None] * inv[None, :]
    cos, sin = jnp.cos(ang), jnp.sin(ang)
    p = pos.shape[0]
    one = jnp.ones((p, head - rot), F32)
    zero = jnp.zeros((p, head - rot), F32)
    zh = jnp.zeros((p, half), F32)
    c = jnp.concatenate([cos, cos, one], axis=1)
    s_plus = jnp.concatenate([zh, sin, zero], axis=1)
    s_minus = jnp.concatenate([-sin, zh, zero], axis=1)
    rep = LANES // head
    return tuple(jnp.tile(t, (1, rep)) for t in (c, s_plus, s_minus))


CAST_BLOCK_BYTES = 6 * 1024 * 1024


def _cast_kernel(w_ref, o_ref):
    o_ref[...] = w_ref[...].astype(o_ref.dtype)


def _to_bf16(w):
    r, c = w.shape
    tr = next(t for t in (2048, 1024, 512, 256, 128, 64, 32, 16) if r % t == 0 and t * c * 4 <= CAST_BLOCK_BYTES)
    return pl.pallas_call(
        _cast_kernel,
        grid=(r // tr,),
        in_specs=[pl.BlockSpec((tr, c), lambda i: (i, 0))],
        out_specs=pl.BlockSpec((tr, c), lambda i: (i, 0)),
        out_shape=jax.ShapeDtypeStruct((r, c), BF16),
        compiler_params=_cparams(("arbitrary",)),
        name="to_bf16",
    )(w)


def _ada_kernel(c_ref, w_ref, b_ref, o_ref):
    c = c_ref[...]
    h = (c * jax.nn.sigmoid(c)).astype(BF16)
    o_ref[...] = _dot(h, w_ref[...].astype(BF16)) + b_ref[...]


def _ada(c, w_ada, b_ada):
    depth, d, n = w_ada.shape
    ns = c.shape[0]
    tn = _pick(n, (1024, 512, 256, 128))
    return pl.pallas_call(
        _ada_kernel,
        grid=(depth, n // tn),
        in_specs=[pl.BlockSpec((ns, d), lambda l, j: (0, 0)),
                  pl.BlockSpec((None, d, tn), lambda l, j: (l, 0, j)),
                  pl.BlockSpec((None, 1, tn), lambda l, j: (l, 0, j))],
        out_specs=pl.BlockSpec((None, ns, tn), lambda l, j: (l, 0, j)),
        out_shape=jax.ShapeDtypeStruct((depth, ns, n), F32),
        compiler_params=_cparams(("arbitrary", "arbitrary")),
        name="ada",
    )(c, w_ada, b_ada.reshape(depth, 1, n))


def _col_tiles(lo, hi, step=512):
    return [(a, min(a + step, hi)) for a in range(lo, hi, step)]


E_QA, E_KA, E_VA, E_QB, E_KB, E_VB, E_GT, E_OB, E_END = 0, 1024, 1152, 1280, 1792, 2304, 3328, 3456, 4480


def _pad_cols(w, width):
    return jnp.pad(w, ((0, 0), (0, width - w.shape[1])))


def _even_weight(w):
    qa, ka, va, qb, kb, vb, ib, fb, ob = jnp.split(
        w, np.cumsum((HA * DA, KVA * DA, KVA * DA, HB * DKB, HB * DKB, HB * DVB, HB, HB))[:].tolist(), axis=1)
    gates = _pad_cols(jnp.concatenate([ib, fb], axis=1), LANES)
    return jnp.concatenate([qa, ka, va, qb, kb, vb, gates, ob], axis=1).astype(BF16)


def _proj_even_kernel(x_ref, g_ref, sc_ref, sh_ref, w_ref, gb_ref, cos_ref, sp_ref, sm_ref,
                      qa_o, ka_o, va_o, qb_o, kb_o, vb_o, gt_o, og_o):
    h = _modulate(x_ref[...], g_ref[...], sc_ref[...], sh_ref[...]).astype(BF16)
    cos, sp, sm = cos_ref[...], sp_ref[...], sm_ref[...]
    half = ROT_A // 2

    def mm(lo, hi):
        return _dot(h, w_ref[:, lo:hi])

    for lo, hi in _col_tiles(E_QA, E_KA):
        qa_o[:, lo - E_QA:hi - E_QA] = _rope(mm(lo, hi), cos, sp, sm, half).astype(qa_o.dtype)
    ka_o[...] = _rope(mm(E_KA, E_VA), cos, sp, sm, half)
    va_o[...] = mm(E_VA, E_QB)
    for lo, hi in _col_tiles(E_QB, E_KB):
        qb_o[:, lo - E_QB:hi - E_QB] = (mm(lo, hi) * (DKB ** -0.5)).astype(qb_o.dtype)
    for lo, hi in _col_tiles(E_KB, E_VB):
        kb_o[:, lo - E_KB:hi - E_KB] = mm(lo, hi).astype(kb_o.dtype)
    for lo, hi in _col_tiles(E_VB, E_GT):
        vb_o[:, lo - E_VB:hi - E_VB] = mm(lo, hi).astype(vb_o.dtype)
    t = mm(E_GT, E_OB) + gb_ref[...]
    lane = lax.broadcasted_iota(I32, t.shape, 1)
    gt_o[...] = jnp.where(lane < HB, t, jax.nn.log_sigmoid(t))
    for lo, hi in _col_tiles(E_OB, E_END):
        og_o[:, lo - E_OB:hi - E_OB] = jax.nn.sigmoid(mm(lo, hi)).astype(og_o.dtype)


(O_QC, O_KC, O_VC, O_QI, O_KI, O_WI, O_QDN, O_QDP, O_CKV, O_KPE, O_END) = (
    0, 1024, 1280, 1536, 2560, 2688, 2816, 3840, 4352, 4864, 4992)


def _odd_weight(w):
    qc, kc, vc, qi, ki, wi, qd, ckv, kpe = jnp.split(
        w, np.cumsum((HC * DC, KVC * DC, KVC * DC, HI * DI, DI, HI, HD * (DNOPE + DROPE), KV_LORA)).tolist(), axis=1)
    d = w.shape[0]
    qd = qd.reshape(d, HD, DNOPE + DROPE)
    qdn = qd[:, :, :DNOPE].reshape(d, HD * DNOPE)
    qdp = qd[:, :, DNOPE:].reshape(d, HD * DROPE)
    return jnp.concatenate([qc, kc, vc, qi, _pad_cols(ki, LANES), _pad_cols(wi, LANES), qdn, qdp, ckv,
                            _pad_cols(kpe, LANES)], axis=1).astype(BF16)


def _proj_odd_kernel(x_ref, g_ref, sc_ref, sh_ref, w_ref, gckv_ref,
                     ci_ref, spi_ref, smi_ref, cc_ref, spc_ref, smc_ref, cd_ref, spd_ref, smd_ref,
                     qc_o, kc_o, vc_o, qi_o, ki_o, wi_o, qd_o, ckv_o, kpe_o,
                     kcb_o, vcb_o, kib_o, ckvb_o, kpeb_o):
    h = _modulate(x_ref[...], g_ref[...], sc_ref[...], sh_ref[...]).astype(BF16)
    tab_i = (ci_ref[...], spi_ref[...], smi_ref[...], ROT_I // 2)
    tab_c = (cc_ref[...], spc_ref[...], smc_ref[...], ROT_C // 2)
    tab_d = (cd_ref[...], spd_ref[...], smd_ref[...], DROPE // 2)

    def mm(lo, hi):
        return _dot(h, w_ref[:, lo:hi])

    for lo, hi in _col_tiles(O_QC, O_KC):
        qc_o[:, lo - O_QC:hi - O_QC] = (_rope(mm(lo, hi), *tab_c) * (DC ** -0.5)).astype(qc_o.dtype)
    kc = _rope(mm(O_KC, O_VC), *tab_c)
    kc_o[...] = kc
    kcb_o[...] = kc.astype(BF16)
    vc = mm(O_VC, O_QI)
    vc_o[...] = vc
    vcb_o[...] = vc.astype(BF16)
    for lo, hi in _col_tiles(O_QI, O_KI):
        qi_o[:, lo - O_QI:hi - O_QI] = _rope(mm(lo, hi), *tab_i).astype(qi_o.dtype)
    ki = _rope(mm(O_KI, O_WI), *tab_i)[:, :DI]
    ki_o[...] = ki
    kib_o[...] = ki.astype(BF16)
    wi_o[...] = mm(O_WI, O_QDN) * (HI ** -0.5 * DI ** -0.5)
    for lo, hi in _col_tiles(O_QDN, O_QDP):
        qn = (mm(lo, hi) * MLA_SCALE).astype(qd_o.dtype)
        for k in range((hi - lo) // DNOPE):
            hh = (lo - O_QDN) // DNOPE + k
            qd_o[:, hh * QD_SLOT:hh * QD_SLOT + DNOPE] = qn[:, k * DNOPE:(k + 1) * DNOPE]
    qp = _rope(mm(O_QDP, O_CKV), *tab_d) * MLA_SCALE
    low_lanes = lax.broadcasted_iota(I32, (qp.shape[0], LANES), 1) < DROPE
    for hh in range(HD):
        pair = qp[:, (hh // 2) * LANES:(hh // 2 + 1) * LANES]
        if hh % 2:
            pair = pltpu.roll(pair, DROPE, 1)
        qd_o[:, hh * QD_SLOT + DNOPE:(hh + 1) * QD_SLOT] = jnp.where(low_lanes, pair, 0.0).astype(qd_o.dtype)
    c = mm(O_CKV, O_KPE)
    c = c * lax.rsqrt(jnp.mean(c * c, axis=-1, keepdims=True) + EPS) * gckv_ref[...]
    ckv_o[...] = c
    ckvb_o[...] = c.astype(BF16)
    kpe = _rope(mm(O_KPE, O_END), *tab_d)[:, :DROPE]
    kpe_o[...] = kpe
    kpeb_o[...] = kpe.astype(BF16)


def _token_tile(t):
    return _pick(t, (512,))


def _proj_in(kernel_fn, name, x, modc, layer, g, w, extra, tables, tab_map, outs):
    t, d = x.shape
    tm = _token_tile(t)
    rows = tm // CHUNK
    in_specs = [pl.BlockSpec((tm, d), lambda i: (i, 0)),
                _resident((1, d), lambda i: (0, 0)),
                pl.BlockSpec((None, rows, d), lambda i: (layer, i, 1)),
                pl.BlockSpec((None, rows, d), lambda i: (layer, i, 0)),
                _resident(w.shape, lambda i: (0, 0)),
                _resident(extra.shape, lambda i: (0, 0))]
    in_specs += [pl.BlockSpec((tm, LANES), lambda i: (tab_map(i), 0)) for _ in tables]
    return pl.pallas_call(
        kernel_fn,
        grid=(t // tm,),
        in_specs=in_specs,
        out_specs=[pl.BlockSpec((tm, wd), lambda i: (i, 0)) for wd, _ in outs],
        out_shape=[jax.ShapeDtypeStruct((t, wd), dt) for wd, dt in outs],
        compiler_params=_cparams(("arbitrary",)),
        name=name,
    )(x, g, modc, modc, w, extra, *tables)


def _swa_kernel(tq, tiles_per_seq, sink_ref, q_ref, kw_ref, kc_ref, vw_ref, vc_ref, *rest):
    o_ref = rest[-1]
    kb = jnp.concatenate([kw_ref[...], kc_ref[...]], axis=0).astype(BF16)
    vb = jnp.concatenate([vw_ref[...], vc_ref[...]], axis=0).astype(BF16)
    qchunk = lax.broadcasted_iota(I32, (tq, WINDOW + tq), 0) // CHUNK
    kchunk = lax.broadcasted_iota(I32, (tq, WINDOW + tq), 1) // CHUNK
    valid = jnp.logical_and(kchunk >= qchunk, kchunk <= qchunk + WIN_CHUNKS)
    if tiles_per_seq:
        has_window = pl.program_id(0) % tiles_per_seq != 0
        valid = jnp.logical_and(valid, jnp.logical_or(kchunk >= WIN_CHUNKS, has_window))
    bias = jnp.where(valid, 0.0, NEG)
    group = HA // KVA
    for hq in range(HA):
        kv = hq // group
        q = q_ref[:, hq * DA:(hq + 1) * DA]
        s = _dot_nt(q, kb[:, kv * DA:(kv + 1) * DA]) * (DA ** -0.5) + bias
        sink = sink_ref[hq]
        m = jnp.maximum(jnp.max(s, axis=-1, keepdims=True), sink)
        p = jnp.exp(s - m)
        den = jnp.sum(p, axis=-1, keepdims=True) + jnp.exp(sink - m)
        o = _dot(p.astype(BF16), vb[:, kv * DA:(kv + 1) * DA]) / den
        o_ref[:, hq * DA:(hq + 1) * DA] = o.astype(o_ref.dtype)


def _swa(qa, ka, va, win_k, win_v, sinks, tp, s):
    t = qa.shape[0]
    tq = _pick(s, (2 * WINDOW, WINDOW))
    wpt = tq // WINDOW
    qspec = lambda rows, off: pl.BlockSpec((rows, HA * DA), lambda g: (g + off, 0))
    kvspec = lambda rows, off: pl.BlockSpec((rows, KVA * DA), lambda g: (g + off, 0))
    wspec_p = pl.BlockSpec((WINDOW, KVA * DA), lambda g: (jnp.maximum(g * wpt - 1, 0), 0))
    smem = pl.BlockSpec(memory_space=pltpu.SMEM)
    out_shape = jax.ShapeDtypeStruct((t, HA * DA), BF16)
    oa = pl.pallas_call(
        functools.partial(_swa_kernel, tq, s // tq),
        grid=(tp // tq,),
        in_specs=[smem, qspec(tq, 0), wspec_p, kvspec(tq, 0), wspec_p, kvspec(tq, 0)],
        out_specs=qspec(tq, 0),
        out_shape=out_shape,
        compiler_params=_cparams(("arbitrary",)),
        name="swa_prompt",
    )(sinks, qa, ka, ka, va, va)
    off = tp // CHUNK
    return pl.pallas_call(
        functools.partial(_swa_kernel, CHUNK, 0),
        grid=((t - tp) // CHUNK,),
        in_specs=[smem, qspec(CHUNK, off), kvspec(WINDOW, 0), kvspec(CHUNK, off), kvspec(WINDOW, 0),
                  kvspec(CHUNK, off), pl.BlockSpec(memory_space=pl.ANY)],
        out_specs=qspec(CHUNK, off),
        out_shape=out_shape,
        input_output_aliases={6: 0},
        compiler_params=_cparams(("arbitrary",)),
        name="swa_sample",
    )(sinks, qa, win_k, ka, win_v, va, oa)


def _mlstm_kernel(n_prompt_chunks, seq_chunks, q_ref, k_ref, v_ref, gt_ref, og_ref, gmh_ref,
                  c0_ref, n0_ref, m0_ref, h_o, c_o, n_o, m_o, c_s, n_s, m_s):
    g = pl.program_id(0)
    is_prompt = g < n_prompt_chunks
    c_idx = g % seq_chunks
    first = jnp.logical_or(jnp.logical_not(is_prompt), c_idx == 0)
    last = jnp.logical_or(jnp.logical_not(is_prompt), c_idx == seq_chunks - 1)

    @pl.when(first)
    def _():
        c_s[...] = c0_ref[...]
        n_s[...] = n0_ref[...]
        m_s[...] = m0_ref[...]

    gt = gt_ref[...]
    gt_t = gt.T
    row = lax.broadcasted_iota(I32, (CHUNK, CHUNK), 0)
    col = lax.broadcasted_iota(I32, (CHUNK, CHUNK), 1)
    causal = col <= row
    for hh in range(HB):
        ig_row = gt_t[hh:hh + 1, :]
        lf_row = gt_t[HB + hh:HB + hh + 1, :]
        ig_col = gt[:, hh:hh + 1]
        lf_col = gt[:, HB + hh:HB + hh + 1]
        b_col = jnp.sum(jnp.where(causal, lf_row, 0.0), axis=1, keepdims=True)
        b_row = jnp.sum(jnp.where(row <= col, lf_col, 0.0), axis=0, keepdims=True)
        m_prev = m_s[:, hh:hh + 1]
        d = jnp.where(causal, b_col - b_row + ig_row, NEG)
        inter = b_col + m_prev
        m_t = jnp.maximum(inter, jnp.max(d, axis=1, keepdims=True))
        w_intra = jnp.exp(d - m_t)
        w_inter = jnp.exp(inter - m_t)
        q = q_ref[:, hh * DKB:(hh + 1) * DKB]
        k = k_ref[:, hh * DKB:(hh + 1) * DKB]
        v = v_ref[:, hh * DVB:(hh + 1) * DVB]
        c_prev = c_s[hh]
        n_prev = n_s[hh:hh + 1, :]
        a = w_intra * _dot_nt(q, k)
        num = _dot(a.astype(BF16), v) + w_inter * _dot(q, c_prev.astype(BF16))
        den = (jnp.sum(a, axis=1, keepdims=True)
               + w_inter * jnp.sum(q.astype(F32) * n_prev, axis=1, keepdims=True))
        hv = num / jnp.maximum(jnp.abs(den), jnp.exp(-m_t))
        b_end = b_col[CHUNK - 1:CHUNK, :]
        g_row = b_end - b_row + ig_row
        g_col = b_end - b_col + ig_col
        m_new = jnp.maximum(b_end + m_prev, jnp.max(g_row, axis=1, keepdims=True))
        w_s = jnp.exp(g_col - m_new)
        w_c = jnp.exp(b_end + m_prev - m_new)
        c_s[hh] = w_c * c_prev + _dot_tn(k, (w_s * v.astype(F32)).astype(BF16))
        n_s[hh:hh + 1, :] = w_c * n_prev + jnp.sum(w_s * k.astype(F32), axis=0, keepdims=True)
        m_s[:, hh:hh + 1] = m_new
        y = hv * lax.rsqrt(jnp.mean(hv * hv, axis=-1, keepdims=True) + EPS)
        y = y * gmh_ref[:, hh * DVB:(hh + 1) * DVB] * og_ref[:, hh * DVB:(hh + 1) * DVB].astype(F32)
        h_o[:, hh * DVB:(hh + 1) * DVB] = y.astype(h_o.dtype)

    @pl.when(last)
    def _():
        c_o[...] = c_s[...]
        n_o[...] = n_s[...]
        m_o[...] = m_s[...]


def _mlstm(qb, kb, vb, gates, og, g_mh, c0, n0, m0, n_prompt_chunks, seq_chunks, n_prompt_seqs):
    t = qb.shape[0]
    nch = t // CHUNK
    ns = c0.shape[0]

    def seq(g):
        return jnp.where(g < n_prompt_chunks, g // seq_chunks, n_prompt_seqs + g - n_prompt_chunks)

    tok = lambda w: pl.BlockSpec((CHUNK, w), lambda g: (g, 0))
    st_specs = [pl.BlockSpec((None, HB, DKB, DVB), lambda g: (seq(g), 0, 0, 0)),
                pl.BlockSpec((None, HB, DKB), lambda g: (seq(g), 0, 0)),
                pl.BlockSpec((None, 1, HB), lambda g: (seq(g), 0, 0))]
    return pl.pallas_call(
        functools.partial(_mlstm_kernel, n_prompt_chunks, seq_chunks),
        grid=(nch,),
        in_specs=[tok(HB * DKB), tok(HB * DKB), tok(HB * DVB), tok(LANES), tok(HB * DVB),
                  _resident((1, HB * DVB), lambda g: (0, 0))] + st_specs,
        out_specs=[tok(HB * DVB)] + st_specs,
        out_shape=[jax.ShapeDtypeStruct((t, HB * DVB), BF16),
                   jax.ShapeDtypeStruct((ns, HB, DKB, DVB), F32),
                   jax.ShapeDtypeStruct((ns, HB, DKB), F32),
                   jax.ShapeDtypeStruct((ns, 1, HB), F32)],
        scratch_shapes=[pltpu.VMEM((HB, DKB, DVB), F32), pltpu.VMEM((HB, DKB), F32),
                        pltpu.VMEM((1, HB), F32)],
        compiler_params=_cparams(("arbitrary",)),
        name="mlstm",
    )(qb, kb, vb, gates, og, g_mh, c0, n0, m0)


INT_MIN = -2 ** 31


def _order_key(x):
    b = pltpu.bitcast(x, I32)
    return jnp.where(b >= 0, b, b ^ jnp.int32(0x7FFFFFFF))


def _dsa_kernel(tq, kblk, l_valid, l_pad, topk, causal,
                qc_ref, qi_ref, wi_ref, k_ref, v_ref, ki_ref, *rest):
    o_ref, key_s = rest[-2:]
    i = pl.program_id(1)
    q0 = i * tq
    rowq = lax.broadcasted_iota(I32, (tq, 1), 0)
    if causal:
        lim = ((q0 + rowq) // CHUNK + 1) * CHUNK
        nkb = (q0 + tq + kblk - 1) // kblk
    else:
        lim = jnp.full((tq, 1), l_valid, I32)
        nkb = l_pad // kblk
    lane = lax.broadcasted_iota(I32, (tq, kblk), 1)

    wi = wi_ref[...]

    def score_block(kb, carry):
        start = pl.multiple_of(kb * kblk, kblk)
        kib = ki_ref[pl.ds(start, kblk), :]
        acc = jnp.zeros((tq, kblk), F32)
        for hh in range(HI):
            s = _dot_nt(qi_ref[:, hh * DI:(hh + 1) * DI], kib)
            acc = acc + wi[:, hh:hh + 1] * jnp.maximum(s, 0.0)
        acc = acc + 0.0
        acc = jnp.where(start + lane < lim, acc, NEG)
        key_s[kb] = _order_key(acc)
        return carry

    lax.fori_loop(0, nkb, score_block, 0)

    lane1 = lax.broadcasted_iota(I32, (tq, LANES), 1)

    def count(pred_fn):
        def blk(kb, part):
            start = kb * kblk
            for c in range(kblk // LANES):
                keys = key_s[kb, :, c * LANES:(c + 1) * LANES]
                part = part + pred_fn(keys, start + c * LANES + lane1).astype(I32)
            return part
        part = lax.fori_loop(0, nkb, blk, jnp.zeros((tq, LANES), I32))
        return jnp.sum(part, axis=1, keepdims=True)

    def thr_bit(carry):
        b, thr, n_ge, _ = carry
        cand = thr + lax.shift_left(jnp.int32(1), 31 - b)
        cand_b = jnp.broadcast_to(cand, (tq, LANES))
        cnt = count(lambda keys, idx: keys >= cand_b)
        keep = cnt >= topk
        n_ge = jnp.where(keep, cnt, n_ge)
        return b + 1, jnp.where(keep, cand, thr), n_ge, jnp.max(n_ge)

    _, thr, n_ge, _ = lax.while_loop(
        lambda c: jnp.logical_and(c[0] < 32, c[3] > topk), thr_bit,
        (jnp.int32(0), jnp.full((tq, 1), INT_MIN, I32), jnp.full((tq, 1), l_pad, I32), jnp.int32(l_pad)))

    nbits = max(1, (l_pad - 1).bit_length())

    def tie_cut():
        thr_b = jnp.broadcast_to(thr, (tq, LANES))
        need = topk - count(lambda keys, idx: keys > thr_b)

        def cut_bit(b, cut):
            cand = cut + lax.shift_left(jnp.int32(1), nbits - 1 - b)
            cand_b = jnp.broadcast_to(cand, (tq, LANES))
            cnt = count(lambda keys, idx: jnp.logical_and(keys == thr_b, idx < cand_b))
            return jnp.where(cnt < need, cand, cut)

        return lax.fori_loop(0, nbits, cut_bit, jnp.zeros((tq, 1), I32))

    cut = lax.cond(jnp.max(n_ge) > topk, tie_cut, lambda: jnp.full((tq, 1), l_pad, I32))

    group = HC // KVC
    qs = [jnp.concatenate([qc_ref[:, (kv * group + j) * DC:(kv * group + j + 1) * DC]
                           for j in range(group)], axis=0) for kv in range(KVC)]

    def attend_block(kb, carry):
        start = pl.multiple_of(kb * kblk, kblk)
        keys = key_s[kb]
        idx = start + lane
        sel = jnp.logical_or(keys > thr, jnp.logical_and(keys == thr, idx <= cut))
        sel = jnp.logical_and(sel, idx < lim)
        bias = jnp.where(sel, 0.0, NEG)
        new = []
        for kv in range(KVC):
            m, l, acc = carry[kv]
            kk = k_ref[pl.ds(start, kblk), kv * DC:(kv + 1) * DC]
            vv = v_ref[pl.ds(start, kblk), kv * DC:(kv + 1) * DC]
            s = _dot_nt(qs[kv], kk)
            s = jnp.concatenate([s[j * tq:(j + 1) * tq] + bias for j in range(group)], axis=0)
            m_new = jnp.maximum(m, jnp.max(s, axis=1, keepdims=True))
            alpha = jnp.exp(m - m_new)
            p = jnp.exp(s - m_new)
            l = alpha * l + jnp.sum(p, axis=1, keepdims=True)
            acc = alpha * acc + _dot(p.astype(BF16), vv)
            new.append((m_new, l, acc))
        return tuple(new)

    init = tuple((jnp.full((group * tq, 1), NEG, F32), jnp.zeros((group * tq, 1), F32),
                  jnp.zeros((group * tq, DC), F32)) for _ in range(KVC))
    res = lax.fori_loop(0, nkb, attend_block, init)
    for kv in range(KVC):
        _, l, acc = res[kv]
        o = acc / l
        for j in range(group):
            hq = kv * group + j
            o_ref[:, hq * DC:(hq + 1) * DC] = o[j * tq:(j + 1) * tq, :].astype(o_ref.dtype)


def _dsa(qc, qi, wi, k, v, ki, prev_out, nseq, sq, l_valid, l_pad, causal, tq, q_off, name):
    topk = min(INDEX_TOPK, l_valid // 4)
    kblk = _pick(l_pad, (512, 256, 128))
    assert kblk >= topk and sq % tq == 0 and q_off % tq == 0
    nq = sq // tq
    qmap = lambda b, i: (q_off // tq + b * nq + i, 0)
    kmap = lambda b, i: (b, 0)
    in_specs = [pl.BlockSpec((tq, HC * DC), qmap), pl.BlockSpec((tq, HI * DI), qmap),
                pl.BlockSpec((tq, LANES), qmap),
                _resident((l_pad, KVC * DC), kmap), _resident((l_pad, KVC * DC), kmap),
                _resident((l_pad, DI), kmap)]
    args = [qc, qi, wi, k, v, ki]
    aliases = {}
    if prev_out is not None:
        in_specs.append(pl.BlockSpec(memory_space=pl.ANY))
        aliases = {len(args): 0}
        args.append(prev_out)
    return pl.pallas_call(
        functools.partial(_dsa_kernel, tq, kblk, l_valid, l_pad, topk, causal),
        grid=(nseq, nq),
        in_specs=in_specs,
        out_specs=pl.BlockSpec((tq, HC * DC), qmap),
        out_shape=jax.ShapeDtypeStruct((qc.shape[0], HC * DC), BF16),
        scratch_shapes=[pltpu.VMEM((l_pad // kblk, tq, kblk), I32)],
        input_output_aliases=aliases,
        compiler_params=_cparams(("arbitrary", "arbitrary")),
        name=name,
    )(*args)


def _matmul_kernel(a_ref, w_ref, o_ref):
    o_ref[...] = _dot(a_ref[...], w_ref[...]).astype(o_ref.dtype)


def _latent_up(ckv, w_ukv):
    r = ckv.shape[0]
    tm = _pick(r, (1024, 512, 256, 128, 64))
    n = w_ukv.shape[1]
    return pl.pallas_call(
        _matmul_kernel,
        grid=(r // tm,),
        in_specs=[pl.BlockSpec((tm, KV_LORA), lambda i: (i, 0)), _resident(w_ukv.shape, lambda i: (0, 0))],
        out_specs=pl.BlockSpec((tm, n), lambda i: (i, 0)),
        out_shape=jax.ShapeDtypeStruct((r, n), BF16),
        compiler_params=_cparams(("arbitrary",)),
        name="latent_up",
    )(ckv, w_ukv)


def _mla_last_block(i, tq, kblk):
    return (i * tq + tq - 1) // kblk


def _latent_up_t_kernel(ckv_ref, kpe_ref, wk_ref, wvt_ref, k_o, vt_o):
    ckv = ckv_ref[...]
    kn = _dot(ckv, wk_ref[...]).astype(k_o.dtype)
    kpe = jnp.concatenate([kpe_ref[...], jnp.zeros((kpe_ref.shape[0], LANES - DROPE), k_o.dtype)], axis=1)
    for hh in range(HD):
        k_o[:, hh * QD_SLOT:hh * QD_SLOT + DNOPE] = kn[:, hh * DNOPE:(hh + 1) * DNOPE]
        k_o[:, hh * QD_SLOT + DNOPE:(hh + 1) * QD_SLOT] = kpe
    vt_o[...] = _dot_nt(wvt_ref[...], ckv).astype(vt_o.dtype)


def _latent_up_t(ckv, kpe, w_uk, w_uv_t, rows):
    tm = _pick(rows, (512, 256, 128))
    return pl.pallas_call(
        _latent_up_t_kernel,
        grid=(rows // tm,),
        in_specs=[pl.BlockSpec((tm, KV_LORA), lambda i: (i, 0)), pl.BlockSpec((tm, DROPE), lambda i: (i, 0)),
                  _resident(w_uk.shape, lambda i: (0, 0)), _resident(w_uv_t.shape, lambda i: (0, 0))],
        out_specs=[pl.BlockSpec((tm, HD * QD_SLOT), lambda i: (i, 0)),
                   pl.BlockSpec((HD * DVD, tm), lambda i: (0, i))],
        out_shape=[jax.ShapeDtypeStruct((rows, HD * QD_SLOT), BF16),
                   jax.ShapeDtypeStruct((HD * DVD, rows), BF16)],
        compiler_params=_cparams(("arbitrary",)),
        name="latent_up_t",
    )(ckv, kpe, w_uk, w_uv_t)


def _mla_t_kernel(tq, kblk, it_ref, jt_ref, q_ref, k_ref, vt_ref, o_ref, m_s, l_s, acc_s):
    step_id = pl.program_id(1)
    i = it_ref[step_id]
    j = jt_ref[step_id]
    last = _mla_last_block(i, tq, kblk)

    @pl.when(j == 0)
    def _():
        m_s[...] = jnp.full(m_s.shape, NEG, F32)
        l_s[...] = jnp.zeros(l_s.shape, F32)
        acc_s[...] = jnp.zeros(acc_s.shape, F32)

    def step(masked):
        if masked:
            kidx = j * kblk + lax.broadcasted_iota(I32, (kblk, tq), 0)
            qpos = i * tq + lax.broadcasted_iota(I32, (kblk, tq), 1)
            bias = jnp.where(kidx < (qpos // CHUNK + 1) * CHUNK, 0.0, NEG)
        for hh in range(HD):
            s = _dot_nt(k_ref[:, hh * QD_SLOT:(hh + 1) * QD_SLOT], q_ref[:, hh * QD_SLOT:(hh + 1) * QD_SLOT])
            if masked:
                s = s + bias
            m = m_s[hh]
            m_new = jnp.maximum(m, jnp.max(s, axis=0, keepdims=True))
            alpha = jnp.exp(m - m_new)
            p = jnp.exp(s - m_new)
            l_s[hh] = alpha * l_s[hh] + jnp.sum(p, axis=0, keepdims=True)
            acc_s[hh] = alpha * acc_s[hh] + _dot(vt_ref[hh * DVD:(hh + 1) * DVD, :], p.astype(BF16))
            m_s[hh] = m_new

    n_full = (i * tq + CHUNK) // kblk
    pl.when(j < n_full)(functools.partial(step, False))
    pl.when(j >= n_full)(functools.partial(step, True))

    @pl.when(j == last)
    def _():
        for hh in range(HD):
            o_ref[:, hh * DVD:(hh + 1) * DVD] = (acc_s[hh] / l_s[hh]).T.astype(o_ref.dtype)


def _mla_t(qd, k_cat, v_t, nseq, sq, tq):
    kblk = _pick(sq, (512, 256, 128))
    nq, nk = sq // tq, sq // kblk
    pairs = [(i, j) for i in range(nq) for j in range(_mla_last_block(i, tq, kblk) + 1)]
    it = jnp.asarray([p[0] for p in pairs], I32)
    jt = jnp.asarray([p[1] for p in pairs], I32)
    qmap = lambda b, t, it, jt: (b * nq + it[t], 0)
    return pl.pallas_call(
        functools.partial(_mla_t_kernel, tq, kblk),
        grid_spec=pltpu.PrefetchScalarGridSpec(
            num_scalar_prefetch=2,
            grid=(nseq, len(pairs)),
            in_specs=[pl.BlockSpec((tq, HD * QD_SLOT), qmap),
                      pl.BlockSpec((kblk, HD * QD_SLOT), lambda b, t, it, jt: (b * nk + jt[t], 0)),
                      pl.BlockSpec((HD * DVD, kblk), lambda b, t, it, jt: (0, b * nk + jt[t]))],
            out_specs=pl.BlockSpec((tq, HD * DVD), qmap),
            scratch_shapes=[pltpu.VMEM((HD, 1, tq), F32), pltpu.VMEM((HD, 1, tq), F32),
                            pltpu.VMEM((HD, DVD, tq), F32)]),
        out_shape=jax.ShapeDtypeStruct((qd.shape[0], HD * DVD), BF16),
        compiler_params=_cparams(("arbitrary", "arbitrary")),
        name="mla_prompt",
    )(it, jt, qd, k_cat, v_t)


def _mla_kernel(tq, kblk, l_valid, causal, it_ref, jt_ref, q_ref, kv_ref, kpe_ref, *rest):
    o_ref, m_s, l_s, acc_s = rest[-4:]
    step_id = pl.program_id(1)
    i = it_ref[step_id]
    j = jt_ref[step_id]
    last = _mla_last_block(i, tq, kblk) if causal else pl.cdiv(l_valid, kblk) - 1

    @pl.when(j == 0)
    def _():
        m_s[...] = jnp.full(m_s.shape, NEG, F32)
        l_s[...] = jnp.zeros(l_s.shape, F32)
        acc_s[...] = jnp.zeros(acc_s.shape, F32)

    def step(masked):
        if masked:
            idx = j * kblk + lax.broadcasted_iota(I32, (tq, kblk), 1)
            if causal:
                rowq = i * tq + lax.broadcasted_iota(I32, (tq, kblk), 0)
                valid = idx < (rowq // CHUNK + 1) * CHUNK
            else:
                valid = idx < l_valid
            bias = jnp.where(valid, 0.0, NEG)
        kpe = kpe_ref[...]
        for hh in range(HD):
            kn = kv_ref[:, hh * DNOPE:(hh + 1) * DNOPE]
            vv = kv_ref[:, HD * DNOPE + hh * DVD:HD * DNOPE + (hh + 1) * DVD]
            s = (_dot_nt(q_ref[:, hh * QD_SLOT:hh * QD_SLOT + DNOPE], kn)
                 + _dot_nt(q_ref[:, hh * QD_SLOT + DNOPE:hh * QD_SLOT + DNOPE + DROPE], kpe))
            if masked:
                s = s + bias
            m = m_s[hh]
            m_new = jnp.maximum(m, jnp.max(s, axis=1, keepdims=True))
            alpha = jnp.exp(m - m_new)
            p = jnp.exp(s - m_new)
            l_s[hh] = alpha * l_s[hh] + jnp.sum(p, axis=1, keepdims=True)
            acc_s[hh] = alpha * acc_s[hh] + _dot(p.astype(BF16), vv)
            m_s[hh] = m_new

    if causal:
        n_full = (i * tq + CHUNK) // kblk
    else:
        n_full = l_valid // kblk
    pl.when(j < n_full)(functools.partial(step, False))
    pl.when(j >= n_full)(functools.partial(step, True))

    @pl.when(j == last)
    def _():
        for hh in range(HD):
            o_ref[:, hh * DVD:(hh + 1) * DVD] = (acc_s[hh] / l_s[hh]).astype(o_ref.dtype)


def _mla(qd, kv_up, kpe, prev_out, nseq, sq, l_valid, l_pad, causal, tq, q_off, kv_off, name):
    kblk = _pick(l_pad, (512, 256, 128))
    assert q_off % tq == 0 and kv_off % kblk == 0
    nq, nk = sq // tq, l_pad // kblk
    n_blocks = (lambda i: (i * tq + tq - 1) // kblk + 1) if causal else (lambda i: -(-l_valid // kblk))
    pairs = [(i, j) for i in range(nq) for j in range(n_blocks(i))]
    it = jnp.asarray([p[0] for p in pairs], I32)
    jt = jnp.asarray([p[1] for p in pairs], I32)
    qmap = lambda b, t, it, jt: (q_off // tq + b * nq + it[t], 0)
    kvmap = lambda b, t, it, jt: (kv_off // kblk + b * nk + jt[t], 0)
    kpemap = lambda b, t, it, jt: (b * nk + jt[t], 0)
    in_specs = [pl.BlockSpec((tq, HD * QD_SLOT), qmap),
                pl.BlockSpec((kblk, HD * (DNOPE + DVD)), kvmap), pl.BlockSpec((kblk, DROPE), kpemap)]
    args = [it, jt, qd, kv_up, kpe]
    aliases = {}
    if prev_out is not None:
        in_specs.append(pl.BlockSpec(memory_space=pl.ANY))
        aliases = {len(args): 0}
        args.append(prev_out)
    return pl.pallas_call(
        functools.partial(_mla_kernel, tq, kblk, l_valid, causal),
        grid_spec=pltpu.PrefetchScalarGridSpec(
            num_scalar_prefetch=2,
            grid=(nseq, len(pairs)),
            in_specs=in_specs,
            out_specs=pl.BlockSpec((tq, HD * DVD), qmap),
            scratch_shapes=[pltpu.VMEM((HD, tq, 1), F32), pltpu.VMEM((HD, tq, 1), F32),
                            pltpu.VMEM((HD, tq, DVD), F32)]),
        out_shape=jax.ShapeDtypeStruct((qd.shape[0], HD * DVD), BF16),
        input_output_aliases=aliases,
        compiler_params=_cparams(("arbitrary", "arbitrary")),
        name=name,
    )(*args)


def _proj_out_kernel(a1_ref, a2_ref, w1_ref, w2_ref, x_ref, gate_ref, o_ref):
    y = _dot(a1_ref[...], w1_ref[...]) + _dot(a2_ref[...], w2_ref[...])
    o_ref[...] = x_ref[...] + _gate_rows(y, gate_ref[...])


def _proj_out(a1, a2, w_out, x, modc, layer):
    t, d = x.shape
    tm = _token_tile(t)
    rows = tm // CHUNK
    k1, k2 = a1.shape[1], a2.shape[1]
    assert k1 == k2
    w = _to_bf16(w_out)
    return pl.pallas_call(
        _proj_out_kernel,
        grid=(t // tm,),
        in_specs=[pl.BlockSpec((tm, k1), lambda i: (i, 0)), pl.BlockSpec((tm, k2), lambda i: (i, 0)),
                  _resident((k1, d), lambda i: (0, 0)), _resident((k2, d), lambda i: (1, 0)),
                  pl.BlockSpec((tm, d), lambda i: (i, 0)),
                  pl.BlockSpec((None, rows, d), lambda i: (layer, i, 2))],
        out_specs=pl.BlockSpec((tm, d), lambda i: (i, 0)),
        out_shape=jax.ShapeDtypeStruct((t, d), F32),
        compiler_params=_cparams(("arbitrary",)),
        name="proj_out",
    )(a1, a2, w, w, x, modc)


def _ffn_kernel(final_norm, x_ref, g_ref, sc_ref, sh_ref, gate_ref, wg_ref, wu_ref, wd_ref, gf_ref,
                o_ref, h_s, acc_s):
    f = pl.program_id(1)

    @pl.when(f == 0)
    def _():
        h_s[...] = _modulate(x_ref[...], g_ref[...], sc_ref[...], sh_ref[...]).astype(BF16)
        acc_s[...] = jnp.zeros(acc_s.shape, F32)

    h = h_s[...]
    a = _dot(h, wg_ref[...])
    u = _dot(h, wu_ref[...])
    acc_s[...] += _dot((a * jax.nn.sigmoid(a) * u).astype(BF16), wd_ref[...])

    @pl.when(f == pl.num_programs(1) - 1)
    def _():
        y = x_ref[...] + _gate_rows(acc_s[...], gate_ref[...])
        if final_norm:
            y = y * lax.rsqrt(jnp.mean(y * y, axis=-1, keepdims=True) + EPS) * gf_ref[...]
        o_ref[...] = y


def _ffn(x, modc, layer, g, wg, wu, wd, g_final, final_norm):
    t, d = x.shape
    f = wg.shape[1]
    tm = _token_tile(t)
    tf = _pick(f, (512, 256, 128))
    rows = tm // CHUNK
    modspec = lambda comp: pl.BlockSpec((None, rows, d), lambda i, j: (layer, i, comp))
    return pl.pallas_call(
        functools.partial(_ffn_kernel, final_norm),
        grid=(t // tm, f // tf),
        in_specs=[pl.BlockSpec((tm, d), lambda i, j: (i, 0)),
                  _resident((1, d), lambda i, j: (0, 0)),
                  modspec(4), modspec(3), modspec(5),
                  pl.BlockSpec((d, tf), lambda i, j: (0, j)),
                  pl.BlockSpec((d, tf), lambda i, j: (0, j)),
                  pl.BlockSpec((tf, d), lambda i, j: (j, 0)),
                  _resident((1, d), lambda i, j: (0, 0))],
        out_specs=pl.BlockSpec((tm, d), lambda i, j: (i, 0)),
        out_shape=jax.ShapeDtypeStruct((t, d), F32),
        scratch_shapes=[pltpu.VMEM((tm, d), BF16), pltpu.VMEM((tm, d), F32)],
        compiler_params=_cparams(("arbitrary", "arbitrary")),
        name="ffn",
    )(x, g, modc, modc, modc, _to_bf16(wg), _to_bf16(wu), _to_bf16(wd), g_final)


def _round_up(n, m):
    return (n + m - 1) // m * m


def kernel(x_prompt, x_sample, c_prompt, c_sample, cache_a_k, cache_a_v, state_b_c, state_b_n, state_b_m,
           cache_c_k, cache_c_v, cache_c_idx, cache_d_ckv, cache_d_kpe, w_ada, b_ada, g_norm_mix, g_norm_ffn,
           w_in_even, w_out_even, sinks_a, b_igate, b_fgate, g_mlstm, w_in_odd, w_out_odd, g_ckv, w_uk, w_uv,
           w_ffn_gate, w_ffn_up, w_ffn_down, g_final):
    nb, s, d = x_prompt.shape
    db, ds, _ = x_sample.shape
    assert ds == CHUNK and s % CHUNK == 0
    depth = w_ada.shape[0]
    past = cache_c_k.shape[2]
    tp, ts = nb * s, db * ds
    t = tp + ts
    tm = _token_tile(t)
    assert s % tm == 0 and tp % tm == 0
    seq_chunks = s // CHUNK
    npc = tp // CHUNK

    x = jnp.concatenate([x_prompt.reshape(tp, d), x_sample.reshape(ts, d)], axis=0)
    mod = _ada(jnp.concatenate([c_prompt, c_sample], axis=0), w_ada, b_ada)
    mod_p = jnp.broadcast_to(mod[:, :nb, None, :], (depth, nb, seq_chunks, 6 * d)).reshape(depth, npc, 6 * d)
    modc = jnp.concatenate([mod_p, mod[:, nb:]], axis=1)

    pos = jnp.concatenate([jnp.arange(s, dtype=I32), jnp.tile(past + jnp.arange(ds, dtype=I32), tm // ds)])
    tab_a = _rope_tables(pos, ROT_A, DA)
    tab_c = _rope_tables(pos, ROT_C, DC)
    tab_d = _rope_tables(pos, DROPE, DROPE)
    prompt_tiles, seq_tiles = tp // tm, s // tm
    tab_map = lambda i: jnp.where(i < prompt_tiles, i % seq_tiles, seq_tiles)

    w_ukv_all = jnp.concatenate([w_uk, w_uv], axis=2).astype(BF16)
    l_s = past + ds
    lp_s = _round_up(l_s, 512)

    def cat_past(cache, new, width):
        full = jnp.concatenate([cache.reshape(db, past, width), new.reshape(db, ds, width)], axis=1)
        full = jnp.pad(full, ((0, 0), (0, lp_s - l_s), (0, 0)))
        return full.astype(BF16).reshape(db * lp_s, width)

    even_p, even_s, odd_p, odd_s = [], [], [], []
    for l in range(depth):
        i = l // 2
        g_mix = g_norm_mix[l].reshape(1, d)
        if l % 2 == 0:
            gate_bias = _pad_cols(jnp.concatenate([b_igate[i], b_fgate[i]]).reshape(1, 2 * HB), LANES)
            qa, ka, va, qb, kb, vb, gates, og = _proj_in(
                _proj_even_kernel, "proj_even", x, modc, l, g_mix, _even_weight(w_in_even[i]), gate_bias,
                list(tab_a), tab_map,
                [(HA * DA, BF16), (KVA * DA, F32), (KVA * DA, F32), (HB * DKB, BF16), (HB * DKB, BF16),
                 (HB * DVB, BF16), (LANES, F32), (HB * DVB, BF16)])
            ka_s = ka[tp:].reshape(db, ds, KVA * DA)
            va_s = va[tp:].reshape(db, ds, KVA * DA)
            win_k = cache_a_k[i].reshape(db, WINDOW, KVA * DA)
            win_v = cache_a_v[i].reshape(db, WINDOW, KVA * DA)
            kband = jnp.concatenate([win_k, ka_s], axis=1)
            vband = jnp.concatenate([win_v, va_s], axis=1)
            oa = _swa(qa, ka, va, win_k.reshape(db * WINDOW, KVA * DA), win_v.reshape(db * WINDOW, KVA * DA),
                      sinks_a[i], tp, s)
            c0 = jnp.concatenate([jnp.zeros((nb, HB, DKB, DVB), F32), state_b_c[i]], axis=0)
            n0 = jnp.concatenate([jnp.zeros((nb, HB, DKB), F32), state_b_n[i]], axis=0)
            m0 = jnp.concatenate([jnp.zeros((nb, HB), F32), state_b_m[i]], axis=0).reshape(nb + db, 1, HB)
            hb, c_new, n_new, m_new = _mlstm(qb, kb, vb, gates, og, g_mlstm[i].reshape(1, HB * DVB),
                                             c0, n0, m0, npc, seq_chunks, nb)
            m_new = m_new.reshape(nb + db, HB)
            ka_p = ka[:tp].reshape(nb, s, KVA, DA)
            va_p = va[:tp].reshape(nb, s, KVA, DA)
            even_p.append((ka_p[:, -WINDOW:], va_p[:, -WINDOW:], c_new[:nb], n_new[:nb], m_new[:nb]))
            even_s.append((kband[:, -WINDOW:].reshape(db, WINDOW, KVA, DA),
                           vband[:, -WINDOW:].reshape(db, WINDOW, KVA, DA), c_new[nb:], n_new[nb:], m_new[nb:]))
            x = _proj_out(oa, hb, w_out_even[i], x, modc, l)
        else:
            qc, kc, vc, qi, ki, wi, qd, ckv, kpe, kcb, vcb, kib, ckvb, kpeb = _proj_in(
                _proj_odd_kernel, "proj_odd", x, modc, l, g_mix, _odd_weight(w_in_odd[i]),
                g_ckv[i].reshape(1, KV_LORA), list(tab_a) + list(tab_c) + list(tab_d), tab_map,
                [(HC * DC, BF16), (KVC * DC, F32), (KVC * DC, F32), (HI * DI, BF16), (DI, F32), (LANES, F32),
                 (HD * QD_SLOT, BF16), (KV_LORA, F32), (DROPE, F32),
                 (KVC * DC, BF16), (KVC * DC, BF16), (DI, BF16), (KV_LORA, BF16), (DROPE, BF16)])
            oc = _dsa(qc, qi, wi, kcb, vcb, kib, None, nb, s, s, s, True, 128, 0, "dsa_prompt")
            oc = _dsa(qc, qi, wi, cat_past(cache_c_k[i], kcb[tp:], KVC * DC),
                      cat_past(cache_c_v[i], vcb[tp:], KVC * DC), cat_past(cache_c_idx[i], kib[tp:], DI),
                      oc, db, ds, l_s, lp_s, False, ds, tp, "dsa_sample")
            k_cat, v_t = _latent_up_t(ckvb, kpeb, w_uk[i].astype(BF16), w_uv[i].T.astype(BF16), tp)
            od = _mla_t(qd, k_cat, v_t, nb, s, _pick(s, (256, 128)))
            kv_up = _latent_up(cat_past(cache_d_ckv[i], ckvb[tp:], KV_LORA), w_ukv_all[i])
            od = _mla(qd, kv_up, cat_past(cache_d_kpe[i], kpeb[tp:], DROPE), od,
                      db, ds, l_s, lp_s, False, ds, tp, 0, "mla_sample")
            odd_p.append((kc[:tp].reshape(nb, s, KVC, DC), vc[:tp].reshape(nb, s, KVC, DC),
                          ki[:tp].reshape(nb, s, DI), ckv[:tp].reshape(nb, s, KV_LORA),
                          kpe[:tp].reshape(nb, s, DROPE)))
            odd_s.append((kc[tp:].reshape(db, ds, KVC, DC), vc[tp:].reshape(db, ds, KVC, DC),
                          ki[tp:].reshape(db, ds, DI), ckv[tp:].reshape(db, ds, KV_LORA),
                          kpe[tp:].reshape(db, ds, DROPE)))
            x = _proj_out(oc, od, w_out_odd[i], x, modc, l)
        x = _ffn(x, modc, l, g_norm_ffn[l].reshape(1, d), w_ffn_gate[l], w_ffn_up[l], w_ffn_down[l],
                 g_final.reshape(1, d), l == depth - 1)

    y_prompt = x[:tp].reshape(nb, s, d)
    y_sample = x[tp:].reshape(db, ds, d)
    st_p = [jnp.stack([e[j] for e in even_p]) for j in range(5)] + [jnp.stack([o[j] for o in odd_p]) for j in range(5)]
    st_s = [jnp.stack([e[j] for e in even_s]) for j in range(5)] + [jnp.stack([o[j] for o in odd_s]) for j in range(5)]
    return (y_prompt, y_sample, *st_p, *st_s)
```

```python
import functools
import math

import jax
import jax.numpy as jnp
import numpy as np
from jax import lax
from jax.experimental import pallas as pl
from jax.experimental.pallas import tpu as pltpu

F32 = jnp.float32
BF16 = jnp.bfloat16
I32 = jnp.int32

CHUNK = 64
ROPE_THETA = 500000.0
EPS = 1e-6
NEG = -1e30
LANES = 128

HA, KVA, DA = 16, 2, 64
ROT_A = DA // 4
WINDOW = 128
WIN_CHUNKS = WINDOW // CHUNK
HB, DKB, DVB = 4, 128, 256
HC, KVC, DC = 8, 2, 128
ROT_C = DC // 4
HI, DI = 16, 64
ROT_I = DI // 4
INDEX_TOPK = 256
HD, DNOPE, DROPE, DVD, KV_LORA = 8, 128, 64, 128, 512
LOG2E = math.log2(math.e)
DSA_SCALE = DC ** -0.5 * LOG2E
MLA_SCALE = (DNOPE + DROPE) ** -0.5 * LOG2E
QD_SLOT = 2 * LANES

VMEM_LIMIT = 56 * 1024 * 1024


def _cparams(sem, vmem=VMEM_LIMIT):
    return pltpu.CompilerParams(dimension_semantics=sem, vmem_limit_bytes=vmem)


def _pick(n, options):
    for o in options:
        if n % o == 0:
            return o
    raise ValueError(f"no tile in {options} divides {n}")


def _resident(shape, index_map):
    return pl.BlockSpec(shape, index_map, pipeline_mode=pl.Buffered(1))


def _dot(a, b):
    return jnp.dot(a, b, preferred_element_type=F32)


def _dot_nt(a, b):
    return lax.dot_general(a, b, (((1,), (1,)), ((), ())), preferred_element_type=F32)


def _dot_tn(a, b):
    return lax.dot_general(a, b, (((0,), (0,)), ((), ())), preferred_element_type=F32)


def _modulate(x, g, sc, sh):
    tm, d = x.shape
    y = x * lax.rsqrt(jnp.mean(x * x, axis=-1, keepdims=True) + EPS) * g
    y = y.reshape(tm // CHUNK, CHUNK, d) * (1.0 + sc[:, None, :]) + sh[:, None, :]
    return y.reshape(tm, d)


def _gate_rows(y, g):
    tm, d = y.shape
    return (y.reshape(tm // CHUNK, CHUNK, d) * g[:, None, :]).reshape(tm, d)


def _rope(x, cos, sp, sm, half):
    blocks = []
    for j in range(x.shape[1] // LANES):
        xb = x[:, j * LANES:(j + 1) * LANES]
        blocks.append(xb * cos + pltpu.roll(xb, half, 1) * sp + pltpu.roll(xb, LANES - half, 1) * sm)
    return blocks[0] if len(blocks) == 1 else jnp.concatenate(blocks, axis=1)


def _rope_tables(pos, rot, head):
    half = rot // 2
    inv = ROPE_THETA ** (-jnp.arange(half, dtype=F32) * 2.0 / rot)
    ang = pos.astype(F32)[:, None] * inv[None, :]
    cos, sin = jnp.cos(ang), jnp.sin(ang)
    p = pos.shape[0]
    one = jnp.ones((p, head - rot), F32)
    zero = jnp.zeros((p, head - rot), F32)
    zh = jnp.zeros((p, half), F32)
    c = jnp.concatenate([cos, cos, one], axis=1)
    s_plus = jnp.concatenate([zh, sin, zero], axis=1)
    s_minus = jnp.concatenate([-sin, zh, zero], axis=1)
    rep = LANES // head
    return tuple(jnp.tile(t, (1, rep)) for t in (c, s_plus, s_minus))


CAST_BLOCK_BYTES = 6 * 1024 * 1024


def _cast_kernel(w_ref, o_ref):
    o_ref[...] = w_ref[...].astype(o_ref.dtype)


def _to_bf16(w):
    r, c = w.shape
    tr = next(t for t in (2048, 1024, 512, 256, 128, 64, 32, 16) if r % t == 0 and t * c * 4 <= CAST_BLOCK_BYTES)
    return pl.pallas_call(
        _cast_kernel,
        grid=(r // tr,),
        in_specs=[pl.BlockSpec((tr, c), lambda i: (i, 0))],
        out_specs=pl.BlockSpec((tr, c), lambda i: (i, 0)),
        out_shape=jax.ShapeDtypeStruct((r, c), BF16),
        compiler_params=_cparams(("arbitrary",)),
        name="to_bf16",
    )(w)


def _ada_kernel(c_ref, w_ref, b_ref, o_ref):
    c = c_ref[...]
    h = (c * jax.nn.sigmoid(c)).astype(BF16)
    o_ref[...] = _dot(h, w_ref[...].astype(BF16)) + b_ref[...]


def _ada(c, w_ada, b_ada):
    depth, d, n = w_ada.shape
    ns = c.shape[0]
    tn = _pick(n, (1024, 512, 256, 128))
    return pl.pallas_call(
        _ada_kernel,
        grid=(depth, n // tn),
        in_specs=[pl.BlockSpec((ns, d), lambda l, j: (0, 0)),
                  pl.BlockSpec((None, d, tn), lambda l, j: (l, 0, j)),
                  pl.BlockSpec((None, 1, tn), lambda l, j: (l, 0, j))],
        out_specs=pl.BlockSpec((None, ns, tn), lambda l, j: (l, 0, j)),
        out_shape=jax.ShapeDtypeStruct((depth, ns, n), F32),
        compiler_params=_cparams(("arbitrary", "arbitrary")),
        name="ada",
    )(c, w_ada, b_ada.reshape(depth, 1, n))


def _col_tiles(lo, hi, step=512):
    return [(a, min(a + step, hi)) for a in range(lo, hi, step)]


E_QA, E_KA, E_VA, E_QB, E_KB, E_VB, E_GT, E_OB, E_END = 0, 1024, 1152, 1280, 1792, 2304, 3328, 3456, 4480


def _pad_cols(w, width):
    return jnp.pad(w, ((0, 0), (0, width - w.shape[1])))


def _even_weight(w):
    qa, ka, va, qb, kb, vb, ib, fb, ob = jnp.split(
        w, np.cumsum((HA * DA, KVA * DA, KVA * DA, HB * DKB, HB * DKB, HB * DVB, HB, HB))[:].tolist(), axis=1)
    gates = _pad_cols(jnp.concatenate([ib, fb], axis=1), LANES)
    return jnp.concatenate([qa, ka, va, qb, kb, vb, gates, ob], axis=1).astype(BF16)


def _proj_even_kernel(x_ref, g_ref, sc_ref, sh_ref, w_ref, gb_ref, cos_ref, sp_ref, sm_ref,
                      qa_o, ka_o, va_o, qb_o, kb_o, vb_o, gt_o, og_o):
    h = _modulate(x_ref[...], g_ref[...], sc_ref[...], sh_ref[...]).astype(BF16)
    cos, sp, sm = cos_ref[...], sp_ref[...], sm_ref[...]
    half = ROT_A // 2

    def mm(lo, hi):
        return _dot(h, w_ref[:, lo:hi])

    for lo, hi in _col_tiles(E_QA, E_KA):
        qa_o[:, lo - E_QA:hi - E_QA] = _rope(mm(lo, hi), cos, sp, sm, half).astype(qa_o.dtype)
    ka_o[...] = _rope(mm(E_KA, E_VA), cos, sp, sm, half)
    va_o[...] = mm(E_VA, E_QB)
    for lo, hi in _col_tiles(E_QB, E_KB):
        qb_o[:, lo - E_QB:hi - E_QB] = (mm(lo, hi) * (DKB ** -0.5)).astype(qb_o.dtype)
    for lo, hi in _col_tiles(E_KB, E_VB):
        kb_o[:, lo - E_KB:hi - E_KB] = mm(lo, hi).astype(kb_o.dtype)
    for lo, hi in _col_tiles(E_VB, E_GT):
        vb_o[:, lo - E_VB:hi - E_VB] = mm(lo, hi).astype(vb_o.dtype)
    t = mm(E_GT, E_OB) + gb_ref[...]
    lane = lax.broadcasted_iota(I32, t.shape, 1)
    gt_o[...] = jnp.where(lane < HB, t, jax.nn.log_sigmoid(t))
    for lo, hi in _col_tiles(E_OB, E_END):
        og_o[:, lo - E_OB:hi - E_OB] = jax.nn.sigmoid(mm(lo, hi)).astype(og_o.dtype)


(O_QC, O_KC, O_VC, O_QI, O_KI, O_WI, O_QDN, O_QDP, O_CKV, O_KPE, O_END) = (
    0, 1024, 1280, 1536, 2560, 2688, 2816, 3840, 4352, 4864, 4992)


def _odd_weight(w):
    qc, kc, vc, qi, ki, wi, qd, ckv, kpe = jnp.split(
        w, np.cumsum((HC * DC, KVC * DC, KVC * DC, HI * DI, DI, HI, HD * (DNOPE + DROPE), KV_LORA)).tolist(), axis=1)
    d = w.shape[0]
    qd = qd.reshape(d, HD, DNOPE + DROPE)
    qdn = qd[:, :, :DNOPE].reshape(d, HD * DNOPE)
    qdp = qd[:, :, DNOPE:].reshape(d, HD * DROPE)
    return jnp.concatenate([qc, kc, vc, qi, _pad_cols(ki, LANES), _pad_cols(wi, LANES), qdn, qdp, ckv,
                            _pad_cols(kpe, LANES)], axis=1).astype(BF16)


def _proj_odd_kernel(x_ref, g_ref, sc_ref, sh_ref, w_ref, gckv_ref,
                     ci_ref, spi_ref, smi_ref, cc_ref, spc_ref, smc_ref, cd_ref, spd_ref, smd_ref,
                     qc_o, kc_o, vc_o, qi_o, ki_o, wi_o, qd_o, ckv_o, kpe_o,
                     kcb_o, vcb_o, kib_o, ckvb_o, kpeb_o):
    h = _modulate(x_ref[...], g_ref[...], sc_ref[...], sh_ref[...]).astype(BF16)
    tab_i = (ci_ref[...], spi_ref[...], smi_ref[...], ROT_I // 2)
    tab_c = (cc_ref[...], spc_ref[...], smc_ref[...], ROT_C // 2)
    tab_d = (cd_ref[...], spd_ref[...], smd_ref[...], DROPE // 2)

    def mm(lo, hi):
        return _dot(h, w_ref[:, lo:hi])

    for lo, hi in _col_tiles(O_QC, O_KC):
        qc_o[:, lo - O_QC:hi - O_QC] = (_rope(mm(lo, hi), *tab_c) * DSA_SCALE).astype(qc_o.dtype)
    kc = _rope(mm(O_KC, O_VC), *tab_c)
    kc_o[...] = kc
    kcb_o[...] = kc.astype(BF16)
    vc = mm(O_VC, O_QI)
    vc_o[...] = vc
    vcb_o[...] = vc.astype(BF16)
    for lo, hi in _col_tiles(O_QI, O_KI):
        qi_o[:, lo - O_QI:hi - O_QI] = _rope(mm(lo, hi), *tab_i).astype(qi_o.dtype)
    ki = _rope(mm(O_KI, O_WI), *tab_i)[:, :DI]
    ki_o[...] = ki
    kib_o[...] = ki.astype(BF16)
    wi_o[...] = mm(O_WI, O_QDN) * (HI ** -0.5 * DI ** -0.5)
    for lo, hi in _col_tiles(O_QDN, O_QDP):
        qn = (mm(lo, hi) * MLA_SCALE).astype(qd_o.dtype)
        for k in range((hi - lo) // DNOPE):
            hh = (lo - O_QDN) // DNOPE + k
            qd_o[:, hh * QD_SLOT:hh * QD_SLOT + DNOPE] = qn[:, k * DNOPE:(k + 1) * DNOPE]
    qp = _rope(mm(O_QDP, O_CKV), *tab_d) * MLA_SCALE
    low_lanes = lax.broadcasted_iota(I32, (qp.shape[0], LANES), 1) < DROPE
    for hh in range(HD):
        pair = qp[:, (hh // 2) * LANES:(hh // 2 + 1) * LANES]
        if hh % 2:
            pair = pltpu.roll(pair, DROPE, 1)
        qd_o[:, hh * QD_SLOT + DNOPE:(hh + 1) * QD_SLOT] = jnp.where(low_lanes, pair, 0.0).astype(qd_o.dtype)
    c = mm(O_CKV, O_KPE)
    c = c * lax.rsqrt(jnp.mean(c * c, axis=-1, keepdims=True) + EPS) * gckv_ref[...]
    ckv_o[...] = c
    ckvb_o[...] = c.astype(BF16)
    kpe = _rope(mm(O_KPE, O_END), *tab_d)[:, :DROPE]
    kpe_o[...] = kpe
    kpeb_o[...] = kpe.astype(BF16)


def _token_tile(t):
    return _pick(t, (512,))


def _proj_in(kernel_fn, name, x, modc, layer, g, w, extra, tables, tab_map, outs):
    t, d = x.shape
    tm = _token_tile(t)
    rows = tm // CHUNK
    in_specs = [pl.BlockSpec((tm, d), lambda i: (i, 0)),
                _resident((1, d), lambda i: (0, 0)),
                pl.BlockSpec((None, rows, d), lambda i: (layer, i, 1)),
                pl.BlockSpec((None, rows, d), lambda i: (layer, i, 0)),
                _resident(w.shape, lambda i: (0, 0)),
                _resident(extra.shape, lambda i: (0, 0))]
    in_specs += [pl.BlockSpec((tm, LANES), lambda i: (tab_map(i), 0)) for _ in tables]
    return pl.pallas_call(
        kernel_fn,
        grid=(t // tm,),
        in_specs=in_specs,
        out_specs=[pl.BlockSpec((tm, wd), lambda i: (i, 0)) for wd, _ in outs],
        out_shape=[jax.ShapeDtypeStruct((t, wd), dt) for wd, dt in outs],
        compiler_params=_cparams(("arbitrary",)),
        name=name,
    )(x, g, modc, modc, w, extra, *tables)


def _swa_kernel(tq, tiles_per_seq, sink_ref, q_ref, kw_ref, kc_ref, vw_ref, vc_ref, *rest):
    o_ref = rest[-1]
    kb = jnp.concatenate([kw_ref[...], kc_ref[...]], axis=0).astype(BF16)
    vb = jnp.concatenate([vw_ref[...], vc_ref[...]], axis=0).astype(BF16)
    qchunk = lax.broadcasted_iota(I32, (tq, WINDOW + tq), 0) // CHUNK
    kchunk = lax.broadcasted_iota(I32, (tq, WINDOW + tq), 1) // CHUNK
    valid = jnp.logical_and(kchunk >= qchunk, kchunk <= qchunk + WIN_CHUNKS)
    if tiles_per_seq:
        has_window = pl.program_id(0) % tiles_per_seq != 0
        valid = jnp.logical_and(valid, jnp.logical_or(kchunk >= WIN_CHUNKS, has_window))
    bias = jnp.where(valid, 0.0, NEG)
    group = HA // KVA
    for hq in range(HA):
        kv = hq // group
        q = q_ref[:, hq * DA:(hq + 1) * DA]
        s = _dot_nt(q, kb[:, kv * DA:(kv + 1) * DA]) * (DA ** -0.5) + bias
        sink = sink_ref[hq]
        m = jnp.maximum(jnp.max(s, axis=-1, keepdims=True), sink)
        p = jnp.exp(s - m)
        den = jnp.sum(p, axis=-1, keepdims=True) + jnp.exp(sink - m)
        o = _dot(p.astype(BF16), vb[:, kv * DA:(kv + 1) * DA]) / den
        o_ref[:, hq * DA:(hq + 1) * DA] = o.astype(o_ref.dtype)


def _swa(qa, ka, va, win_k, win_v, sinks, tp, s):
    t = qa.shape[0]
    tq = _pick(s, (2 * WINDOW, WINDOW))
    wpt = tq // WINDOW
    qspec = lambda rows, off: pl.BlockSpec((rows, HA * DA), lambda g: (g + off, 0))
    kvspec = lambda rows, off: pl.BlockSpec((rows, KVA * DA), lambda g: (g + off, 0))
    wspec_p = pl.BlockSpec((WINDOW, KVA * DA), lambda g: (jnp.maximum(g * wpt - 1, 0), 0))
    smem = pl.BlockSpec(memory_space=pltpu.SMEM)
    out_shape = jax.ShapeDtypeStruct((t, HA * DA), BF16)
    oa = pl.pallas_call(
        functools.partial(_swa_kernel, tq, s // tq),
        grid=(tp // tq,),
        in_specs=[smem, qspec(tq, 0), wspec_p, kvspec(tq, 0), wspec_p, kvspec(tq, 0)],
        out_specs=qspec(tq, 0),
        out_shape=out_shape,
        compiler_params=_cparams(("arbitrary",)),
        name="swa_prompt",
    )(sinks, qa, ka, ka, va, va)
    off = tp // CHUNK
    return pl.pallas_call(
        functools.partial(_swa_kernel, CHUNK, 0),
        grid=((t - tp) // CHUNK,),
        in_specs=[smem, qspec(CHUNK, off), kvspec(WINDOW, 0), kvspec(CHUNK, off), kvspec(WINDOW, 0),
                  kvspec(CHUNK, off), pl.BlockSpec(memory_space=pl.ANY)],
        out_specs=qspec(CHUNK, off),
        out_shape=out_shape,
        input_output_aliases={6: 0},
        compiler_params=_cparams(("arbitrary",)),
        name="swa_sample",
    )(sinks, qa, win_k, ka, win_v, va, oa)


def _mlstm_kernel(n_prompt_chunks, seq_chunks, q_ref, k_ref, v_ref, gt_ref, og_ref, gmh_ref,
                  c0_ref, n0_ref, m0_ref, h_o, c_o, n_o, m_o, c_s, n_s, m_s):
    g = pl.program_id(0)
    is_prompt = g < n_prompt_chunks
    c_idx = g % seq_chunks
    first = jnp.logical_or(jnp.logical_not(is_prompt), c_idx == 0)
    last = jnp.logical_or(jnp.logical_not(is_prompt), c_idx == seq_chunks - 1)

    @pl.when(first)
    def _():
        c_s[...] = c0_ref[...]
        n_s[...] = n0_ref[...]
        m_s[...] = m0_ref[...]

    gt = gt_ref[...]
    gt_t = gt.T
    row = lax.broadcasted_iota(I32, (CHUNK, CHUNK), 0)
    col = lax.broadcasted_iota(I32, (CHUNK, CHUNK), 1)
    causal = col <= row
    for hh in range(HB):
        ig_row = gt_t[hh:hh + 1, :]
        lf_row = gt_t[HB + hh:HB + hh + 1, :]
        ig_col = gt[:, hh:hh + 1]
        lf_col = gt[:, HB + hh:HB + hh + 1]
        b_col = jnp.sum(jnp.where(causal, lf_row, 0.0), axis=1, keepdims=True)
        b_row = jnp.sum(jnp.where(row <= col, lf_col, 0.0), axis=0, keepdims=True)
        m_prev = m_s[:, hh:hh + 1]
        d = jnp.where(causal, b_col - b_row + ig_row, NEG)
        inter = b_col + m_prev
        m_t = jnp.maximum(inter, jnp.max(d, axis=1, keepdims=True))
        w_intra = jnp.exp(d - m_t)
        w_inter = jnp.exp(inter - m_t)
        q = q_ref[:, hh * DKB:(hh + 1) * DKB]
        k = k_ref[:, hh * DKB:(hh + 1) * DKB]
        v = v_ref[:, hh * DVB:(hh + 1) * DVB]
        c_prev = c_s[hh]
        n_prev = n_s[hh:hh + 1, :]
        a = w_intra * _dot_nt(q, k)
        num = _dot(a.astype(BF16), v) + w_inter * _dot(q, c_prev.astype(BF16))
        den = (jnp.sum(a, axis=1, keepdims=True)
               + w_inter * jnp.sum(q.astype(F32) * n_prev, axis=1, keepdims=True))
        hv = num / jnp.maximum(jnp.abs(den), jnp.exp(-m_t))
        b_end = b_col[CHUNK - 1:CHUNK, :]
        g_row = b_end - b_row + ig_row
        g_col = b_end - b_col + ig_col
        m_new = jnp.maximum(b_end + m_prev, jnp.max(g_row, axis=1, keepdims=True))
        w_s = jnp.exp(g_col - m_new)
        w_c = jnp.exp(b_end + m_prev - m_new)
        c_s[hh] = w_c * c_prev + _dot_tn(k, (w_s * v.astype(F32)).astype(BF16))
        n_s[hh:hh + 1, :] = w_c * n_prev + jnp.sum(w_s * k.astype(F32), axis=0, keepdims=True)
        m_s[:, hh:hh + 1] = m_new
        y = hv * lax.rsqrt(jnp.mean(hv * hv, axis=-1, keepdims=True) + EPS)
        y = y * gmh_ref[:, hh * DVB:(hh + 1) * DVB] * og_ref[:, hh * DVB:(hh + 1) * DVB].astype(F32)
        h_o[:, hh * DVB:(hh + 1) * DVB] = y.astype(h_o.dtype)

    @pl.when(last)
    def _():
        c_o[...] = c_s[...]
        n_o[...] = n_s[...]
        m_o[...] = m_s[...]


def _mlstm(qb, kb, vb, gates, og, g_mh, c0, n0, m0, n_prompt_chunks, seq_chunks, n_prompt_seqs):
    t = qb.shape[0]
    nch = t // CHUNK
    ns = c0.shape[0]

    def seq(g):
        return jnp.where(g < n_prompt_chunks, g // seq_chunks, n_prompt_seqs + g - n_prompt_chunks)

    tok = lambda w: pl.BlockSpec((CHUNK, w), lambda g: (g, 0))
    st_specs = [pl.BlockSpec((None, HB, DKB, DVB), lambda g: (seq(g), 0, 0, 0)),
                pl.BlockSpec((None, HB, DKB), lambda g: (seq(g), 0, 0)),
                pl.BlockSpec((None, 1, HB), lambda g: (seq(g), 0, 0))]
    return pl.pallas_call(
        functools.partial(_mlstm_kernel, n_prompt_chunks, seq_chunks),
        grid=(nch,),
        in_specs=[tok(HB * DKB), tok(HB * DKB), tok(HB * DVB), tok(LANES), tok(HB * DVB),
                  _resident((1, HB * DVB), lambda g: (0, 0))] + st_specs,
        out_specs=[tok(HB * DVB)] + st_specs,
        out_shape=[jax.ShapeDtypeStruct((t, HB * DVB), BF16),
                   jax.ShapeDtypeStruct((ns, HB, DKB, DVB), F32),
                   jax.ShapeDtypeStruct((ns, HB, DKB), F32),
                   jax.ShapeDtypeStruct((ns, 1, HB), F32)],
        scratch_shapes=[pltpu.VMEM((HB, DKB, DVB), F32), pltpu.VMEM((HB, DKB), F32),
                        pltpu.VMEM((1, HB), F32)],
        compiler_params=_cparams(("arbitrary",)),
        name="mlstm",
    )(qb, kb, vb, gates, og, g_mh, c0, n0, m0)


INT_MIN = -2 ** 31


def _order_key(x):
    b = pltpu.bitcast(x, I32)
    return jnp.where(b >= 0, b, b ^ jnp.int32(0x7FFFFFFF))


def _dsa_kernel(tq, kblk, l_valid, l_pad, topk, causal,
                qc_ref, qi_ref, wi_ref, k_ref, v_ref, ki_ref, *rest):
    o_ref, key_s = rest[-2:]
    i = pl.program_id(1)
    q0 = i * tq
    rowq = lax.broadcasted_iota(I32, (tq, 1), 0)
    if causal:
        lim = ((q0 + rowq) // CHUNK + 1) * CHUNK
        nkb = (q0 + tq + kblk - 1) // kblk
    else:
        lim = jnp.full((tq, 1), l_valid, I32)
        nkb = l_pad // kblk
    lane = lax.broadcasted_iota(I32, (tq, kblk), 1)

    wi = wi_ref[...]

    def score_block(kb, carry):
        start = pl.multiple_of(kb * kblk, kblk)
        kib = ki_ref[pl.ds(start, kblk), :]
        acc = jnp.zeros((tq, kblk), F32)
        for hh in range(HI):
            s = _dot_nt(qi_ref[:, hh * DI:(hh + 1) * DI], kib)
            acc = acc + wi[:, hh:hh + 1] * jnp.maximum(s, 0.0)
        acc = acc + 0.0
        acc = jnp.where(start + lane < lim, acc, NEG)
        key_s[kb] = _order_key(acc)
        return carry

    lax.fori_loop(0, nkb, score_block, 0)

    lane1 = lax.broadcasted_iota(I32, (tq, LANES), 1)

    def count(pred_fn):
        def blk(kb, part):
            start = kb * kblk
            for c in range(kblk // LANES):
                keys = key_s[kb, :, c * LANES:(c + 1) * LANES]
                part = part + pred_fn(keys, start + c * LANES + lane1).astype(I32)
            return part
        part = lax.fori_loop(0, nkb, blk, jnp.zeros((tq, LANES), I32))
        return jnp.sum(part, axis=1, keepdims=True)

    def thr_bit(carry):
        b, thr, n_ge, _ = carry
        cand = thr + lax.shift_left(jnp.int32(1), 31 - b)
        cand_b = jnp.broadcast_to(cand, (tq, LANES))
        cnt = count(lambda keys, idx: keys >= cand_b)
        keep = cnt >= topk
        n_ge = jnp.where(keep, cnt, n_ge)
        return b + 1, jnp.where(keep, cand, thr), n_ge, jnp.max(n_ge)

    _, thr, n_ge, _ = lax.while_loop(
        lambda c: jnp.logical_and(c[0] < 32, c[3] > topk), thr_bit,
        (jnp.int32(0), jnp.full((tq, 1), INT_MIN, I32), jnp.full((tq, 1), l_pad, I32), jnp.int32(l_pad)))

    nbits = max(1, (l_pad - 1).bit_length())

    def tie_cut():
        thr_b = jnp.broadcast_to(thr, (tq, LANES))
        need = topk - count(lambda keys, idx: keys > thr_b)

        def cut_bit(b, cut):
            cand = cut + lax.shift_left(jnp.int32(1), nbits - 1 - b)
            cand_b = jnp.broadcast_to(cand, (tq, LANES))
            cnt = count(lambda keys, idx: jnp.logical_and(keys == thr_b, idx < cand_b))
            return jnp.where(cnt < need, cand, cut)

        return lax.fori_loop(0, nbits, cut_bit, jnp.zeros((tq, 1), I32))

    cut = lax.cond(jnp.max(n_ge) > topk, tie_cut, lambda: jnp.full((tq, 1), l_pad, I32))

    group = HC // KVC
    qs = [jnp.concatenate([qc_ref[:, (kv * group + j) * DC:(kv * group + j + 1) * DC]
                           for j in range(group)], axis=0) for kv in range(KVC)]

    def attend_block(kb, carry):
        start = pl.multiple_of(kb * kblk, kblk)
        keys = key_s[kb]
        idx = start + lane
        sel = jnp.logical_or(keys > thr, jnp.logical_and(keys == thr, idx <= cut))
        sel = jnp.logical_and(sel, idx < lim)
        bias = jnp.where(sel, 0.0, NEG)
        new = []
        for kv in range(KVC):
            m, l, acc = carry[kv]
            kk = k_ref[pl.ds(start, kblk), kv * DC:(kv + 1) * DC]
            vv = v_ref[pl.ds(start, kblk), kv * DC:(kv + 1) * DC]
            s = _dot_nt(qs[kv], kk)
            s = jnp.concatenate([s[j * tq:(j + 1) * tq] + bias for j in range(group)], axis=0)
            m_new = jnp.maximum(m, jnp.max(s, axis=1, keepdims=True))
            alpha = jnp.exp2(m - m_new)
            p = jnp.exp2(s - m_new)
            l = alpha * l + jnp.sum(p, axis=1, keepdims=True)
            acc = alpha * acc + _dot(p.astype(BF16), vv)
            new.append((m_new, l, acc))
        return tuple(new)

    init = tuple((jnp.full((group * tq, 1), NEG, F32), jnp.zeros((group * tq, 1), F32),
                  jnp.zeros((group * tq, DC), F32)) for _ in range(KVC))
    res = lax.fori_loop(0, nkb, attend_block, init)
    for kv in range(KVC):
        _, l, acc = res[kv]
        o = acc / l
        for j in range(group):
            hq = kv * group + j
            o_ref[:, hq * DC:(hq + 1) * DC] = o[j * tq:(j + 1) * tq, :].astype(o_ref.dtype)


def _dsa(qc, qi, wi, k, v, ki, prev_out, nseq, sq, l_valid, l_pad, causal, tq, q_off, name):
    topk = min(INDEX_TOPK, l_valid // 4)
    kblk = _pick(l_pad, (512, 256, 128))
    assert kblk >= topk and sq % tq == 0 and q_off % tq == 0
    nq = sq // tq
    qmap = lambda b, i: (q_off // tq + b * nq + i, 0)
    kmap = lambda b, i: (b, 0)
    in_specs = [pl.BlockSpec((tq, HC * DC), qmap), pl.BlockSpec((tq, HI * DI), qmap),
                pl.BlockSpec((tq, LANES), qmap),
                _resident((l_pad, KVC * DC), kmap), _resident((l_pad, KVC * DC), kmap),
                _resident((l_pad, DI), kmap)]
    args = [qc, qi, wi, k, v, ki]
    aliases = {}
    if prev_out is not None:
        in_specs.append(pl.BlockSpec(memory_space=pl.ANY))
        aliases = {len(args): 0}
        args.append(prev_out)
    return pl.pallas_call(
        functools.partial(_dsa_kernel, tq, kblk, l_valid, l_pad, topk, causal),
        grid=(nseq, nq),
        in_specs=in_specs,
        out_specs=pl.BlockSpec((tq, HC * DC), qmap),
        out_shape=jax.ShapeDtypeStruct((qc.shape[0], HC * DC), BF16),
        scratch_shapes=[pltpu.VMEM((l_pad // kblk, tq, kblk), I32)],
        input_output_aliases=aliases,
        compiler_params=_cparams(("arbitrary", "arbitrary")),
        name=name,
    )(*args)


def _matmul_kernel(a_ref, w_ref, o_ref):
    o_ref[...] = _dot(a_ref[...], w_ref[...]).astype(o_ref.dtype)


def _latent_up(ckv, w_ukv):
    r = ckv.shape[0]
    tm = _pick(r, (1024, 512, 256, 128, 64))
    n = w_ukv.shape[1]
    return pl.pallas_call(
        _matmul_kernel,
        grid=(r // tm,),
        in_specs=[pl.BlockSpec((tm, KV_LORA), lambda i: (i, 0)), _resident(w_ukv.shape, lambda i: (0, 0))],
        out_specs=pl.BlockSpec((tm, n), lambda i: (i, 0)),
        out_shape=jax.ShapeDtypeStruct((r, n), BF16),
        compiler_params=_cparams(("arbitrary",)),
        name="latent_up",
    )(ckv, w_ukv)


def _mla_last_block(i, tq, kblk):
    return (i * tq + tq - 1) // kblk


def _latent_up_t_kernel(ckv_ref, kpe_ref, wk_ref, wvt_ref, k_o, vt_o):
    ckv = ckv_ref[...]
    kn = _dot(ckv, wk_ref[...]).astype(k_o.dtype)
    kpe = jnp.concatenate([kpe_ref[...], jnp.zeros((kpe_ref.shape[0], LANES - DROPE), k_o.dtype)], axis=1)
    for hh in range(HD):
        k_o[:, hh * QD_SLOT:hh * QD_SLOT + DNOPE] = kn[:, hh * DNOPE:(hh + 1) * DNOPE]
        k_o[:, hh * QD_SLOT + DNOPE:(hh + 1) * QD_SLOT] = kpe
    vt_o[...] = _dot_nt(wvt_ref[...], ckv).astype(vt_o.dtype)


def _latent_up_t(ckv, kpe, w_uk, w_uv_t, rows):
    tm = _pick(rows, (512, 256, 128))
    return pl.pallas_call(
        _latent_up_t_kernel,
        grid=(rows // tm,),
        in_specs=[pl.BlockSpec((tm, KV_LORA), lambda i: (i, 0)), pl.BlockSpec((tm, DROPE), lambda i: (i, 0)),
                  _resident(w_uk.shape, lambda i: (0, 0)), _resident(w_uv_t.shape, lambda i: (0, 0))],
        out_specs=[pl.BlockSpec((tm, HD * QD_SLOT), lambda i: (i, 0)),
                   pl.BlockSpec((HD * DVD, tm), lambda i: (0, i))],
        out_shape=[jax.ShapeDtypeStruct((rows, HD * QD_SLOT), BF16),
                   jax.ShapeDtypeStruct((HD * DVD, rows), BF16)],
        compiler_params=_cparams(("arbitrary",)),
        name="latent_up_t",
    )(ckv, kpe, w_uk, w_uv_t)


def _mla_t_kernel(tq, kblk, it_ref, jt_ref, q_ref, k_ref, vt_ref, o_ref, m_s, l_s, acc_s):
    step_id = pl.program_id(1)
    i = it_ref[step_id]
    j = jt_ref[step_id]
    last = _mla_last_block(i, tq, kblk)

    @pl.when(j == 0)
    def _():
        m_s[...] = jnp.full(m_s.shape, NEG, F32)
        l_s[...] = jnp.zeros(l_s.shape, F32)
        acc_s[...] = jnp.zeros(acc_s.shape, F32)

    def step(masked):
        if masked:
            kidx = j * kblk + lax.broadcasted_iota(I32, (kblk, tq), 0)
            qpos = i * tq + lax.broadcasted_iota(I32, (kblk, tq), 1)
            bias = jnp.where(kidx < (qpos // CHUNK + 1) * CHUNK, 0.0, NEG)
        for hh in range(HD):
            s = _dot_nt(k_ref[:, hh * QD_SLOT:(hh + 1) * QD_SLOT], q_ref[:, hh * QD_SLOT:(hh + 1) * QD_SLOT])
            if masked:
                s = s + bias
            m = m_s[hh]
            m_new = jnp.maximum(m, jnp.max(s, axis=0, keepdims=True))
            alpha = jnp.exp2(m - m_new)
            p = jnp.exp2(s - m_new)
            l_s[hh] = alpha * l_s[hh] + jnp.sum(p, axis=0, keepdims=True)
            acc_s[hh] = alpha * acc_s[hh] + _dot(vt_ref[hh * DVD:(hh + 1) * DVD, :], p.astype(BF16))
            m_s[hh] = m_new

    n_full = (i * tq + CHUNK) // kblk
    pl.when(j < n_full)(functools.partial(step, False))
    pl.when(j >= n_full)(functools.partial(step, True))

    @pl.when(j == last)
    def _():
        for hh in range(HD):
            o_ref[:, hh * DVD:(hh + 1) * DVD] = (acc_s[hh] / l_s[hh]).T.astype(o_ref.dtype)


def _mla_t(qd, k_cat, v_t, nseq, sq, tq):
    kblk = _pick(sq, (512, 256, 128))
    nq, nk = sq // tq, sq // kblk
    pairs = [(i, j) for i in range(nq) for j in range(_mla_last_block(i, tq, kblk) + 1)]
    it = jnp.asarray([p[0] for p in pairs], I32)
    jt = jnp.asarray([p[1] for p in pairs], I32)
    qmap = lambda b, t, it, jt: (b * nq + it[t], 0)
    return pl.pallas_call(
        functools.partial(_mla_t_kernel, tq, kblk),
        grid_spec=pltpu.PrefetchScalarGridSpec(
            num_scalar_prefetch=2,
            grid=(nseq, len(pairs)),
            in_specs=[pl.BlockSpec((tq, HD * QD_SLOT), qmap),
                      pl.BlockSpec((kblk, HD * QD_SLOT), lambda b, t, it, jt: (b * nk + jt[t], 0)),
                      pl.BlockSpec((HD * DVD, kblk), lambda b, t, it, jt: (0, b * nk + jt[t]))],
            out_specs=pl.BlockSpec((tq, HD * DVD), qmap),
            scratch_shapes=[pltpu.VMEM((HD, 1, tq), F32), pltpu.VMEM((HD, 1, tq), F32),
                            pltpu.VMEM((HD, DVD, tq), F32)]),
        out_shape=jax.ShapeDtypeStruct((qd.shape[0], HD * DVD), BF16),
        compiler_params=_cparams(("arbitrary", "arbitrary")),
        name="mla_prompt",
    )(it, jt, qd, k_cat, v_t)


def _mla_kernel(tq, kblk, l_valid, causal, it_ref, jt_ref, q_ref, kv_ref, kpe_ref, *rest):
    o_ref, m_s, l_s, acc_s = rest[-4:]
    step_id = pl.program_id(1)
    i = it_ref[step_id]
    j = jt_ref[step_id]
    last = _mla_last_block(i, tq, kblk) if causal else pl.cdiv(l_valid, kblk) - 1

    @pl.when(j == 0)
    def _():
        m_s[...] = jnp.full(m_s.shape, NEG, F32)
        l_s[...] = jnp.zeros(l_s.shape, F32)
        acc_s[...] = jnp.zeros(acc_s.shape, F32)

    def step(masked):
        if masked:
            idx = j * kblk + lax.broadcasted_iota(I32, (tq, kblk), 1)
            if causal:
                rowq = i * tq + lax.broadcasted_iota(I32, (tq, kblk), 0)
                valid = idx < (rowq // CHUNK + 1) * CHUNK
            else:
                valid = idx < l_valid
            bias = jnp.where(valid, 0.0, NEG)
        kpe = kpe_ref[...]
        for hh in range(HD):
            kn = kv_ref[:, hh * DNOPE:(hh + 1) * DNOPE]
            vv = kv_ref[:, HD * DNOPE + hh * DVD:HD * DNOPE + (hh + 1) * DVD]
            s = (_dot_nt(q_ref[:, hh * QD_SLOT:hh * QD_SLOT + DNOPE], kn)
                 + _dot_nt(q_ref[:, hh * QD_SLOT + DNOPE:hh * QD_SLOT + DNOPE + DROPE], kpe))
            if masked:
                s = s + bias
            m = m_s[hh]
            m_new = jnp.maximum(m, jnp.max(s, axis=1, keepdims=True))
            alpha = jnp.exp2(m - m_new)
            p = jnp.exp2(s - m_new)
            l_s[hh] = alpha * l_s[hh] + jnp.sum(p, axis=1, keepdims=True)
            acc_s[hh] = alpha * acc_s[hh] + _dot(p.astype(BF16), vv)
            m_s[hh] = m_new

    if causal:
        n_full = (i * tq + CHUNK) // kblk
    else:
        n_full = l_valid // kblk
    pl.when(j < n_full)(functools.partial(step, False))
    pl.when(j >= n_full)(functools.partial(step, True))

    @pl.when(j == last)
    def _():
        for hh in range(HD):
            o_ref[:, hh * DVD:(hh + 1) * DVD] = (acc_s[hh] / l_s[hh]).astype(o_ref.dtype)


def _mla(qd, kv_up, kpe, prev_out, nseq, sq, l_valid, l_pad, causal, tq, q_off, kv_off, name):
    kblk = _pick(l_pad, (512, 256, 128))
    assert q_off % tq == 0 and kv_off % kblk == 0
    nq, nk = sq // tq, l_pad // kblk
    n_blocks = (lambda i: (i * tq + tq - 1) // kblk + 1) if causal else (lambda i: -(-l_valid // kblk))
    pairs = [(i, j) for i in range(nq) for j in range(n_blocks(i))]
    it = jnp.asarray([p[0] for p in pairs], I32)
    jt = jnp.asarray([p[1] for p in pairs], I32)
    qmap = lambda b, t, it, jt: (q_off // tq + b * nq + it[t], 0)
    kvmap = lambda b, t, it, jt: (kv_off // kblk + b * nk + jt[t], 0)
    kpemap = lambda b, t, it, jt: (b * nk + jt[t], 0)
    in_specs = [pl.BlockSpec((tq, HD * QD_SLOT), qmap),
                pl.BlockSpec((kblk, HD * (DNOPE + DVD)), kvmap), pl.BlockSpec((kblk, DROPE), kpemap)]
    args = [it, jt, qd, kv_up, kpe]
    aliases = {}
    if prev_out is not None:
        in_specs.append(pl.BlockSpec(memory_space=pl.ANY))
        aliases = {len(args): 0}
        args.append(prev_out)
    return pl.pallas_call(
        functools.partial(_mla_kernel, tq, kblk, l_valid, causal),
        grid_spec=pltpu.PrefetchScalarGridSpec(
            num_scalar_prefetch=2,
            grid=(nseq, len(pairs)),
            in_specs=in_specs,
            out_specs=pl.BlockSpec((tq, HD * DVD), qmap),
            scratch_shapes=[pltpu.VMEM((HD, tq, 1), F32), pltpu.VMEM((HD, tq, 1), F32),
                            pltpu.VMEM((HD, tq, DVD), F32)]),
        out_shape=jax.ShapeDtypeStruct((qd.shape[0], HD * DVD), BF16),
        input_output_aliases=aliases,
        compiler_params=_cparams(("arbitrary", "arbitrary")),
        name=name,
    )(*args)


def _proj_out_kernel(a1_ref, a2_ref, w1_ref, w2_ref, x_ref, gate_ref, o_ref):
    y = _dot(a1_ref[...], w1_ref[...]) + _dot(a2_ref[...], w2_ref[...])
    o_ref[...] = x_ref[...] + _gate_rows(y, gate_ref[...])


def _proj_out(a1, a2, w_out, x, modc, layer):
    t, d = x.shape
    tm = _token_tile(t)
    rows = tm // CHUNK
    k1, k2 = a1.shape[1], a2.shape[1]
    assert k1 == k2
    w = _to_bf16(w_out)
    return pl.pallas_call(
        _proj_out_kernel,
        grid=(t // tm,),
        in_specs=[pl.BlockSpec((tm, k1), lambda i: (i, 0)), pl.BlockSpec((tm, k2), lambda i: (i, 0)),
                  _resident((k1, d), lambda i: (0, 0)), _resident((k2, d), lambda i: (1, 0)),
                  pl.BlockSpec((tm, d), lambda i: (i, 0)),
                  pl.BlockSpec((None, rows, d), lambda i: (layer, i, 2))],
        out_specs=pl.BlockSpec((tm, d), lambda i: (i, 0)),
        out_shape=jax.ShapeDtypeStruct((t, d), F32),
        compiler_params=_cparams(("arbitrary",)),
        name="proj_out",
    )(a1, a2, w, w, x, modc)


def _ffn_kernel(final_norm, x_ref, g_ref, sc_ref, sh_ref, gate_ref, wg_ref, wu_ref, wd_ref, gf_ref,
                o_ref, h_s):
    f = pl.program_id(1)

    @pl.when(f == 0)
    def _():
        h_s[...] = _modulate(x_ref[...], g_ref[...], sc_ref[...], sh_ref[...]).astype(BF16)
        o_ref[...] = jnp.zeros(o_ref.shape, F32)

    h = h_s[...]
    a = _dot(h, wg_ref[...])
    u = _dot(h, wu_ref[...])
    o_ref[...] += _dot((a * jax.nn.sigmoid(a) * u).astype(BF16), wd_ref[...])

    @pl.when(f == pl.num_programs(1) - 1)
    def _():
        y = x_ref[...] + _gate_rows(o_ref[...], gate_ref[...])
        if final_norm:
            y = y * lax.rsqrt(jnp.mean(y * y, axis=-1, keepdims=True) + EPS) * gf_ref[...]
        o_ref[...] = y


def _ffn(x, modc, layer, g, wg, wu, wd, g_final, final_norm):
    t, d = x.shape
    f = wg.shape[1]
    tm = _pick(t, (1024, 512))
    tf = _pick(f, (512, 256, 128))
    rows = tm // CHUNK
    modspec = lambda comp: pl.BlockSpec((None, rows, d), lambda i, j: (layer, i, comp))
    return pl.pallas_call(
        functools.partial(_ffn_kernel, final_norm),
        grid=(t // tm, f // tf),
        in_specs=[_resident((tm, d), lambda i, j: (i, 0)),
                  _resident((1, d), lambda i, j: (0, 0)),
                  modspec(4), modspec(3), modspec(5),
                  pl.BlockSpec((d, tf), lambda i, j: (0, j)),
                  pl.BlockSpec((d, tf), lambda i, j: (0, j)),
                  pl.BlockSpec((tf, d), lambda i, j: (j, 0)),
                  _resident((1, d), lambda i, j: (0, 0))],
        out_specs=pl.BlockSpec((tm, d), lambda i, j: (i, 0)),
        out_shape=jax.ShapeDtypeStruct((t, d), F32),
        scratch_shapes=[pltpu.VMEM((tm, d), BF16)],
        compiler_params=_cparams(("arbitrary", "arbitrary")),
        name="ffn",
    )(x, g, modc, modc, modc, _to_bf16(wg), _to_bf16(wu), _to_bf16(wd), g_final)


def _round_up(n, m):
    return (n + m - 1) // m * m


def kernel(x_prompt, x_sample, c_prompt, c_sample, cache_a_k, cache_a_v, state_b_c, state_b_n, state_b_m,
           cache_c_k, cache_c_v, cache_c_idx, cache_d_ckv, cache_d_kpe, w_ada, b_ada, g_norm_mix, g_norm_ffn,
           w_in_even, w_out_even, sinks_a, b_igate, b_fgate, g_mlstm, w_in_odd, w_out_odd, g_ckv, w_uk, w_uv,
           w_ffn_gate, w_ffn_up, w_ffn_down, g_final):
    nb, s, d = x_prompt.shape
    db, ds, _ = x_sample.shape
    assert ds == CHUNK and s % CHUNK == 0
    depth = w_ada.shape[0]
    past = cache_c_k.shape[2]
    tp, ts = nb * s, db * ds
    t = tp + ts
    tm = _token_tile(t)
    assert s % tm == 0 and tp % tm == 0
    seq_chunks = s // CHUNK
    npc = tp // CHUNK

    x = jnp.concatenate([x_prompt.reshape(tp, d), x_sample.reshape(ts, d)], axis=0)
    mod = _ada(jnp.concatenate([c_prompt, c_sample], axis=0), w_ada, b_ada)
    mod_p = jnp.broadcast_to(mod[:, :nb, None, :], (depth, nb, seq_chunks, 6 * d)).reshape(depth, npc, 6 * d)
    modc = jnp.concatenate([mod_p, mod[:, nb:]], axis=1)

    pos = jnp.concatenate([jnp.arange(s, dtype=I32), jnp.tile(past + jnp.arange(ds, dtype=I32), tm // ds)])
    tab_a = _rope_tables(pos, ROT_A, DA)
    tab_c = _rope_tables(pos, ROT_C, DC)
    tab_d = _rope_tables(pos, DROPE, DROPE)
    prompt_tiles, seq_tiles = tp // tm, s // tm
    tab_map = lambda i: jnp.where(i < prompt_tiles, i % seq_tiles, seq_tiles)

    w_ukv_all = jnp.concatenate([w_uk, w_uv], axis=2).astype(BF16)
    l_s = past + ds
    lp_s = _round_up(l_s, 512)

    def cat_past(cache, new, width):
        full = jnp.concatenate([cache.reshape(db, past, width), new.reshape(db, ds, width)], axis=1)
        full = jnp.pad(full, ((0, 0), (0, lp_s - l_s), (0, 0)))
        return full.astype(BF16).reshape(db * lp_s, width)

    even_p, even_s, odd_p, odd_s = [], [], [], []
    for l in range(depth):
        i = l // 2
        g_mix = g_norm_mix[l].reshape(1, d)
        if l % 2 == 0:
            gate_bias = _pad_cols(jnp.concatenate([b_igate[i], b_fgate[i]]).reshape(1, 2 * HB), LANES)
            qa, ka, va, qb, kb, vb, gates, og = _proj_in(
                _proj_even_kernel, "proj_even", x, modc, l, g_mix, _even_weight(w_in_even[i]), gate_bias,
                list(tab_a), tab_map,
                [(HA * DA, BF16), (KVA * DA, F32), (KVA * DA, F32), (HB * DKB, BF16), (HB * DKB, BF16),
                 (HB * DVB, BF16), (LANES, F32), (HB * DVB, BF16)])
            ka_s = ka[tp:].reshape(db, ds, KVA * DA)
            va_s = va[tp:].reshape(db, ds, KVA * DA)
            win_k = cache_a_k[i].reshape(db, WINDOW, KVA * DA)
            win_v = cache_a_v[i].reshape(db, WINDOW, KVA * DA)
            kband = jnp.concatenate([win_k, ka_s], axis=1)
            vband = jnp.concatenate([win_v, va_s], axis=1)
            oa = _swa(qa, ka, va, win_k.reshape(db * WINDOW, KVA * DA), win_v.reshape(db * WINDOW, KVA * DA),
                      sinks_a[i], tp, s)
            c0 = jnp.concatenate([jnp.zeros((nb, HB, DKB, DVB), F32), state_b_c[i]], axis=0)
            n0 = jnp.concatenate([jnp.zeros((nb, HB, DKB), F32), state_b_n[i]], axis=0)
            m0 = jnp.concatenate([jnp.zeros((nb, HB), F32), state_b_m[i]], axis=0).reshape(nb + db, 1, HB)
            hb, c_new, n_new, m_new = _mlstm(qb, kb, vb, gates, og, g_mlstm[i].reshape(1, HB * DVB),
                                             c0, n0, m0, npc, seq_chunks, nb)
            m_new = m_new.reshape(nb + db, HB)
            ka_p = ka[:tp].reshape(nb, s, KVA, DA)
            va_p = va[:tp].reshape(nb, s, KVA, DA)
            even_p.append((ka_p[:, -WINDOW:], va_p[:, -WINDOW:], c_new[:nb], n_new[:nb], m_new[:nb]))
            even_s.append((kband[:, -WINDOW:].reshape(db, WINDOW, KVA, DA),
                           vband[:, -WINDOW:].reshape(db, WINDOW, KVA, DA), c_new[nb:], n_new[nb:], m_new[nb:]))
            x = _proj_out(oa, hb, w_out_even[i], x, modc, l)
        else:
            qc, kc, vc, qi, ki, wi, qd, ckv, kpe, kcb, vcb, kib, ckvb, kpeb = _proj_in(
                _proj_odd_kernel, "proj_odd", x, modc, l, g_mix, _odd_weight(w_in_odd[i]),
                g_ckv[i].reshape(1, KV_LORA), list(tab_a) + list(tab_c) + list(tab_d), tab_map,
                [(HC * DC, BF16), (KVC * DC, F32), (KVC * DC, F32), (HI * DI, BF16), (DI, F32), (LANES, F32),
                 (HD * QD_SLOT, BF16), (KV_LORA, F32), (DROPE, F32),
                 (KVC * DC, BF16), (KVC * DC, BF16), (DI, BF16), (KV_LORA, BF16), (DROPE, BF16)])
            oc = _dsa(qc, qi, wi, kcb, vcb, kib, None, nb, s, s, s, True, 128, 0, "dsa_prompt")
            oc = _dsa(qc, qi, wi, cat_past(cache_c_k[i], kcb[tp:], KVC * DC),
                      cat_past(cache_c_v[i], vcb[tp:], KVC * DC), cat_past(cache_c_idx[i], kib[tp:], DI),
                      oc, db, ds, l_s, lp_s, False, ds, tp, "dsa_sample")
            k_cat, v_t = _latent_up_t(ckvb, kpeb, w_uk[i].astype(BF16), w_uv[i].T.astype(BF16), tp)
            od = _mla_t(qd, k_cat, v_t, nb, s, _pick(s, (512, 256, 128)))
            kv_up = _latent_up(cat_past(cache_d_ckv[i], ckvb[tp:], KV_LORA), w_ukv_all[i])
            od = _mla(qd, kv_up, cat_past(cache_d_kpe[i], kpeb[tp:], DROPE), od,
                      db, ds, l_s, lp_s, False, ds, tp, 0, "mla_sample")
            odd_p.append((kc[:tp].reshape(nb, s, KVC, DC), vc[:tp].reshape(nb, s, KVC, DC),
                          ki[:tp].reshape(nb, s, DI), ckv[:tp].reshape(nb, s, KV_LORA),
                          kpe[:tp].reshape(nb, s, DROPE)))
            odd_s.append((kc[tp:].reshape(db, ds, KVC, DC), vc[tp:].reshape(db, ds, KVC, DC),
                          ki[tp:].reshape(db, ds, DI), ckv[tp:].reshape(db, ds, KV_LORA),
                          kpe[tp:].reshape(db, ds, DROPE)))
            x = _proj_out(oc, od, w_out_odd[i], x, modc, l)
        x = _ffn(x, modc, l, g_norm_ffn[l].reshape(1, d), w_ffn_gate[l], w_ffn_up[l], w_ffn_down[l],
                 g_final.reshape(1, d), l == depth - 1)

    y_prompt = x[:tp].reshape(nb, s, d)
    y_sample = x[tp:].reshape(db, ds, d)
    st_p = [jnp.stack([e[j] for e in even_p]) for j in range(5)] + [jnp.stack([o[j] for o in odd_p]) for j in range(5)]
    st_s = [jnp.stack([e[j] for e in even_s]) for j in range(5)] + [jnp.stack([o[j] for o in odd_s]) for j in range(5)]
    return (y_prompt, y_sample, *st_p, *st_s)
```

```python
import functools
import math

import jax
import jax.numpy as jnp
import numpy as np
from jax import lax
from jax.experimental import pallas as pl
from jax.experimental.pallas import tpu as pltpu

F32 = jnp.float32
BF16 = jnp.bfloat16
I32 = jnp.int32

CHUNK = 64
ROPE_THETA = 500000.0
EPS = 1e-6
NEG = -1e30
LANES = 128

HA, KVA, DA = 16, 2, 64
ROT_A = DA // 4
WINDOW = 128
WIN_CHUNKS = WINDOW // CHUNK
HB, DKB, DVB = 4, 128, 256
HC, KVC, DC = 8, 2, 128
ROT_C = DC // 4
HI, DI = 16, 64
ROT_I = DI // 4
INDEX_TOPK = 256
HD, DNOPE, DROPE, DVD, KV_LORA = 8, 128, 64, 128, 512
LOG2E = math.log2(math.e)
DSA_SCALE = DC ** -0.5 * LOG2E
MLA_SCALE = (DNOPE + DROPE) ** -0.5 * LOG2E
QD_SLOT = 2 * LANES

VMEM_LIMIT = 56 * 1024 * 1024
PROJ_ODD_VMEM_LIMIT = 60 * 1024 * 1024


def _cparams(sem, vmem=VMEM_LIMIT):
    return pltpu.CompilerParams(dimension_semantics=sem, vmem_limit_bytes=vmem)


def _pick(n, options):
    for o in options:
        if n % o == 0:
            return o
    raise ValueError(f"no tile in {options} divides {n}")


def _resident(shape, index_map):
    return pl.BlockSpec(shape, index_map, pipeline_mode=pl.Buffered(1))


def _dot(a, b):
    return jnp.dot(a, b, preferred_element_type=F32)


def _dot_nt(a, b):
    return lax.dot_general(a, b, (((1,), (1,)), ((), ())), preferred_element_type=F32)


def _dot_tn(a, b):
    return lax.dot_general(a, b, (((0,), (0,)), ((), ())), preferred_element_type=F32)


def _modulate(x, g, sc, sh):
    tm, d = x.shape
    y = x * lax.rsqrt(jnp.mean(x * x, axis=-1, keepdims=True) + EPS) * g
    y = y.reshape(tm // CHUNK, CHUNK, d) * (1.0 + sc[:, None, :]) + sh[:, None, :]
    return y.reshape(tm, d)


def _gate_rows(y, g):
    tm, d = y.shape
    return (y.reshape(tm // CHUNK, CHUNK, d) * g[:, None, :]).reshape(tm, d)


def _rope(x, cos, sp, sm, half):
    blocks = []
    for j in range(x.shape[1] // LANES):
        xb = x[:, j * LANES:(j + 1) * LANES]
        blocks.append(xb * cos + pltpu.roll(xb, half, 1) * sp + pltpu.roll(xb, LANES - half, 1) * sm)
    return blocks[0] if len(blocks) == 1 else jnp.concatenate(blocks, axis=1)


def _rope_tables(pos, rot, head):
    half = rot // 2
    inv = ROPE_THETA ** (-jnp.arange(half, dtype=F32) * 2.0 / rot)
    ang = pos.astype(F32)[:, None] * inv[None, :]
    cos, sin = jnp.cos(ang), jnp.sin(ang)
    p = pos.shape[0]
    one = jnp.ones((p, head - rot), F32)
    zero = jnp.zeros((p, head - rot), F32)
    zh = jnp.zeros((p, half), F32)
    c = jnp.concatenate([cos, cos, one], axis=1)
    s_plus = jnp.concatenate([zh, sin, zero], axis=1)
    s_minus = jnp.concatenate([-sin, zh, zero], axis=1)
    rep = LANES // head
    return tuple(jnp.tile(t, (1, rep)) for t in (c, s_plus, s_minus))


CAST_BLOCK_BYTES = 6 * 1024 * 1024


def _cast_kernel(w_ref, o_ref):
    o_ref[...] = w_ref[...].astype(o_ref.dtype)


def _to_bf16(w):
    r, c = w.shape
    tr = next(t for t in (2048, 1024, 512, 256, 128, 64, 32, 16) if r % t == 0 and t * c * 4 <= CAST_BLOCK_BYTES)
    return pl.pallas_call(
        _cast_kernel,
        grid=(r // tr,),
        in_specs=[pl.BlockSpec((tr, c), lambda i: (i, 0))],
        out_specs=pl.BlockSpec((tr, c), lambda i: (i, 0)),
        out_shape=jax.ShapeDtypeStruct((r, c), BF16),
        compiler_params=_cparams(("arbitrary",)),
        name="to_bf16",
    )(w)


def _ada_kernel(c_ref, w_ref, b_ref, o_ref):
    c = c_ref[...]
    h = (c * jax.nn.sigmoid(c)).astype(BF16)
    o_ref[...] = _dot(h, w_ref[...].astype(BF16)) + b_ref[...]


def _ada(c, w_ada, b_ada):
    depth, d, n = w_ada.shape
    ns = c.shape[0]
    tn = _pick(n, (1024, 512, 256, 128))
    return pl.pallas_call(
        _ada_kernel,
        grid=(depth, n // tn),
        in_specs=[pl.BlockSpec((ns, d), lambda l, j: (0, 0)),
                  pl.BlockSpec((None, d, tn), lambda l, j: (l, 0, j)),
                  pl.BlockSpec((None, 1, tn), lambda l, j: (l, 0, j))],
        out_specs=pl.BlockSpec((None, ns, tn), lambda l, j: (l, 0, j)),
        out_shape=jax.ShapeDtypeStruct((depth, ns, n), F32),
        compiler_params=_cparams(("arbitrary", "arbitrary")),
        name="ada",
    )(c, w_ada, b_ada.reshape(depth, 1, n))


def _col_tiles(lo, hi, step=512):
    return [(a, min(a + step, hi)) for a in range(lo, hi, step)]


E_QA, E_KA, E_VA, E_QB, E_KB, E_VB, E_GT, E_OB, E_END = 0, 1024, 1152, 1280, 1792, 2304, 3328, 3456, 4480


def _pad_cols(w, width):
    return jnp.pad(w, ((0, 0), (0, width - w.shape[1])))


def _even_weight(w):
    qa, ka, va, qb, kb, vb, ib, fb, ob = jnp.split(
        w, np.cumsum((HA * DA, KVA * DA, KVA * DA, HB * DKB, HB * DKB, HB * DVB, HB, HB))[:].tolist(), axis=1)
    gates = _pad_cols(jnp.concatenate([ib, fb], axis=1), LANES)
    return jnp.concatenate([qa, ka, va, qb, kb, vb, gates, ob], axis=1).astype(BF16)


def _proj_even_kernel(x_ref, g_ref, sc_ref, sh_ref, w_ref, gb_ref, cos_ref, sp_ref, sm_ref,
                      qa_o, ka_o, va_o, qb_o, kb_o, vb_o, gt_o, og_o):
    h = _modulate(x_ref[...], g_ref[...], sc_ref[...], sh_ref[...]).astype(BF16)
    cos, sp, sm = cos_ref[...], sp_ref[...], sm_ref[...]
    half = ROT_A // 2

    def mm(lo, hi):
        return _dot(h, w_ref[:, lo:hi])

    for lo, hi in _col_tiles(E_QA, E_KA):
        qa_o[:, lo - E_QA:hi - E_QA] = _rope(mm(lo, hi), cos, sp, sm, half).astype(qa_o.dtype)
    ka_o[...] = _rope(mm(E_KA, E_VA), cos, sp, sm, half)
    va_o[...] = mm(E_VA, E_QB)
    for lo, hi in _col_tiles(E_QB, E_KB):
        qb_o[:, lo - E_QB:hi - E_QB] = (mm(lo, hi) * (DKB ** -0.5)).astype(qb_o.dtype)
    for lo, hi in _col_tiles(E_KB, E_VB):
        kb_o[:, lo - E_KB:hi - E_KB] = mm(lo, hi).astype(kb_o.dtype)
    for lo, hi in _col_tiles(E_VB, E_GT):
        vb_o[:, lo - E_VB:hi - E_VB] = mm(lo, hi).astype(vb_o.dtype)
    t = mm(E_GT, E_OB) + gb_ref[...]
    lane = lax.broadcasted_iota(I32, t.shape, 1)
    gt_o[...] = jnp.where(lane < HB, t, jax.nn.log_sigmoid(t))
    for lo, hi in _col_tiles(E_OB, E_END):
        og_o[:, lo - E_OB:hi - E_OB] = jax.nn.sigmoid(mm(lo, hi)).astype(og_o.dtype)


(O_QC, O_KC, O_VC, O_QI, O_KI, O_WI, O_QDN, O_QDP, O_CKV, O_KPE, O_END) = (
    0, 1024, 1280, 1536, 2560, 2688, 2816, 3840, 4352, 4864, 4992)


def _odd_weight(w):
    qc, kc, vc, qi, ki, wi, qd, ckv, kpe = jnp.split(
        w, np.cumsum((HC * DC, KVC * DC, KVC * DC, HI * DI, DI, HI, HD * (DNOPE + DROPE), KV_LORA)).tolist(), axis=1)
    d = w.shape[0]
    qd = qd.reshape(d, HD, DNOPE + DROPE)
    qdn = qd[:, :, :DNOPE].reshape(d, HD * DNOPE)
    qdp = qd[:, :, DNOPE:].reshape(d, HD * DROPE)
    return jnp.concatenate([qc, kc, vc, qi, _pad_cols(ki, LANES), _pad_cols(wi, LANES), qdn, qdp, ckv,
                            _pad_cols(kpe, LANES)], axis=1).astype(BF16)


def _store_split(prompt_tiles, prompt_ref, sample_ref, val):
    i = pl.program_id(0)

    @pl.when(i < prompt_tiles)
    def _():
        prompt_ref[...] = val

    @pl.when(i >= prompt_tiles)
    def _():
        sample_ref[...] = val


def _proj_odd_kernel(prompt_tiles, x_ref, g_ref, sc_ref, sh_ref, w_ref, gckv_ref,
                     ci_ref, spi_ref, smi_ref, cc_ref, spc_ref, smc_ref, cd_ref, spd_ref, smd_ref,
                     qc_o, kc_p, kc_s, vc_p, vc_s, qi_o, ki_p, ki_s, wi_o, qd_o, ckv_p, ckv_s, kpe_p, kpe_s,
                     kcb_o, vcb_o, kib_o, ckvb_o, kpeb_o):
    h = _modulate(x_ref[...], g_ref[...], sc_ref[...], sh_ref[...]).astype(BF16)
    tab_i = (ci_ref[...], spi_ref[...], smi_ref[...], ROT_I // 2)
    tab_c = (cc_ref[...], spc_ref[...], smc_ref[...], ROT_C // 2)
    tab_d = (cd_ref[...], spd_ref[...], smd_ref[...], DROPE // 2)

    def mm(lo, hi):
        return _dot(h, w_ref[:, lo:hi])

    for lo, hi in _col_tiles(O_QC, O_KC):
        qc_o[:, lo - O_QC:hi - O_QC] = (_rope(mm(lo, hi), *tab_c) * DSA_SCALE).astype(qc_o.dtype)
    kc = _rope(mm(O_KC, O_VC), *tab_c)
    _store_split(prompt_tiles, kc_p, kc_s, kc)
    kcb_o[...] = kc.astype(BF16)
    vc = mm(O_VC, O_QI)
    _store_split(prompt_tiles, vc_p, vc_s, vc)
    vcb_o[...] = vc.astype(BF16)
    for lo, hi in _col_tiles(O_QI, O_KI):
        qi_o[:, lo - O_QI:hi - O_QI] = _rope(mm(lo, hi), *tab_i).astype(qi_o.dtype)
    ki = _rope(mm(O_KI, O_WI), *tab_i)[:, :DI]
    _store_split(prompt_tiles, ki_p, ki_s, ki)
    kib_o[...] = ki.astype(BF16)
    wi_o[...] = mm(O_WI, O_QDN) * (HI ** -0.5 * DI ** -0.5)
    for lo, hi in _col_tiles(O_QDN, O_QDP):
        qn = (mm(lo, hi) * MLA_SCALE).astype(qd_o.dtype)
        for k in range((hi - lo) // DNOPE):
            hh = (lo - O_QDN) // DNOPE + k
            qd_o[:, hh * QD_SLOT:hh * QD_SLOT + DNOPE] = qn[:, k * DNOPE:(k + 1) * DNOPE]
    qp = _rope(mm(O_QDP, O_CKV), *tab_d) * MLA_SCALE
    low_lanes = lax.broadcasted_iota(I32, (qp.shape[0], LANES), 1) < DROPE
    for hh in range(HD):
        pair = qp[:, (hh // 2) * LANES:(hh // 2 + 1) * LANES]
        if hh % 2:
            pair = pltpu.roll(pair, DROPE, 1)
        qd_o[:, hh * QD_SLOT + DNOPE:(hh + 1) * QD_SLOT] = jnp.where(low_lanes, pair, 0.0).astype(qd_o.dtype)
    c = mm(O_CKV, O_KPE)
    c = c * lax.rsqrt(jnp.mean(c * c, axis=-1, keepdims=True) + EPS) * gckv_ref[...]
    _store_split(prompt_tiles, ckv_p, ckv_s, c)
    ckvb_o[...] = c.astype(BF16)
    kpe = _rope(mm(O_KPE, O_END), *tab_d)[:, :DROPE]
    _store_split(prompt_tiles, kpe_p, kpe_s, kpe)
    kpeb_o[...] = kpe.astype(BF16)


def _token_tile(t):
    return _pick(t, (512,))


def _proj_in(kernel_fn, name, x, modc, layer, g, w, extra, tables, tab_map, outs, prompt_rows=0,
             vmem=VMEM_LIMIT):
    t, d = x.shape
    tm = _token_tile(t)
    rows = tm // CHUNK
    ptiles = prompt_rows // tm
    out_specs, out_shape = [], []
    for wd, dt, split in outs:
        if split:
            out_specs += [pl.BlockSpec((tm, wd), lambda i: (jnp.minimum(i, ptiles - 1), 0)),
                          pl.BlockSpec((tm, wd), lambda i: (jnp.maximum(i - ptiles, 0), 0))]
            out_shape += [jax.ShapeDtypeStruct((prompt_rows, wd), dt),
                          jax.ShapeDtypeStruct((t - prompt_rows, wd), dt)]
        else:
            out_specs.append(pl.BlockSpec((tm, wd), lambda i: (i, 0)))
            out_shape.append(jax.ShapeDtypeStruct((t, wd), dt))
    in_specs = [pl.BlockSpec((tm, d), lambda i: (i, 0)),
                _resident((1, d), lambda i: (0, 0)),
                pl.BlockSpec((None, rows, d), lambda i: (layer, i, 1)),
                pl.BlockSpec((None, rows, d), lambda i: (layer, i, 0)),
                _resident(w.shape, lambda i: (0, 0)),
                _resident(extra.shape, lambda i: (0, 0))]
    in_specs += [pl.BlockSpec((tm, LANES), lambda i: (tab_map(i), 0)) for _ in tables]
    return pl.pallas_call(
        kernel_fn,
        grid=(t // tm,),
        in_specs=in_specs,
        out_specs=out_specs,
        out_shape=out_shape,
        compiler_params=_cparams(("arbitrary",), vmem),
        name=name,
    )(x, g, modc, modc, w, extra, *tables)


def _swa_kernel(tq, tiles_per_seq, sink_ref, q_ref, kw_ref, kc_ref, vw_ref, vc_ref, *rest):
    o_ref = rest[-1]
    kb = jnp.concatenate([kw_ref[...], kc_ref[...]], axis=0).astype(BF16)
    vb = jnp.concatenate([vw_ref[...], vc_ref[...]], axis=0).astype(BF16)
    qchunk = lax.broadcasted_iota(I32, (tq, WINDOW + tq), 0) // CHUNK
    kchunk = lax.broadcasted_iota(I32, (tq, WINDOW + tq), 1) // CHUNK
    valid = jnp.logical_and(kchunk >= qchunk, kchunk <= qchunk + WIN_CHUNKS)
    if tiles_per_seq:
        has_window = pl.program_id(0) % tiles_per_seq != 0
        valid = jnp.logical_and(valid, jnp.logical_or(kchunk >= WIN_CHUNKS, has_window))
    bias = jnp.where(valid, 0.0, NEG)
    group = HA // KVA
    for hq in range(HA):
        kv = hq // group
        q = q_ref[:, hq * DA:(hq + 1) * DA]
        s = _dot_nt(q, kb[:, kv * DA:(kv + 1) * DA]) * (DA ** -0.5) + bias
        sink = sink_ref[hq]
        m = jnp.maximum(jnp.max(s, axis=-1, keepdims=True), sink)
        p = jnp.exp(s - m)
        den = jnp.sum(p, axis=-1, keepdims=True) + jnp.exp(sink - m)
        o = _dot(p.astype(BF16), vb[:, kv * DA:(kv + 1) * DA]) / den
        o_ref[:, hq * DA:(hq + 1) * DA] = o.astype(o_ref.dtype)


def _swa(qa, ka, va, win_k, win_v, sinks, tp, s):
    t = qa.shape[0]
    tq = _pick(s, (2 * WINDOW, WINDOW))
    wpt = tq // WINDOW
    qspec = lambda rows, off: pl.BlockSpec((rows, HA * DA), lambda g: (g + off, 0))
    kvspec = lambda rows, off: pl.BlockSpec((rows, KVA * DA), lambda g: (g + off, 0))
    wspec_p = pl.BlockSpec((WINDOW, KVA * DA), lambda g: (jnp.maximum(g * wpt - 1, 0), 0))
    smem = pl.BlockSpec(memory_space=pltpu.SMEM)
    out_shape = jax.ShapeDtypeStruct((t, HA * DA), BF16)
    oa = pl.pallas_call(
        functools.partial(_swa_kernel, tq, s // tq),
        grid=(tp // tq,),
        in_specs=[smem, qspec(tq, 0), wspec_p, kvspec(tq, 0), wspec_p, kvspec(tq, 0)],
        out_specs=qspec(tq, 0),
        out_shape=out_shape,
        compiler_params=_cparams(("arbitrary",)),
        name="swa_prompt",
    )(sinks, qa, ka, ka, va, va)
    off = tp // CHUNK
    return pl.pallas_call(
        functools.partial(_swa_kernel, CHUNK, 0),
        grid=((t - tp) // CHUNK,),
        in_specs=[smem, qspec(CHUNK, off), kvspec(WINDOW, 0), kvspec(CHUNK, off), kvspec(WINDOW, 0),
                  kvspec(CHUNK, off), pl.BlockSpec(memory_space=pl.ANY)],
        out_specs=qspec(CHUNK, off),
        out_shape=out_shape,
        input_output_aliases={6: 0},
        compiler_params=_cparams(("arbitrary",)),
        name="swa_sample",
    )(sinks, qa, win_k, ka, win_v, va, oa)


def _mlstm_kernel(n_prompt_chunks, seq_chunks, q_ref, k_ref, v_ref, gt_ref, og_ref, gmh_ref,
                  c0_ref, n0_ref, m0_ref, h_o, c_o, n_o, m_o, c_s, n_s, m_s):
    g = pl.program_id(0)
    is_prompt = g < n_prompt_chunks
    c_idx = g % seq_chunks
    first = jnp.logical_or(jnp.logical_not(is_prompt), c_idx == 0)
    last = jnp.logical_or(jnp.logical_not(is_prompt), c_idx == seq_chunks - 1)

    @pl.when(first)
    def _():
        c_s[...] = c0_ref[...]
        n_s[...] = n0_ref[...]
        m_s[...] = m0_ref[...]

    gt = gt_ref[...]
    gt_t = gt.T
    row = lax.broadcasted_iota(I32, (CHUNK, CHUNK), 0)
    col = lax.broadcasted_iota(I32, (CHUNK, CHUNK), 1)
    causal = col <= row
    for hh in range(HB):
        ig_row = gt_t[hh:hh + 1, :]
        lf_row = gt_t[HB + hh:HB + hh + 1, :]
        ig_col = gt[:, hh:hh + 1]
        lf_col = gt[:, HB + hh:HB + hh + 1]
        b_col = jnp.sum(jnp.where(causal, lf_row, 0.0), axis=1, keepdims=True)
        b_row = jnp.sum(jnp.where(row <= col, lf_col, 0.0), axis=0, keepdims=True)
        m_prev = m_s[:, hh:hh + 1]
        d = jnp.where(causal, b_col - b_row + ig_row, NEG)
        inter = b_col + m_prev
        m_t = jnp.maximum(inter, jnp.max(d, axis=1, keepdims=True))
        w_intra = jnp.exp(d - m_t)
        w_inter = jnp.exp(inter - m_t)
        q = q_ref[:, hh * DKB:(hh + 1) * DKB]
        k = k_ref[:, hh * DKB:(hh + 1) * DKB]
        v = v_ref[:, hh * DVB:(hh + 1) * DVB]
        c_prev = c_s[hh]
        n_prev = n_s[hh:hh + 1, :]
        a = w_intra * _dot_nt(q, k)
        num = _dot(a.astype(BF16), v) + w_inter * _dot(q, c_prev.astype(BF16))
        den = (jnp.sum(a, axis=1, keepdims=True)
               + w_inter * jnp.sum(q.astype(F32) * n_prev, axis=1, keepdims=True))
        hv = num / jnp.maximum(jnp.abs(den), jnp.exp(-m_t))
        b_end = b_col[CHUNK - 1:CHUNK, :]
        g_row = b_end - b_row + ig_row
        g_col = b_end - b_col + ig_col
        m_new = jnp.maximum(b_end + m_prev, jnp.max(g_row, axis=1, keepdims=True))
        w_s = jnp.exp(g_col - m_new)
        w_c = jnp.exp(b_end + m_prev - m_new)
        c_s[hh] = w_c * c_prev + _dot_tn(k, (w_s * v.astype(F32)).astype(BF16))
        n_s[hh:hh + 1, :] = w_c * n_prev + jnp.sum(w_s * k.astype(F32), axis=0, keepdims=True)
        m_s[:, hh:hh + 1] = m_new
        y = hv * lax.rsqrt(jnp.mean(hv * hv, axis=-1, keepdims=True) + EPS)
        y = y * gmh_ref[:, hh * DVB:(hh + 1) * DVB] * og_ref[:, hh * DVB:(hh + 1) * DVB].astype(F32)
        h_o[:, hh * DVB:(hh + 1) * DVB] = y.astype(h_o.dtype)

    @pl.when(last)
    def _():
        c_o[...] = c_s[...]
        n_o[...] = n_s[...]
        m_o[...] = m_s[...]


def _mlstm(qb, kb, vb, gates, og, g_mh, c0, n0, m0, n_prompt_chunks, seq_chunks, n_prompt_seqs):
    t = qb.shape[0]
    nch = t // CHUNK
    ns = c0.shape[0]

    def seq(g):
        return jnp.where(g < n_prompt_chunks, g // seq_chunks, n_prompt_seqs + g - n_prompt_chunks)

    tok = lambda w: pl.BlockSpec((CHUNK, w), lambda g: (g, 0))
    st_specs = [pl.BlockSpec((None, HB, DKB, DVB), lambda g: (seq(g), 0, 0, 0)),
                pl.BlockSpec((None, HB, DKB), lambda g: (seq(g), 0, 0)),
                pl.BlockSpec((None, 1, HB), lambda g: (seq(g), 0, 0))]
    return pl.pallas_call(
        functools.partial(_mlstm_kernel, n_prompt_chunks, seq_chunks),
        grid=(nch,),
        in_specs=[tok(HB * DKB), tok(HB * DKB), tok(HB * DVB), tok(LANES), tok(HB * DVB),
                  _resident((1, HB * DVB), lambda g: (0, 0))] + st_specs,
        out_specs=[tok(HB * DVB)] + st_specs,
        out_shape=[jax.ShapeDtypeStruct((t, HB * DVB), BF16),
                   jax.ShapeDtypeStruct((ns, HB, DKB, DVB), F32),
                   jax.ShapeDtypeStruct((ns, HB, DKB), F32),
                   jax.ShapeDtypeStruct((ns, 1, HB), F32)],
        scratch_shapes=[pltpu.VMEM((HB, DKB, DVB), F32), pltpu.VMEM((HB, DKB), F32),
                        pltpu.VMEM((1, HB), F32)],
        compiler_params=_cparams(("arbitrary",)),
        name="mlstm",
    )(qb, kb, vb, gates, og, g_mh, c0, n0, m0)


INT_MIN = -2 ** 31


def _order_key(x):
    b = pltpu.bitcast(x, I32)
    return jnp.where(b >= 0, b, b ^ jnp.int32(0x7FFFFFFF))


def _dsa_kernel(tq, kblk, l_valid, l_pad, topk, causal,
                qc_ref, qi_ref, wi_ref, k_ref, v_ref, ki_ref, *rest):
    o_ref, key_s = rest[-2:]
    i = pl.program_id(1)
    q0 = i * tq
    rowq = lax.broadcasted_iota(I32, (tq, 1), 0)
    if causal:
        lim = ((q0 + rowq) // CHUNK + 1) * CHUNK
        nkb = (q0 + tq + kblk - 1) // kblk
    else:
        lim = jnp.full((tq, 1), l_valid, I32)
        nkb = l_pad // kblk
    lane = lax.broadcasted_iota(I32, (tq, kblk), 1)

    wi = wi_ref[...]

    def score_block(kb, carry):
        start = pl.multiple_of(kb * kblk, kblk)
        kib = ki_ref[pl.ds(start, kblk), :]
        acc = jnp.zeros((tq, kblk), F32)
        for hh in range(HI):
            s = _dot_nt(qi_ref[:, hh * DI:(hh + 1) * DI], kib)
            acc = acc + wi[:, hh:hh + 1] * jnp.maximum(s, 0.0)
        acc = acc + 0.0
        acc = jnp.where(start + lane < lim, acc, NEG)
        key_s[kb] = _order_key(acc)
        return carry

    lax.fori_loop(0, nkb, score_block, 0)

    lane1 = lax.broadcasted_iota(I32, (tq, LANES), 1)

    def count(pred_fn):
        def blk(kb, part):
            start = kb * kblk
            for c in range(kblk // LANES):
                keys = key_s[kb, :, c * LANES:(c + 1) * LANES]
                part = part + pred_fn(keys, start + c * LANES + lane1).astype(I32)
            return part
        part = lax.fori_loop(0, nkb, blk, jnp.zeros((tq, LANES), I32))
        return jnp.sum(part, axis=1, keepdims=True)

    def thr_bit(carry):
        b, thr, n_ge, _ = carry
        cand = thr + lax.shift_left(jnp.int32(1), 31 - b)
        cand_b = jnp.broadcast_to(cand, (tq, LANES))
        cnt = count(lambda keys, idx: keys >= cand_b)
        keep = cnt >= topk
        n_ge = jnp.where(keep, cnt, n_ge)
        return b + 1, jnp.where(keep, cand, thr), n_ge, jnp.max(n_ge)

    _, thr, n_ge, _ = lax.while_loop(
        lambda c: jnp.logical_and(c[0] < 32, c[3] > topk), thr_bit,
        (jnp.int32(0), jnp.full((tq, 1), INT_MIN, I32), jnp.full((tq, 1), l_pad, I32), jnp.int32(l_pad)))

    nbits = max(1, (l_pad - 1).bit_length())

    def tie_cut():
        thr_b = jnp.broadcast_to(thr, (tq, LANES))
        need = topk - count(lambda keys, idx: keys > thr_b)

        def cut_bit(b, cut):
            cand = cut + lax.shift_left(jnp.int32(1), nbits - 1 - b)
            cand_b = jnp.broadcast_to(cand, (tq, LANES))
            cnt = count(lambda keys, idx: jnp.logical_and(keys == thr_b, idx < cand_b))
            return jnp.where(cnt < need, cand, cut)

        return lax.fori_loop(0, nbits, cut_bit, jnp.zeros((tq, 1), I32))

    cut = lax.cond(jnp.max(n_ge) > topk, tie_cut, lambda: jnp.full((tq, 1), l_pad, I32))

    group = HC // KVC
    qs = [jnp.concatenate([qc_ref[:, (kv * group + j) * DC:(kv * group + j + 1) * DC]
                           for j in range(group)], axis=0) for kv in range(KVC)]

    def attend_block(kb, carry):
        start = pl.multiple_of(kb * kblk, kblk)
        keys = key_s[kb]
        idx = start + lane
        sel = jnp.logical_or(keys > thr, jnp.logical_and(keys == thr, idx <= cut))
        sel = jnp.logical_and(sel, idx < lim)
        bias = jnp.where(sel, 0.0, NEG)
        new = []
        for kv in range(KVC):
            m, l, acc = carry[kv]
            kk = k_ref[pl.ds(start, kblk), kv * DC:(kv + 1) * DC]
            vv = v_ref[pl.ds(start, kblk), kv * DC:(kv + 1) * DC]
            s = _dot_nt(qs[kv], kk)
            s = jnp.concatenate([s[j * tq:(j + 1) * tq] + bias for j in range(group)], axis=0)
            m_new = jnp.maximum(m, jnp.max(s, axis=1, keepdims=True))
            alpha = jnp.exp2(m - m_new)
            p = jnp.exp2(s - m_new)
            l = alpha * l + jnp.sum(p, axis=1, keepdims=True)
            acc = alpha * acc + _dot(p.astype(BF16), vv)
            new.append((m_new, l, acc))
        return tuple(new)

    init = tuple((jnp.full((group * tq, 1), NEG, F32), jnp.zeros((group * tq, 1), F32),
                  jnp.zeros((group * tq, DC), F32)) for _ in range(KVC))
    res = lax.fori_loop(0, nkb, attend_block, init)
    for kv in range(KVC):
        _, l, acc = res[kv]
        o = acc / l
        for j in range(group):
            hq = kv * group + j
            o_ref[:, hq * DC:(hq + 1) * DC] = o[j * tq:(j + 1) * tq, :].astype(o_ref.dtype)


def _dsa(qc, qi, wi, k, v, ki, prev_out, nseq, sq, l_valid, l_pad, causal, tq, q_off, name):
    topk = min(INDEX_TOPK, l_valid // 4)
    kblk = _pick(l_pad, (512, 256, 128))
    assert kblk >= topk and sq % tq == 0 and q_off % tq == 0
    nq = sq // tq
    qmap = lambda b, i: (q_off // tq + b * nq + i, 0)
    kmap = lambda b, i: (b, 0)
    in_specs = [pl.BlockSpec((tq, HC * DC), qmap), pl.BlockSpec((tq, HI * DI), qmap),
                pl.BlockSpec((tq, LANES), qmap),
                _resident((l_pad, KVC * DC), kmap), _resident((l_pad, KVC * DC), kmap),
                _resident((l_pad, DI), kmap)]
    args = [qc, qi, wi, k, v, ki]
    aliases = {}
    if prev_out is not None:
        in_specs.append(pl.BlockSpec(memory_space=pl.ANY))
        aliases = {len(args): 0}
        args.append(prev_out)
    return pl.pallas_call(
        functools.partial(_dsa_kernel, tq, kblk, l_valid, l_pad, topk, causal),
        grid=(nseq, nq),
        in_specs=in_specs,
        out_specs=pl.BlockSpec((tq, HC * DC), qmap),
        out_shape=jax.ShapeDtypeStruct((qc.shape[0], HC * DC), BF16),
        scratch_shapes=[pltpu.VMEM((l_pad // kblk, tq, kblk), I32)],
        input_output_aliases=aliases,
        compiler_params=_cparams(("arbitrary", "arbitrary")),
        name=name,
    )(*args)


def _matmul_kernel(a_ref, w_ref, o_ref):
    o_ref[...] = _dot(a_ref[...], w_ref[...]).astype(o_ref.dtype)


def _latent_up(ckv, w_ukv):
    r = ckv.shape[0]
    tm = _pick(r, (1024, 512, 256, 128, 64))
    n = w_ukv.shape[1]
    return pl.pallas_call(
        _matmul_kernel,
        grid=(r // tm,),
        in_specs=[pl.BlockSpec((tm, KV_LORA), lambda i: (i, 0)), _resident(w_ukv.shape, lambda i: (0, 0))],
        out_specs=pl.BlockSpec((tm, n), lambda i: (i, 0)),
        out_shape=jax.ShapeDtypeStruct((r, n), BF16),
        compiler_params=_cparams(("arbitrary",)),
        name="latent_up",
    )(ckv, w_ukv)


def _mla_last_block(i, tq, kblk):
    return (i * tq + tq - 1) // kblk


def _latent_up_t_kernel(ckv_ref, kpe_ref, wk_ref, wvt_ref, k_o, vt_o):
    ckv = ckv_ref[...]
    kn = _dot(ckv, wk_ref[...]).astype(k_o.dtype)
    kpe = jnp.concatenate([kpe_ref[...], jnp.zeros((kpe_ref.shape[0], LANES - DROPE), k_o.dtype)], axis=1)
    for hh in range(HD):
        k_o[:, hh * QD_SLOT:hh * QD_SLOT + DNOPE] = kn[:, hh * DNOPE:(hh + 1) * DNOPE]
        k_o[:, hh * QD_SLOT + DNOPE:(hh + 1) * QD_SLOT] = kpe
    vt_o[...] = _dot_nt(wvt_ref[...], ckv).astype(vt_o.dtype)


def _latent_up_t(ckv, kpe, w_uk, w_uv_t, rows):
    tm = _pick(rows, (512, 256, 128))
    return pl.pallas_call(
        _latent_up_t_kernel,
        grid=(rows // tm,),
        in_specs=[pl.BlockSpec((tm, KV_LORA), lambda i: (i, 0)), pl.BlockSpec((tm, DROPE), lambda i: (i, 0)),
                  _resident(w_uk.shape, lambda i: (0, 0)), _resident(w_uv_t.shape, lambda i: (0, 0))],
        out_specs=[pl.BlockSpec((tm, HD * QD_SLOT), lambda i: (i, 0)),
                   pl.BlockSpec((HD * DVD, tm), lambda i: (0, i))],
        out_shape=[jax.ShapeDtypeStruct((rows, HD * QD_SLOT), BF16),
                   jax.ShapeDtypeStruct((HD * DVD, rows), BF16)],
        compiler_params=_cparams(("arbitrary",)),
        name="latent_up_t",
    )(ckv, kpe, w_uk, w_uv_t)


def _mla_t_kernel(tq, kblk, it_ref, jt_ref, q_ref, k_ref, vt_ref, o_ref, m_s, l_s, acc_s):
    step_id = pl.program_id(1)
    i = it_ref[step_id]
    j = jt_ref[step_id]
    last = _mla_last_block(i, tq, kblk)

    @pl.when(j == 0)
    def _():
        m_s[...] = jnp.full(m_s.shape, NEG, F32)
        l_s[...] = jnp.zeros(l_s.shape, F32)
        acc_s[...] = jnp.zeros(acc_s.shape, F32)

    def step(masked):
        if masked:
            kidx = j * kblk + lax.broadcasted_iota(I32, (kblk, tq), 0)
            qpos = i * tq + lax.broadcasted_iota(I32, (kblk, tq), 1)
            bias = jnp.where(kidx < (qpos // CHUNK + 1) * CHUNK, 0.0, NEG)
        for hh in range(HD):
            s = _dot_nt(k_ref[:, hh * QD_SLOT:(hh + 1) * QD_SLOT], q_ref[:, hh * QD_SLOT:(hh + 1) * QD_SLOT])
            if masked:
                s = s + bias
            m = m_s[hh]
            m_new = jnp.maximum(m, jnp.max(s, axis=0, keepdims=True))
            alpha = jnp.exp2(m - m_new)
            p = jnp.exp2(s - m_new)
            l_s[hh] = alpha * l_s[hh] + jnp.sum(p, axis=0, keepdims=True)
            acc_s[hh] = alpha * acc_s[hh] + _dot(vt_ref[hh * DVD:(hh + 1) * DVD, :], p.astype(BF16))
            m_s[hh] = m_new

    n_full = (i * tq + CHUNK) // kblk
    pl.when(j < n_full)(functools.partial(step, False))
    pl.when(j >= n_full)(functools.partial(step, True))

    @pl.when(j == last)
    def _():
        for hh in range(HD):
            o_ref[:, hh * DVD:(hh + 1) * DVD] = (acc_s[hh] / l_s[hh]).T.astype(o_ref.dtype)


def _mla_t(qd, k_cat, v_t, nseq, sq, tq):
    kblk = _pick(sq, (512, 256, 128))
    nq, nk = sq // tq, sq // kblk
    pairs = [(i, j) for i in range(nq) for j in range(_mla_last_block(i, tq, kblk) + 1)]
    it = jnp.asarray([p[0] for p in pairs], I32)
    jt = jnp.asarray([p[1] for p in pairs], I32)
    qmap = lambda b, t, it, jt: (b * nq + it[t], 0)
    return pl.pallas_call(
        functools.partial(_mla_t_kernel, tq, kblk),
        grid_spec=pltpu.PrefetchScalarGridSpec(
            num_scalar_prefetch=2,
            grid=(nseq, len(pairs)),
            in_specs=[pl.BlockSpec((tq, HD * QD_SLOT), qmap),
                      pl.BlockSpec((kblk, HD * QD_SLOT), lambda b, t, it, jt: (b * nk + jt[t], 0)),
                      pl.BlockSpec((HD * DVD, kblk), lambda b, t, it, jt: (0, b * nk + jt[t]))],
            out_specs=pl.BlockSpec((tq, HD * DVD), qmap),
            scratch_shapes=[pltpu.VMEM((HD, 1, tq), F32), pltpu.VMEM((HD, 1, tq), F32),
                            pltpu.VMEM((HD, DVD, tq), F32)]),
        out_shape=jax.ShapeDtypeStruct((qd.shape[0], HD * DVD), BF16),
        compiler_params=_cparams(("arbitrary", "arbitrary")),
        name="mla_prompt",
    )(it, jt, qd, k_cat, v_t)


def _mla_kernel(tq, kblk, l_valid, causal, it_ref, jt_ref, q_ref, kv_ref, kpe_ref, *rest):
    o_ref, m_s, l_s, acc_s = rest[-4:]
    step_id = pl.program_id(1)
    i = it_ref[step_id]
    j = jt_ref[step_id]
    last = _mla_last_block(i, tq, kblk) if causal else pl.cdiv(l_valid, kblk) - 1

    @pl.when(j == 0)
    def _():
        m_s[...] = jnp.full(m_s.shape, NEG, F32)
        l_s[...] = jnp.zeros(l_s.shape, F32)
        acc_s[...] = jnp.zeros(acc_s.shape, F32)

    def step(masked):
        if masked:
            idx = j * kblk + lax.broadcasted_iota(I32, (tq, kblk), 1)
            if causal:
                rowq = i * tq + lax.broadcasted_iota(I32, (tq, kblk), 0)
                valid = idx < (rowq // CHUNK + 1) * CHUNK
            else:
                valid = idx < l_valid
            bias = jnp.where(valid, 0.0, NEG)
        kpe = kpe_ref[...]
        for hh in range(HD):
            kn = kv_ref[:, hh * DNOPE:(hh + 1) * DNOPE]
            vv = kv_ref[:, HD * DNOPE + hh * DVD:HD * DNOPE + (hh + 1) * DVD]
            s = (_dot_nt(q_ref[:, hh * QD_SLOT:hh * QD_SLOT + DNOPE], kn)
                 + _dot_nt(q_ref[:, hh * QD_SLOT + DNOPE:hh * QD_SLOT + DNOPE + DROPE], kpe))
            if masked:
                s = s + bias
            m = m_s[hh]
            m_new = jnp.maximum(m, jnp.max(s, axis=1, keepdims=True))
            alpha = jnp.exp2(m - m_new)
            p = jnp.exp2(s - m_new)
            l_s[hh] = alpha * l_s[hh] + jnp.sum(p, axis=1, keepdims=True)
            acc_s[hh] = alpha * acc_s[hh] + _dot(p.astype(BF16), vv)
            m_s[hh] = m_new

    if causal:
        n_full = (i * tq + CHUNK) // kblk
    else:
        n_full = l_valid // kblk
    pl.when(j < n_full)(functools.partial(step, False))
    pl.when(j >= n_full)(functools.partial(step, True))

    @pl.when(j == last)
    def _():
        for hh in range(HD):
            o_ref[:, hh * DVD:(hh + 1) * DVD] = (acc_s[hh] / l_s[hh]).astype(o_ref.dtype)


def _mla(qd, kv_up, kpe, prev_out, nseq, sq, l_valid, l_pad, causal, tq, q_off, kv_off, name):
    kblk = _pick(l_pad, (512, 256, 128))
    assert q_off % tq == 0 and kv_off % kblk == 0
    nq, nk = sq // tq, l_pad // kblk
    n_blocks = (lambda i: (i * tq + tq - 1) // kblk + 1) if causal else (lambda i: -(-l_valid // kblk))
    pairs = [(i, j) for i in range(nq) for j in range(n_blocks(i))]
    it = jnp.asarray([p[0] for p in pairs], I32)
    jt = jnp.asarray([p[1] for p in pairs], I32)
    qmap = lambda b, t, it, jt: (q_off // tq + b * nq + it[t], 0)
    kvmap = lambda b, t, it, jt: (kv_off // kblk + b * nk + jt[t], 0)
    kpemap = lambda b, t, it, jt: (b * nk + jt[t], 0)
    in_specs = [pl.BlockSpec((tq, HD * QD_SLOT), qmap),
                pl.BlockSpec((kblk, HD * (DNOPE + DVD)), kvmap), pl.BlockSpec((kblk, DROPE), kpemap)]
    args = [it, jt, qd, kv_up, kpe]
    aliases = {}
    if prev_out is not None:
        in_specs.append(pl.BlockSpec(memory_space=pl.ANY))
        aliases = {len(args): 0}
        args.append(prev_out)
    return pl.pallas_call(
        functools.partial(_mla_kernel, tq, kblk, l_valid, causal),
        grid_spec=pltpu.PrefetchScalarGridSpec(
            num_scalar_prefetch=2,
            grid=(nseq, len(pairs)),
            in_specs=in_specs,
            out_specs=pl.BlockSpec((tq, HD * DVD), qmap),
            scratch_shapes=[pltpu.VMEM((HD, tq, 1), F32), pltpu.VMEM((HD, tq, 1), F32),
                            pltpu.VMEM((HD, tq, DVD), F32)]),
        out_shape=jax.ShapeDtypeStruct((qd.shape[0], HD * DVD), BF16),
        input_output_aliases=aliases,
        compiler_params=_cparams(("arbitrary", "arbitrary")),
        name=name,
    )(*args)


def _proj_out_kernel(a1_ref, a2_ref, w1_ref, w2_ref, x_ref, gate_ref, o_ref):
    y = _dot(a1_ref[...], w1_ref[...]) + _dot(a2_ref[...], w2_ref[...])
    o_ref[...] = x_ref[...] + _gate_rows(y, gate_ref[...])


def _proj_out(a1, a2, w_out, x, modc, layer):
    t, d = x.shape
    tm = _token_tile(t)
    rows = tm // CHUNK
    k1, k2 = a1.shape[1], a2.shape[1]
    assert k1 == k2
    w = _to_bf16(w_out)
    return pl.pallas_call(
        _proj_out_kernel,
        grid=(t // tm,),
        in_specs=[pl.BlockSpec((tm, k1), lambda i: (i, 0)), pl.BlockSpec((tm, k2), lambda i: (i, 0)),
                  _resident((k1, d), lambda i: (0, 0)), _resident((k2, d), lambda i: (1, 0)),
                  pl.BlockSpec((tm, d), lambda i: (i, 0)),
                  pl.BlockSpec((None, rows, d), lambda i: (layer, i, 2))],
        out_specs=pl.BlockSpec((tm, d), lambda i: (i, 0)),
        out_shape=jax.ShapeDtypeStruct((t, d), F32),
        compiler_params=_cparams(("arbitrary",)),
        name="proj_out",
    )(a1, a2, w, w, x, modc)


def _ffn_kernel(final_norm, x_ref, g_ref, sc_ref, sh_ref, gate_ref, wg_ref, wu_ref, wd_ref, gf_ref,
                *rest):
    h_s, acc_s = rest[-2:]
    i = pl.program_id(0)
    f = pl.program_id(1)

    @pl.when(f == 0)
    def _():
        h_s[...] = _modulate(x_ref[...], g_ref[...], sc_ref[...], sh_ref[...]).astype(BF16)
        acc_s[...] = jnp.zeros(acc_s.shape, F32)

    h = h_s[...]
    a = _dot(h, wg_ref[...])
    u = _dot(h, wu_ref[...])
    acc_s[...] += _dot((a * jax.nn.sigmoid(a) * u).astype(BF16), wd_ref[...])

    @pl.when(f == pl.num_programs(1) - 1)
    def _():
        y = x_ref[...] + _gate_rows(acc_s[...], gate_ref[...])
        if final_norm is None:
            rest[0][...] = y
        else:
            y = y * lax.rsqrt(jnp.mean(y * y, axis=-1, keepdims=True) + EPS) * gf_ref[...]
            yp_ref, ys_ref = rest[:2]

            @pl.when(i < final_norm)
            def _():
                yp_ref[...] = y

            @pl.when(i >= final_norm)
            def _():
                ys_ref[...] = y


def _ffn(x, modc, layer, g, wg, wu, wd, g_final, prompt_rows=None):
    t, d = x.shape
    f = wg.shape[1]
    tm = _token_tile(t)
    tf = _pick(f, (512, 256, 128))
    rows = tm // CHUNK
    modspec = lambda comp: pl.BlockSpec((None, rows, d), lambda i, j: (layer, i, comp))
    if prompt_rows is None:
        final_norm = None
        out_specs = pl.BlockSpec((tm, d), lambda i, j: (i, 0))
        out_shape = jax.ShapeDtypeStruct((t, d), F32)
    else:
        assert prompt_rows % tm == 0
        final_norm = prompt_rows // tm
        out_specs = [pl.BlockSpec((tm, d), lambda i, j: (jnp.minimum(i, final_norm - 1), 0)),
                     pl.BlockSpec((tm, d), lambda i, j: (jnp.maximum(i - final_norm, 0), 0))]
        out_shape = [jax.ShapeDtypeStruct((prompt_rows, d), F32),
                     jax.ShapeDtypeStruct((t - prompt_rows, d), F32)]
    return pl.pallas_call(
        functools.partial(_ffn_kernel, final_norm),
        grid=(t // tm, f // tf),
        in_specs=[pl.BlockSpec((tm, d), lambda i, j: (i, 0)),
                  _resident((1, d), lambda i, j: (0, 0)),
                  modspec(4), modspec(3), modspec(5),
                  pl.BlockSpec((d, tf), lambda i, j: (0, j)),
                  pl.BlockSpec((d, tf), lambda i, j: (0, j)),
                  pl.BlockSpec((tf, d), lambda i, j: (j, 0)),
                  _resident((1, d), lambda i, j: (0, 0))],
        out_specs=out_specs,
        out_shape=out_shape,
        scratch_shapes=[pltpu.VMEM((tm, d), BF16), pltpu.VMEM((tm, d), F32)],
        compiler_params=_cparams(("arbitrary", "arbitrary")),
        name="ffn",
    )(x, g, modc, modc, modc, _to_bf16(wg), _to_bf16(wu), _to_bf16(wd), g_final)


def _round_up(n, m):
    return (n + m - 1) // m * m


def kernel(x_prompt, x_sample, c_prompt, c_sample, cache_a_k, cache_a_v, state_b_c, state_b_n, state_b_m,
           cache_c_k, cache_c_v, cache_c_idx, cache_d_ckv, cache_d_kpe, w_ada, b_ada, g_norm_mix, g_norm_ffn,
           w_in_even, w_out_even, sinks_a, b_igate, b_fgate, g_mlstm, w_in_odd, w_out_odd, g_ckv, w_uk, w_uv,
           w_ffn_gate, w_ffn_up, w_ffn_down, g_final):
    nb, s, d = x_prompt.shape
    db, ds, _ = x_sample.shape
    assert ds == CHUNK and s % CHUNK == 0
    depth = w_ada.shape[0]
    past = cache_c_k.shape[2]
    tp, ts = nb * s, db * ds
    t = tp + ts
    tm = _token_tile(t)
    assert s % tm == 0 and tp % tm == 0
    seq_chunks = s // CHUNK
    npc = tp // CHUNK

    x = jnp.concatenate([x_prompt.reshape(tp, d), x_sample.reshape(ts, d)], axis=0)
    mod = _ada(jnp.concatenate([c_prompt, c_sample], axis=0), w_ada, b_ada)
    mod_p = jnp.broadcast_to(mod[:, :nb, None, :], (depth, nb, seq_chunks, 6 * d)).reshape(depth, npc, 6 * d)
    modc = jnp.concatenate([mod_p, mod[:, nb:]], axis=1)

    pos = jnp.concatenate([jnp.arange(s, dtype=I32), jnp.tile(past + jnp.arange(ds, dtype=I32), tm // ds)])
    tab_a = _rope_tables(pos, ROT_A, DA)
    tab_c = _rope_tables(pos, ROT_C, DC)
    tab_d = _rope_tables(pos, DROPE, DROPE)
    prompt_tiles, seq_tiles = tp // tm, s // tm
    tab_map = lambda i: jnp.where(i < prompt_tiles, i % seq_tiles, seq_tiles)

    w_ukv_all = jnp.concatenate([w_uk, w_uv], axis=2).astype(BF16)
    l_s = past + ds
    lp_s = _round_up(l_s, 512)

    def cat_past(cache, new, width):
        full = jnp.concatenate([cache.reshape(db, past, width), new.reshape(db, ds, width)], axis=1)
        full = jnp.pad(full, ((0, 0), (0, lp_s - l_s), (0, 0)))
        return full.astype(BF16).reshape(db * lp_s, width)

    even_p, even_s, odd_p, odd_s = [], [], [], []
    for l in range(depth):
        i = l // 2
        g_mix = g_norm_mix[l].reshape(1, d)
        if l % 2 == 0:
            gate_bias = _pad_cols(jnp.concatenate([b_igate[i], b_fgate[i]]).reshape(1, 2 * HB), LANES)
            qa, ka, va, qb, kb, vb, gates, og = _proj_in(
                _proj_even_kernel, "proj_even", x, modc, l, g_mix, _even_weight(w_in_even[i]), gate_bias,
                list(tab_a), tab_map,
                [(wd, dt, False) for wd, dt in
                 [(HA * DA, BF16), (KVA * DA, F32), (KVA * DA, F32), (HB * DKB, BF16), (HB * DKB, BF16),
                  (HB * DVB, BF16), (LANES, F32), (HB * DVB, BF16)]])
            ka_s = ka[tp:].reshape(db, ds, KVA * DA)
            va_s = va[tp:].reshape(db, ds, KVA * DA)
            win_k = cache_a_k[i].reshape(db, WINDOW, KVA * DA)
            win_v = cache_a_v[i].reshape(db, WINDOW, KVA * DA)
            kband = jnp.concatenate([win_k, ka_s], axis=1)
            vband = jnp.concatenate([win_v, va_s], axis=1)
            oa = _swa(qa, ka, va, win_k.reshape(db * WINDOW, KVA * DA), win_v.reshape(db * WINDOW, KVA * DA),
                      sinks_a[i], tp, s)
            c0 = jnp.concatenate([jnp.zeros((nb, HB, DKB, DVB), F32), state_b_c[i]], axis=0)
            n0 = jnp.concatenate([jnp.zeros((nb, HB, DKB), F32), state_b_n[i]], axis=0)
            m0 = jnp.concatenate([jnp.zeros((nb, HB), F32), state_b_m[i]], axis=0).reshape(nb + db, 1, HB)
            hb, c_new, n_new, m_new = _mlstm(qb, kb, vb, gates, og, g_mlstm[i].reshape(1, HB * DVB),
                                             c0, n0, m0, npc, seq_chunks, nb)
            m_new = m_new.reshape(nb + db, HB)
            ka_p = ka[:tp].reshape(nb, s, KVA, DA)
            va_p = va[:tp].reshape(nb, s, KVA, DA)
            even_p.append((ka_p[:, -WINDOW:], va_p[:, -WINDOW:], c_new[:nb], n_new[:nb], m_new[:nb]))
            even_s.append((kband[:, -WINDOW:].reshape(db, WINDOW, KVA, DA),
                           vband[:, -WINDOW:].reshape(db, WINDOW, KVA, DA), c_new[nb:], n_new[nb:], m_new[nb:]))
            x = _proj_out(oa, hb, w_out_even[i], x, modc, l)
        else:
            (qc, kc_p, kc_s, vc_p, vc_s, qi, ki_p, ki_s, wi, qd, ckv_p, ckv_s, kpe_p, kpe_s,
             kcb, vcb, kib, ckvb, kpeb) = _proj_in(
                functools.partial(_proj_odd_kernel, tp // tm), "proj_odd", x, modc, l, g_mix,
                _odd_weight(w_in_odd[i]), g_ckv[i].reshape(1, KV_LORA),
                list(tab_a) + list(tab_c) + list(tab_d), tab_map,
                [(HC * DC, BF16, False), (KVC * DC, F32, True), (KVC * DC, F32, True), (HI * DI, BF16, False),
                 (DI, F32, True), (LANES, F32, False), (HD * QD_SLOT, BF16, False), (KV_LORA, F32, True),
                 (DROPE, F32, True), (KVC * DC, BF16, False), (KVC * DC, BF16, False), (DI, BF16, False),
                 (KV_LORA, BF16, False), (DROPE, BF16, False)],
                prompt_rows=tp, vmem=PROJ_ODD_VMEM_LIMIT)
            oc = _dsa(qc, qi, wi, kcb, vcb, kib, None, nb, s, s, s, True, 128, 0, "dsa_prompt")
            oc = _dsa(qc, qi, wi, cat_past(cache_c_k[i], kcb[tp:], KVC * DC),
                      cat_past(cache_c_v[i], vcb[tp:], KVC * DC), cat_past(cache_c_idx[i], kib[tp:], DI),
                      oc, db, ds, l_s, lp_s, False, ds, tp, "dsa_sample")
            k_cat, v_t = _latent_up_t(ckvb, kpeb, w_uk[i].astype(BF16), w_uv[i].T.astype(BF16), tp)
            od = _mla_t(qd, k_cat, v_t, nb, s, _pick(s, (1024, 512, 256, 128)))
            kv_up = _latent_up(cat_past(cache_d_ckv[i], ckvb[tp:], KV_LORA), w_ukv_all[i])
            od = _mla(qd, kv_up, cat_past(cache_d_kpe[i], kpeb[tp:], DROPE), od,
                      db, ds, l_s, lp_s, False, ds, tp, 0, "mla_sample")
            odd_p.append((kc_p.reshape(nb, s, KVC, DC), vc_p.reshape(nb, s, KVC, DC),
                          ki_p.reshape(nb, s, DI), ckv_p.reshape(nb, s, KV_LORA), kpe_p.reshape(nb, s, DROPE)))
            odd_s.append((kc_s.reshape(db, ds, KVC, DC), vc_s.reshape(db, ds, KVC, DC),
                          ki_s.reshape(db, ds, DI), ckv_s.reshape(db, ds, KV_LORA), kpe_s.reshape(db, ds, DROPE)))
            x = _proj_out(oc, od, w_out_odd[i], x, modc, l)
        x = _ffn(x, modc, l, g_norm_ffn[l].reshape(1, d), w_ffn_gate[l], w_ffn_up[l], w_ffn_down[l],
                 g_final.reshape(1, d), tp if l == depth - 1 else None)

    y_prompt = x[0].reshape(nb, s, d)
    y_sample = x[1].reshape(db, ds, d)
    st_p = [jnp.stack([e[j] for e in even_p]) for j in range(5)] + [jnp.stack([o[j] for o in odd_p]) for j in range(5)]
    st_s = [jnp.stack([e[j] for e in even_s]) for j in range(5)] + [jnp.stack([o[j] for o in odd_s]) for j in range(5)]
    return (y_prompt, y_sample, *st_p, *st_s)
```

```python
import functools
import math

import jax
import jax.numpy as jnp
import numpy as np
from jax import lax
from jax.experimental import pallas as pl
from jax.experimental.pallas import tpu as pltpu

F32 = jnp.float32
BF16 = jnp.bfloat16
I32 = jnp.int32

CHUNK = 64
ROPE_THETA = 500000.0
EPS = 1e-6
NEG = -1e30
LANES = 128

HA, KVA, DA = 16, 2, 64
ROT_A = DA // 4
WINDOW = 128
WIN_CHUNKS = WINDOW // CHUNK
HB, DKB, DVB = 4, 128, 256
HC, KVC, DC = 8, 2, 128
ROT_C = DC // 4
HI, DI = 16, 64
ROT_I = DI // 4
INDEX_TOPK = 256
HD, DNOPE, DROPE, DVD, KV_LORA = 8, 128, 64, 128, 512
LOG2E = math.log2(math.e)
DSA_SCALE = DC ** -0.5 * LOG2E
MLA_SCALE = (DNOPE + DROPE) ** -0.5 * LOG2E
QD_SLOT = 2 * LANES

VMEM_LIMIT = 56 * 1024 * 1024

def _cparams(sem, vmem=VMEM_LIMIT):
    return pltpu.CompilerParams(dimension_semantics=sem, vmem_limit_bytes=vmem)


def _pick(n, options):
    for o in options:
        if n % o == 0:
            return o
    raise ValueError(f"no tile in {options} divides {n}")


def _resident(shape, index_map):
    return pl.BlockSpec(shape, index_map, pipeline_mode=pl.Buffered(1))


def _dot(a, b):
    return jnp.dot(a, b, preferred_element_type=F32)


def _dot_nt(a, b):
    return lax.dot_general(a, b, (((1,), (1,)), ((), ())), preferred_element_type=F32)


def _dot_tn(a, b):
    return lax.dot_general(a, b, (((0,), (0,)), ((), ())), preferred_element_type=F32)


def _modulate(x, g, sc, sh):
    tm, d = x.shape
    y = x * lax.rsqrt(jnp.mean(x * x, axis=-1, keepdims=True) + EPS) * g
    y = y.reshape(tm // CHUNK, CHUNK, d) * (1.0 + sc[:, None, :]) + sh[:, None, :]
    return y.reshape(tm, d)


def _gate_rows(y, g):
    tm, d = y.shape
    return (y.reshape(tm // CHUNK, CHUNK, d) * g[:, None, :]).reshape(tm, d)


def _rope(x, cos, sp, sm, half):
    blocks = []
    for j in range(x.shape[1] // LANES):
        xb = x[:, j * LANES:(j + 1) * LANES]
        blocks.append(xb * cos + pltpu.roll(xb, half, 1) * sp + pltpu.roll(xb, LANES - half, 1) * sm)
    return blocks[0] if len(blocks) == 1 else jnp.concatenate(blocks, axis=1)


def _rope_tables(pos, rot, head):
    half = rot // 2
    inv = ROPE_THETA ** (-jnp.arange(half, dtype=F32) * 2.0 / rot)
    ang = pos.astype(F32)[:, None] * inv[None, :]
    cos, sin = jnp.cos(ang), jnp.sin(ang)
    p = pos.shape[0]
    one = jnp.ones((p, head - rot), F32)
    zero = jnp.zeros((p, head - rot), F32)
    zh = jnp.zeros((p, half), F32)
    c = jnp.concatenate([cos, cos, one], axis=1)
    s_plus = jnp.concatenate([zh, sin, zero], axis=1)
    s_minus = jnp.concatenate([-sin, zh, zero], axis=1)
    rep = LANES // head
    return tuple(jnp.tile(t, (1, rep)) for t in (c, s_plus, s_minus))


CAST_BLOCK_BYTES = 6 * 1024 * 1024


def _cast_kernel(w_ref, o_ref):
    o_ref[...] = w_ref[...].astype(o_ref.dtype)


def _to_bf16(w, index):
    _, r, c = w.shape
    tr = next(t for t in (2048, 1024, 512, 256, 128, 64, 32, 16) if r % t == 0 and t * c * 4 <= CAST_BLOCK_BYTES)
    return pl.pallas_call(
        _cast_kernel,
        grid=(r // tr,),
        in_specs=[pl.BlockSpec((None, tr, c), lambda i: (index, i, 0))],
        out_specs=pl.BlockSpec((tr, c), lambda i: (i, 0)),
        out_shape=jax.ShapeDtypeStruct((r, c), BF16),
        compiler_params=_cparams(("arbitrary",)),
        name="to_bf16",
    )(w)


def _ada_kernel(c_ref, w_ref, b_ref, o_ref):
    c = c_ref[...]
    h = (c * jax.nn.sigmoid(c)).astype(BF16)
    o_ref[...] = _dot(h, w_ref[...].astype(BF16)) + b_ref[...]


def _ada(c, w_ada, b_ada):
    depth, d, n = w_ada.shape
    ns = c.shape[0]
    tn = _pick(n, (1024, 512, 256, 128))
    return pl.pallas_call(
        _ada_kernel,
        grid=(depth, n // tn),
        in_specs=[pl.BlockSpec((ns, d), lambda l, j: (0, 0)),
                  pl.BlockSpec((None, d, tn), lambda l, j: (l, 0, j)),
                  pl.BlockSpec((None, 1, tn), lambda l, j: (l, 0, j))],
        out_specs=pl.BlockSpec((None, ns, tn), lambda l, j: (l, 0, j)),
        out_shape=jax.ShapeDtypeStruct((depth, ns, n), F32),
        compiler_params=_cparams(("arbitrary", "arbitrary")),
        name="ada",
    )(c, w_ada, b_ada.reshape(depth, 1, n))


def _col_tiles(lo, hi, step=512):
    return [(a, min(a + step, hi)) for a in range(lo, hi, step)]


E_QA, E_KA, E_VA, E_QB, E_KB, E_VB, E_GT, E_OB, E_END = 0, 1024, 1152, 1280, 1792, 2304, 3328, 3456, 4480


def _pad_cols(w, width):
    return jnp.pad(w, ((0, 0), (0, width - w.shape[1])))


def _even_weight(w):
    qa, ka, va, qb, kb, vb, ib, fb, ob = jnp.split(
        w, np.cumsum((HA * DA, KVA * DA, KVA * DA, HB * DKB, HB * DKB, HB * DVB, HB, HB))[:].tolist(), axis=1)
    gates = _pad_cols(jnp.concatenate([ib, fb], axis=1), LANES)
    return jnp.concatenate([qa, ka, va, qb, kb, vb, gates, ob], axis=1).astype(BF16)


def _proj_even_kernel(x_ref, g_ref, sc_ref, sh_ref, w_ref, gb_ref, cos_ref, sp_ref, sm_ref,
                      qa_o, ka_o, va_o, qb_o, kb_o, vb_o, gt_o, og_o):
    h = _modulate(x_ref[...], g_ref[...], sc_ref[...], sh_ref[...]).astype(BF16)
    cos, sp, sm = cos_ref[...], sp_ref[...], sm_ref[...]
    half = ROT_A // 2

    def mm(lo, hi):
        return _dot(h, w_ref[:, lo:hi])

    for lo, hi in _col_tiles(E_QA, E_KA):
        qa_o[:, lo - E_QA:hi - E_QA] = _rope(mm(lo, hi), cos, sp, sm, half).astype(qa_o.dtype)
    ka_o[...] = _rope(mm(E_KA, E_VA), cos, sp, sm, half)
    va_o[...] = mm(E_VA, E_QB)
    for lo, hi in _col_tiles(E_QB, E_KB):
        qb_o[:, lo - E_QB:hi - E_QB] = (mm(lo, hi) * (DKB ** -0.5)).astype(qb_o.dtype)
    for lo, hi in _col_tiles(E_KB, E_VB):
        kb_o[:, lo - E_KB:hi - E_KB] = mm(lo, hi).astype(kb_o.dtype)
    for lo, hi in _col_tiles(E_VB, E_GT):
        vb_o[:, lo - E_VB:hi - E_VB] = mm(lo, hi).astype(vb_o.dtype)
    t = mm(E_GT, E_OB) + gb_ref[...]
    lane = lax.broadcasted_iota(I32, t.shape, 1)
    gt_o[...] = jnp.where(lane < HB, t, jax.nn.log_sigmoid(t))
    for lo, hi in _col_tiles(E_OB, E_END):
        og_o[:, lo - E_OB:hi - E_OB] = jax.nn.sigmoid(mm(lo, hi)).astype(og_o.dtype)


(O_QC, O_KC, O_VC, O_QI, O_KI, O_WI, O_QDN, O_QDP, O_CKV, O_KPE, O_END) = (
    0, 1024, 1280, 1536, 2560, 2688, 2816, 3840, 4352, 4864, 4992)


def _odd_weight(w):
    qc, kc, vc, qi, ki, wi, qd, ckv, kpe = jnp.split(
        w, np.cumsum((HC * DC, KVC * DC, KVC * DC, HI * DI, DI, HI, HD * (DNOPE + DROPE), KV_LORA)).tolist(), axis=1)
    d = w.shape[0]
    qd = qd.reshape(d, HD, DNOPE + DROPE)
    qdn = qd[:, :, :DNOPE].reshape(d, HD * DNOPE)
    qdp = qd[:, :, DNOPE:].reshape(d, HD * DROPE)
    return jnp.concatenate([qc, kc, vc, qi, _pad_cols(ki, LANES), _pad_cols(wi, LANES), qdn, qdp, ckv,
                            _pad_cols(kpe, LANES)], axis=1).astype(BF16)


def _proj_odd_kernel(x_ref, g_ref, sc_ref, sh_ref, w_ref, gckv_ref,
                     ci_ref, spi_ref, smi_ref, cc_ref, spc_ref, smc_ref, cd_ref, spd_ref, smd_ref,
                     qc_o, kc_o, vc_o, qi_o, ki_o, wi_o, qd_o, ckv_o, kpe_o,
                     kcb_o, vcb_o, kib_o, ckvb_o, kpeb_o):
    h = _modulate(x_ref[...], g_ref[...], sc_ref[...], sh_ref[...]).astype(BF16)
    tab_i = (ci_ref[...], spi_ref[...], smi_ref[...], ROT_I // 2)
    tab_c = (cc_ref[...], spc_ref[...], smc_ref[...], ROT_C // 2)
    tab_d = (cd_ref[...], spd_ref[...], smd_ref[...], DROPE // 2)

    def mm(lo, hi):
        return _dot(h, w_ref[:, lo:hi])

    for lo, hi in _col_tiles(O_QC, O_KC):
        qc_o[:, lo - O_QC:hi - O_QC] = (_rope(mm(lo, hi), *tab_c) * DSA_SCALE).astype(qc_o.dtype)
    kc = _rope(mm(O_KC, O_VC), *tab_c)
    kc_o[...] = kc
    kcb_o[...] = kc.astype(BF16)
    vc = mm(O_VC, O_QI)
    vc_o[...] = vc
    vcb_o[...] = vc.astype(BF16)
    for lo, hi in _col_tiles(O_QI, O_KI):
        qi_o[:, lo - O_QI:hi - O_QI] = _rope(mm(lo, hi), *tab_i).astype(qi_o.dtype)
    ki = _rope(mm(O_KI, O_WI), *tab_i)[:, :DI]
    ki_o[...] = ki
    kib_o[...] = ki.astype(BF16)
    wi_o[...] = mm(O_WI, O_QDN) * (HI ** -0.5 * DI ** -0.5)
    for lo, hi in _col_tiles(O_QDN, O_QDP):
        qn = (mm(lo, hi) * MLA_SCALE).astype(qd_o.dtype)
        for k in range((hi - lo) // DNOPE):
            hh = (lo - O_QDN) // DNOPE + k
            qd_o[:, hh * QD_SLOT:hh * QD_SLOT + DNOPE] = qn[:, k * DNOPE:(k + 1) * DNOPE]
    qp = _rope(mm(O_QDP, O_CKV), *tab_d) * MLA_SCALE
    low_lanes = lax.broadcasted_iota(I32, (qp.shape[0], LANES), 1) < DROPE
    for hh in range(HD):
        pair = qp[:, (hh // 2) * LANES:(hh // 2 + 1) * LANES]
        if hh % 2:
            pair = pltpu.roll(pair, DROPE, 1)
        qd_o[:, hh * QD_SLOT + DNOPE:(hh + 1) * QD_SLOT] = jnp.where(low_lanes, pair, 0.0).astype(qd_o.dtype)
    c = mm(O_CKV, O_KPE)
    c = c * lax.rsqrt(jnp.mean(c * c, axis=-1, keepdims=True) + EPS) * gckv_ref[...]
    ckv_o[...] = c
    ckvb_o[...] = c.astype(BF16)
    kpe = _rope(mm(O_KPE, O_END), *tab_d)[:, :DROPE]
    kpe_o[...] = kpe
    kpeb_o[...] = kpe.astype(BF16)


def _token_tile(t):
    return _pick(t, (512,))


def _proj_in(kernel_fn, name, x, modc, layer, g, w, extra, tables, tab_map, outs):
    t, d = x.shape
    tm = _token_tile(t)
    rows = tm // CHUNK
    in_specs = [pl.BlockSpec((tm, d), lambda i: (i, 0)),
                _resident((1, d), lambda i: (0, 0)),
                pl.BlockSpec((None, rows, d), lambda i: (layer, i, 1)),
                pl.BlockSpec((None, rows, d), lambda i: (layer, i, 0)),
                _resident(w.shape, lambda i: (0, 0)),
                _resident(extra.shape, lambda i: (0, 0))]
    in_specs += [pl.BlockSpec((tm, LANES), lambda i: (tab_map(i), 0)) for _ in tables]
    return pl.pallas_call(
        kernel_fn,
        grid=(t // tm,),
        in_specs=in_specs,
        out_specs=[pl.BlockSpec((tm, wd), lambda i: (i, 0)) for wd, _ in outs],
        out_shape=[jax.ShapeDtypeStruct((t, wd), dt) for wd, dt in outs],
        compiler_params=_cparams(("arbitrary",)),
        name=name,
    )(x, g, modc, modc, w, extra, *tables)


def _swa_kernel(tq, tiles_per_seq, sink_ref, q_ref, kw_ref, kc_ref, vw_ref, vc_ref, *rest):
    o_ref = rest[-1]
    kb = jnp.concatenate([kw_ref[...], kc_ref[...]], axis=0).astype(BF16)
    vb = jnp.concatenate([vw_ref[...], vc_ref[...]], axis=0).astype(BF16)
    qchunk = lax.broadcasted_iota(I32, (tq, WINDOW + tq), 0) // CHUNK
    kchunk = lax.broadcasted_iota(I32, (tq, WINDOW + tq), 1) // CHUNK
    valid = jnp.logical_and(kchunk >= qchunk, kchunk <= qchunk + WIN_CHUNKS)
    if tiles_per_seq:
        has_window = pl.program_id(0) % tiles_per_seq != 0
        valid = jnp.logical_and(valid, jnp.logical_or(kchunk >= WIN_CHUNKS, has_window))
    bias = jnp.where(valid, 0.0, NEG)
    group = HA // KVA
    for hq in range(HA):
        kv = hq // group
        q = q_ref[:, hq * DA:(hq + 1) * DA]
        s = _dot_nt(q, kb[:, kv * DA:(kv + 1) * DA]) * (DA ** -0.5) + bias
        sink = sink_ref[hq]
        m = jnp.maximum(jnp.max(s, axis=-1, keepdims=True), sink)
        p = jnp.exp(s - m)
        den = jnp.sum(p, axis=-1, keepdims=True) + jnp.exp(sink - m)
        o = _dot(p.astype(BF16), vb[:, kv * DA:(kv + 1) * DA]) / den
        o_ref[:, hq * DA:(hq + 1) * DA] = o.astype(o_ref.dtype)


def _swa(qa, ka, va, win_k, win_v, sinks, tp, s):
    t = qa.shape[0]
    tq = _pick(s, (WINDOW,))
    wpt = tq // WINDOW
    qspec = lambda rows, off: pl.BlockSpec((rows, HA * DA), lambda g: (g + off, 0))
    kvspec = lambda rows, off: pl.BlockSpec((rows, KVA * DA), lambda g: (g + off, 0))
    wspec_p = pl.BlockSpec((WINDOW, KVA * DA), lambda g: (jnp.maximum(g * wpt - 1, 0), 0))
    smem = pl.BlockSpec(memory_space=pltpu.SMEM)
    out_shape = jax.ShapeDtypeStruct((t, HA * DA), BF16)
    oa = pl.pallas_call(
        functools.partial(_swa_kernel, tq, s // tq),
        grid=(tp // tq,),
        in_specs=[smem, qspec(tq, 0), wspec_p, kvspec(tq, 0), wspec_p, kvspec(tq, 0)],
        out_specs=qspec(tq, 0),
        out_shape=out_shape,
        compiler_params=_cparams(("arbitrary",)),
        name="swa_prompt",
    )(sinks, qa, ka, ka, va, va)
    off = tp // CHUNK
    return pl.pallas_call(
        functools.partial(_swa_kernel, CHUNK, 0),
        grid=((t - tp) // CHUNK,),
        in_specs=[smem, qspec(CHUNK, off), kvspec(WINDOW, 0), kvspec(CHUNK, off), kvspec(WINDOW, 0),
                  kvspec(CHUNK, off), pl.BlockSpec(memory_space=pl.ANY)],
        out_specs=qspec(CHUNK, off),
        out_shape=out_shape,
        input_output_aliases={6: 0},
        compiler_params=_cparams(("arbitrary",)),
        name="swa_sample",
    )(sinks, qa, win_k, ka, win_v, va, oa)


def _mlstm_kernel(n_prompt_chunks, seq_chunks, q_ref, k_ref, v_ref, gt_ref, og_ref, gmh_ref,
                  c0_ref, n0_ref, m0_ref, h_o, c_o, n_o, m_o, c_s, n_s, m_s):
    g = pl.program_id(0)
    is_prompt = g < n_prompt_chunks
    c_idx = g % seq_chunks
    first = jnp.logical_or(jnp.logical_not(is_prompt), c_idx == 0)
    last = jnp.logical_or(jnp.logical_not(is_prompt), c_idx == seq_chunks - 1)

    @pl.when(first)
    def _():
        c_s[...] = c0_ref[...]
        n_s[...] = n0_ref[...]
        m_s[...] = m0_ref[...]

    gt = gt_ref[...]
    gt_t = gt.T
    row = lax.broadcasted_iota(I32, (CHUNK, CHUNK), 0)
    col = lax.broadcasted_iota(I32, (CHUNK, CHUNK), 1)
    causal = col <= row
    for hh in range(HB):
        ig_row = gt_t[hh:hh + 1, :]
        lf_row = gt_t[HB + hh:HB + hh + 1, :]
        ig_col = gt[:, hh:hh + 1]
        lf_col = gt[:, HB + hh:HB + hh + 1]
        b_col = jnp.sum(jnp.where(causal, lf_row, 0.0), axis=1, keepdims=True)
        b_row = jnp.sum(jnp.where(row <= col, lf_col, 0.0), axis=0, keepdims=True)
        m_prev = m_s[:, hh:hh + 1]
        d = jnp.where(causal, b_col - b_row + ig_row, NEG)
        inter = b_col + m_prev
        m_t = jnp.maximum(inter, jnp.max(d, axis=1, keepdims=True))
        w_intra = jnp.exp(d - m_t)
        w_inter = jnp.exp(inter - m_t)
        q = q_ref[:, hh * DKB:(hh + 1) * DKB]
        k = k_ref[:, hh * DKB:(hh + 1) * DKB]
        v = v_ref[:, hh * DVB:(hh + 1) * DVB]
        c_prev = c_s[hh]
        n_prev = n_s[hh:hh + 1, :]
        a = w_intra * _dot_nt(q, k)
        num = _dot(a.astype(BF16), v) + w_inter * _dot(q, c_prev.astype(BF16))
        den = (jnp.sum(a, axis=1, keepdims=True)
               + w_inter * jnp.sum(q.astype(F32) * n_prev, axis=1, keepdims=True))
        hv = num / jnp.maximum(jnp.abs(den), jnp.exp(-m_t))
        b_end = b_col[CHUNK - 1:CHUNK, :]
        g_row = b_end - b_row + ig_row
        g_col = b_end - b_col + ig_col
        m_new = jnp.maximum(b_end + m_prev, jnp.max(g_row, axis=1, keepdims=True))
        w_s = jnp.exp(g_col - m_new)
        w_c = jnp.exp(b_end + m_prev - m_new)
        c_s[hh] = w_c * c_prev + _dot_tn(k, (w_s * v.astype(F32)).astype(BF16))
        n_s[hh:hh + 1, :] = w_c * n_prev + jnp.sum(w_s * k.astype(F32), axis=0, keepdims=True)
        m_s[:, hh:hh + 1] = m_new
        y = hv * lax.rsqrt(jnp.mean(hv * hv, axis=-1, keepdims=True) + EPS)
        y = y * gmh_ref[:, hh * DVB:(hh + 1) * DVB] * og_ref[:, hh * DVB:(hh + 1) * DVB].astype(F32)
        h_o[:, hh * DVB:(hh + 1) * DVB] = y.astype(h_o.dtype)

    @pl.when(last)
    def _():
        c_o[...] = c_s[...]
        n_o[...] = n_s[...]
        m_o[...] = m_s[...]


def _mlstm(qb, kb, vb, gates, og, g_mh, c0, n0, m0, n_prompt_chunks, seq_chunks, n_prompt_seqs):
    t = qb.shape[0]
    nch = t // CHUNK
    ns = c0.shape[0]

    def seq(g):
        return jnp.where(g < n_prompt_chunks, g // seq_chunks, n_prompt_seqs + g - n_prompt_chunks)

    tok = lambda w: pl.BlockSpec((CHUNK, w), lambda g: (g, 0))
    st_specs = [pl.BlockSpec((None, HB, DKB, DVB), lambda g: (seq(g), 0, 0, 0)),
                pl.BlockSpec((None, HB, DKB), lambda g: (seq(g), 0, 0)),
                pl.BlockSpec((None, 1, HB), lambda g: (seq(g), 0, 0))]
    return pl.pallas_call(
        functools.partial(_mlstm_kernel, n_prompt_chunks, seq_chunks),
        grid=(nch,),
        in_specs=[tok(HB * DKB), tok(HB * DKB), tok(HB * DVB), tok(LANES), tok(HB * DVB),
                  _resident((1, HB * DVB), lambda g: (0, 0))] + st_specs,
        out_specs=[tok(HB * DVB)] + st_specs,
        out_shape=[jax.ShapeDtypeStruct((t, HB * DVB), BF16),
                   jax.ShapeDtypeStruct((ns, HB, DKB, DVB), F32),
                   jax.ShapeDtypeStruct((ns, HB, DKB), F32),
                   jax.ShapeDtypeStruct((ns, 1, HB), F32)],
        scratch_shapes=[pltpu.VMEM((HB, DKB, DVB), F32), pltpu.VMEM((HB, DKB), F32),
                        pltpu.VMEM((1, HB), F32)],
        compiler_params=_cparams(("arbitrary",)),
        name="mlstm",
    )(qb, kb, vb, gates, og, g_mh, c0, n0, m0)


INT_MIN = -2 ** 31


def _order_key(x):
    b = pltpu.bitcast(x, I32)
    return jnp.where(b >= 0, b, b ^ jnp.int32(0x7FFFFFFF))


def _dsa_kernel(tq, kblk, l_valid, l_pad, topk, causal,
                qc_ref, qi_ref, wi_ref, k_ref, v_ref, ki_ref, *rest):
    o_ref, key_s = rest[-2:]
    i = pl.program_id(1)
    q0 = i * tq
    rowq = lax.broadcasted_iota(I32, (tq, 1), 0)
    if causal:
        lim = ((q0 + rowq) // CHUNK + 1) * CHUNK
        nkb = (q0 + tq + kblk - 1) // kblk
    else:
        lim = jnp.full((tq, 1), l_valid, I32)
        nkb = l_pad // kblk
    lane = lax.broadcasted_iota(I32, (tq, kblk), 1)

    wi = wi_ref[...]

    def score_block(kb, carry):
        start = pl.multiple_of(kb * kblk, kblk)
        kib = ki_ref[pl.ds(start, kblk), :]
        acc = jnp.zeros((tq, kblk), F32)
        for hh in range(HI):
            s = _dot_nt(qi_ref[:, hh * DI:(hh + 1) * DI], kib)
            acc = acc + wi[:, hh:hh + 1] * jnp.maximum(s, 0.0)
        acc = acc + 0.0
        acc = jnp.where(start + lane < lim, acc, NEG)
        key_s[kb] = _order_key(acc)
        return carry

    lax.fori_loop(0, nkb, score_block, 0)

    lane1 = lax.broadcasted_iota(I32, (tq, LANES), 1)

    def count(pred_fn):
        def blk(kb, part):
            start = kb * kblk
            for c in range(kblk // LANES):
                keys = key_s[kb, :, c * LANES:(c + 1) * LANES]
                part = part + pred_fn(keys, start + c * LANES + lane1).astype(I32)
            return part
        part = lax.fori_loop(0, nkb, blk, jnp.zeros((tq, LANES), I32))
        return jnp.sum(part, axis=1, keepdims=True)

    def thr_bit(carry):
        b, thr, n_ge, _ = carry
        cand = thr + lax.shift_left(jnp.int32(1), 31 - b)
        cand_b = jnp.broadcast_to(cand, (tq, LANES))
        cnt = count(lambda keys, idx: keys >= cand_b)
        keep = cnt >= topk
        n_ge = jnp.where(keep, cnt, n_ge)
        return b + 1, jnp.where(keep, cand, thr), n_ge, jnp.max(n_ge)

    _, thr, n_ge, _ = lax.while_loop(
        lambda c: jnp.logical_and(c[0] < 32, c[3] > topk), thr_bit,
        (jnp.int32(0), jnp.full((tq, 1), INT_MIN, I32), jnp.full((tq, 1), l_pad, I32), jnp.int32(l_pad)))

    nbits = max(1, (l_pad - 1).bit_length())

    def tie_cut():
        thr_b = jnp.broadcast_to(thr, (tq, LANES))
        need = topk - count(lambda keys, idx: keys > thr_b)

        def cut_bit(b, cut):
            cand = cut + lax.shift_left(jnp.int32(1), nbits - 1 - b)
            cand_b = jnp.broadcast_to(cand, (tq, LANES))
            cnt = count(lambda keys, idx: jnp.logical_and(keys == thr_b, idx < cand_b))
            return jnp.where(cnt < need, cand, cut)

        return lax.fori_loop(0, nbits, cut_bit, jnp.zeros((tq, 1), I32))

    cut = lax.cond(jnp.max(n_ge) > topk, tie_cut, lambda: jnp.full((tq, 1), l_pad, I32))

    group = HC // KVC
    qs = [jnp.concatenate([qc_ref[:, (kv * group + j) * DC:(kv * group + j + 1) * DC]
                           for j in range(group)], axis=0) for kv in range(KVC)]

    def attend_block(kb, carry):
        start = pl.multiple_of(kb * kblk, kblk)
        keys = key_s[kb]
        idx = start + lane
        sel = jnp.logical_or(keys > thr, jnp.logical_and(keys == thr, idx <= cut))
        sel = jnp.logical_and(sel, idx < lim)
        bias = jnp.where(sel, 0.0, NEG)
        new = []
        for kv in range(KVC):
            m, l, acc = carry[kv]
            kk = k_ref[pl.ds(start, kblk), kv * DC:(kv + 1) * DC]
            vv = v_ref[pl.ds(start, kblk), kv * DC:(kv + 1) * DC]
            s = _dot_nt(qs[kv], kk)
            s = jnp.concatenate([s[j * tq:(j + 1) * tq] + bias for j in range(group)], axis=0)
            m_new = jnp.maximum(m, jnp.max(s, axis=1, keepdims=True))
            alpha = jnp.exp2(m - m_new)
            p = jnp.exp2(s - m_new)
            l = alpha * l + jnp.sum(p, axis=1, keepdims=True)
            acc = alpha * acc + _dot(p.astype(BF16), vv)
            new.append((m_new, l, acc))
        return tuple(new)

    init = tuple((jnp.full((group * tq, 1), NEG, F32), jnp.zeros((group * tq, 1), F32),
                  jnp.zeros((group * tq, DC), F32)) for _ in range(KVC))
    res = lax.fori_loop(0, nkb, attend_block, init)
    for kv in range(KVC):
        _, l, acc = res[kv]
        o = acc / l
        for j in range(group):
            hq = kv * group + j
            o_ref[:, hq * DC:(hq + 1) * DC] = o[j * tq:(j + 1) * tq, :].astype(o_ref.dtype)


def _dsa(qc, qi, wi, k, v, ki, prev_out, nseq, sq, l_valid, l_pad, causal, tq, q_off, name):
    topk = min(INDEX_TOPK, l_valid // 4)
    kblk = _pick(l_pad, (512, 256, 128))
    assert kblk >= topk and sq % tq == 0 and q_off % tq == 0
    nq = sq // tq
    qmap = lambda b, i: (q_off // tq + b * nq + i, 0)
    kmap = lambda b, i: (b, 0)
    in_specs = [pl.BlockSpec((tq, HC * DC), qmap), pl.BlockSpec((tq, HI * DI), qmap),
                pl.BlockSpec((tq, LANES), qmap),
                _resident((l_pad, KVC * DC), kmap), _resident((l_pad, KVC * DC), kmap),
                _resident((l_pad, DI), kmap)]
    args = [qc, qi, wi, k, v, ki]
    aliases = {}
    if prev_out is not None:
        in_specs.append(pl.BlockSpec(memory_space=pl.ANY))
        aliases = {len(args): 0}
        args.append(prev_out)
    return pl.pallas_call(
        functools.partial(_dsa_kernel, tq, kblk, l_valid, l_pad, topk, causal),
        grid=(nseq, nq),
        in_specs=in_specs,
        out_specs=pl.BlockSpec((tq, HC * DC), qmap),
        out_shape=jax.ShapeDtypeStruct((qc.shape[0], HC * DC), BF16),
        scratch_shapes=[pltpu.VMEM((l_pad // kblk, tq, kblk), I32)],
        input_output_aliases=aliases,
        compiler_params=_cparams(("arbitrary", "arbitrary")),
        name=name,
    )(*args)


def _matmul_kernel(a_ref, w_ref, o_ref):
    o_ref[...] = _dot(a_ref[...], w_ref[...]).astype(o_ref.dtype)


def _latent_up(ckv, w_ukv):
    r = ckv.shape[0]
    tm = _pick(r, (1024, 512, 256, 128, 64))
    n = w_ukv.shape[1]
    return pl.pallas_call(
        _matmul_kernel,
        grid=(r // tm,),
        in_specs=[pl.BlockSpec((tm, KV_LORA), lambda i: (i, 0)), _resident(w_ukv.shape, lambda i: (0, 0))],
        out_specs=pl.BlockSpec((tm, n), lambda i: (i, 0)),
        out_shape=jax.ShapeDtypeStruct((r, n), BF16),
        compiler_params=_cparams(("arbitrary",)),
        name="latent_up",
    )(ckv, w_ukv)


def _mla_last_block(i, tq, kblk):
    return (i * tq + tq - 1) // kblk


def _latent_up_t_kernel(ckv_ref, kpe_ref, wk_ref, wvt_ref, k_o, vt_o):
    ckv = ckv_ref[...]
    kn = _dot(ckv, wk_ref[...]).astype(k_o.dtype)
    kpe = jnp.concatenate([kpe_ref[...], jnp.zeros((kpe_ref.shape[0], LANES - DROPE), k_o.dtype)], axis=1)
    for hh in range(HD):
        k_o[:, hh * QD_SLOT:hh * QD_SLOT + DNOPE] = kn[:, hh * DNOPE:(hh + 1) * DNOPE]
        k_o[:, hh * QD_SLOT + DNOPE:(hh + 1) * QD_SLOT] = kpe
    vt_o[...] = _dot_nt(wvt_ref[...], ckv).astype(vt_o.dtype)


def _latent_up_t(ckv, kpe, w_uk, w_uv_t, rows):
    tm = _pick(rows, (512, 256, 128))
    return pl.pallas_call(
        _latent_up_t_kernel,
        grid=(rows // tm,),
        in_specs=[pl.BlockSpec((tm, KV_LORA), lambda i: (i, 0)), pl.BlockSpec((tm, DROPE), lambda i: (i, 0)),
                  _resident(w_uk.shape, lambda i: (0, 0)), _resident(w_uv_t.shape, lambda i: (0, 0))],
        out_specs=[pl.BlockSpec((tm, HD * QD_SLOT), lambda i: (i, 0)),
                   pl.BlockSpec((HD * DVD, tm), lambda i: (0, i))],
        out_shape=[jax.ShapeDtypeStruct((rows, HD * QD_SLOT), BF16),
                   jax.ShapeDtypeStruct((HD * DVD, rows), BF16)],
        compiler_params=_cparams(("arbitrary",)),
        name="latent_up_t",
    )(ckv, kpe, w_uk, w_uv_t)


def _mla_t_kernel(tq, kblk, it_ref, jt_ref, q_ref, k_ref, vt_ref, o_ref, m_s, l_s, acc_s):
    step_id = pl.program_id(1)
    i = it_ref[step_id]
    j = jt_ref[step_id]
    last = _mla_last_block(i, tq, kblk)

    @pl.when(j == 0)
    def _():
        m_s[...] = jnp.full(m_s.shape, NEG, F32)
        l_s[...] = jnp.zeros(l_s.shape, F32)
        acc_s[...] = jnp.zeros(acc_s.shape, F32)

    def step(masked):
        if masked:
            kidx = j * kblk + lax.broadcasted_iota(I32, (kblk, tq), 0)
            qpos = i * tq + lax.broadcasted_iota(I32, (kblk, tq), 1)
            bias = jnp.where(kidx < (qpos // CHUNK + 1) * CHUNK, 0.0, NEG)
        for hh in range(HD):
            s = _dot_nt(k_ref[:, hh * QD_SLOT:(hh + 1) * QD_SLOT], q_ref[:, hh * QD_SLOT:(hh + 1) * QD_SLOT])
            if masked:
                s = s + bias
            m = m_s[hh]
            m_new = jnp.maximum(m, jnp.max(s, axis=0, keepdims=True))
            alpha = jnp.exp2(m - m_new)
            p = jnp.exp2(s - m_new)
            l_s[hh] = alpha * l_s[hh] + jnp.sum(p, axis=0, keepdims=True)
            acc_s[hh] = alpha * acc_s[hh] + _dot(vt_ref[hh * DVD:(hh + 1) * DVD, :], p.astype(BF16))
            m_s[hh] = m_new

    n_full = (i * tq + CHUNK) // kblk
    pl.when(j < n_full)(functools.partial(step, False))
    pl.when(j >= n_full)(functools.partial(step, True))

    @pl.when(j == last)
    def _():
        for hh in range(HD):
            o_ref[:, hh * DVD:(hh + 1) * DVD] = (acc_s[hh] / l_s[hh]).T.astype(o_ref.dtype)


def _mla_t(qd, k_cat, v_t, nseq, sq, tq):
    kblk = _pick(sq, (512, 256, 128))
    nq, nk = sq // tq, sq // kblk
    pairs = [(i, j) for i in range(nq) for j in range(_mla_last_block(i, tq, kblk) + 1)]
    it = jnp.asarray([p[0] for p in pairs], I32)
    jt = jnp.asarray([p[1] for p in pairs], I32)
    qmap = lambda b, t, it, jt: (b * nq + it[t], 0)
    return pl.pallas_call(
        functools.partial(_mla_t_kernel, tq, kblk),
        grid_spec=pltpu.PrefetchScalarGridSpec(
            num_scalar_prefetch=2,
            grid=(nseq, len(pairs)),
            in_specs=[pl.BlockSpec((tq, HD * QD_SLOT), qmap),
                      pl.BlockSpec((kblk, HD * QD_SLOT), lambda b, t, it, jt: (b * nk + jt[t], 0)),
                      pl.BlockSpec((HD * DVD, kblk), lambda b, t, it, jt: (0, b * nk + jt[t]))],
            out_specs=pl.BlockSpec((tq, HD * DVD), qmap),
            scratch_shapes=[pltpu.VMEM((HD, 1, tq), F32), pltpu.VMEM((HD, 1, tq), F32),
                            pltpu.VMEM((HD, DVD, tq), F32)]),
        out_shape=jax.ShapeDtypeStruct((qd.shape[0], HD * DVD), BF16),
        compiler_params=_cparams(("arbitrary", "arbitrary")),
        name="mla_prompt",
    )(it, jt, qd, k_cat, v_t)


def _mla_kernel(tq, kblk, l_valid, causal, it_ref, jt_ref, q_ref, kv_ref, kpe_ref, *rest):
    o_ref, m_s, l_s, acc_s = rest[-4:]
    step_id = pl.program_id(1)
    i = it_ref[step_id]
    j = jt_ref[step_id]
    last = _mla_last_block(i, tq, kblk) if causal else pl.cdiv(l_valid, kblk) - 1

    @pl.when(j == 0)
    def _():
        m_s[...] = jnp.full(m_s.shape, NEG, F32)
        l_s[...] = jnp.zeros(l_s.shape, F32)
        acc_s[...] = jnp.zeros(acc_s.shape, F32)

    def step(masked):
        if masked:
            idx = j * kblk + lax.broadcasted_iota(I32, (tq, kblk), 1)
            if causal:
                rowq = i * tq + lax.broadcasted_iota(I32, (tq, kblk), 0)
                valid = idx < (rowq // CHUNK + 1) * CHUNK
            else:
                valid = idx < l_valid
            bias = jnp.where(valid, 0.0, NEG)
        kpe = kpe_ref[...]
        for hh in range(HD):
            kn = kv_ref[:, hh * DNOPE:(hh + 1) * DNOPE]
            vv = kv_ref[:, HD * DNOPE + hh * DVD:HD * DNOPE + (hh + 1) * DVD]
            s = (_dot_nt(q_ref[:, hh * QD_SLOT:hh * QD_SLOT + DNOPE], kn)
                 + _dot_nt(q_ref[:, hh * QD_SLOT + DNOPE:hh * QD_SLOT + DNOPE + DROPE], kpe))
            if masked:
                s = s + bias
            m = m_s[hh]
            m_new = jnp.maximum(m, jnp.max(s, axis=1, keepdims=True))
            alpha = jnp.exp2(m - m_new)
            p = jnp.exp2(s - m_new)
            l_s[hh] = alpha * l_s[hh] + jnp.sum(p, axis=1, keepdims=True)
            acc_s[hh] = alpha * acc_s[hh] + _dot(p.astype(BF16), vv)
            m_s[hh] = m_new

    if causal:
        n_full = (i * tq + CHUNK) // kblk
    else:
        n_full = l_valid // kblk
    pl.when(j < n_full)(functools.partial(step, False))
    pl.when(j >= n_full)(functools.partial(step, True))

    @pl.when(j == last)
    def _():
        for hh in range(HD):
            o_ref[:, hh * DVD:(hh + 1) * DVD] = (acc_s[hh] / l_s[hh]).astype(o_ref.dtype)


def _mla(qd, kv_up, kpe, prev_out, nseq, sq, l_valid, l_pad, causal, tq, q_off, kv_off, name):
    kblk = _pick(l_pad, (512, 256, 128))
    assert q_off % tq == 0 and kv_off % kblk == 0
    nq, nk = sq // tq, l_pad // kblk
    n_blocks = (lambda i: (i * tq + tq - 1) // kblk + 1) if causal else (lambda i: -(-l_valid // kblk))
    pairs = [(i, j) for i in range(nq) for j in range(n_blocks(i))]
    it = jnp.asarray([p[0] for p in pairs], I32)
    jt = jnp.asarray([p[1] for p in pairs], I32)
    qmap = lambda b, t, it, jt: (q_off // tq + b * nq + it[t], 0)
    kvmap = lambda b, t, it, jt: (kv_off // kblk + b * nk + jt[t], 0)
    kpemap = lambda b, t, it, jt: (b * nk + jt[t], 0)
    in_specs = [pl.BlockSpec((tq, HD * QD_SLOT), qmap),
                pl.BlockSpec((kblk, HD * (DNOPE + DVD)), kvmap), pl.BlockSpec((kblk, DROPE), kpemap)]
    args = [it, jt, qd, kv_up, kpe]
    aliases = {}
    if prev_out is not None:
        in_specs.append(pl.BlockSpec(memory_space=pl.ANY))
        aliases = {len(args): 0}
        args.append(prev_out)
    return pl.pallas_call(
        functools.partial(_mla_kernel, tq, kblk, l_valid, causal),
        grid_spec=pltpu.PrefetchScalarGridSpec(
            num_scalar_prefetch=2,
            grid=(nseq, len(pairs)),
            in_specs=in_specs,
            out_specs=pl.BlockSpec((tq, HD * DVD), qmap),
            scratch_shapes=[pltpu.VMEM((HD, tq, 1), F32), pltpu.VMEM((HD, tq, 1), F32),
                            pltpu.VMEM((HD, tq, DVD), F32)]),
        out_shape=jax.ShapeDtypeStruct((qd.shape[0], HD * DVD), BF16),
        input_output_aliases=aliases,
        compiler_params=_cparams(("arbitrary", "arbitrary")),
        name=name,
    )(*args)


def _proj_out_kernel(a1_ref, a2_ref, w1_ref, w2_ref, x_ref, gate_ref, o_ref):
    y = _dot(a1_ref[...], w1_ref[...]) + _dot(a2_ref[...], w2_ref[...])
    o_ref[...] = x_ref[...] + _gate_rows(y, gate_ref[...])


def _proj_out(a1, a2, w_out, pair, x, modc, layer):
    t, d = x.shape
    tm = _token_tile(t)
    rows = tm // CHUNK
    k1, k2 = a1.shape[1], a2.shape[1]
    assert k1 == k2
    w = _to_bf16(w_out, pair)
    return pl.pallas_call(
        _proj_out_kernel,
        grid=(t // tm,),
        in_specs=[pl.BlockSpec((tm, k1), lambda i: (i, 0)), pl.BlockSpec((tm, k2), lambda i: (i, 0)),
                  _resident((k1, d), lambda i: (0, 0)), _resident((k2, d), lambda i: (1, 0)),
                  pl.BlockSpec((tm, d), lambda i: (i, 0)),
                  pl.BlockSpec((None, rows, d), lambda i: (layer, i, 2))],
        out_specs=pl.BlockSpec((tm, d), lambda i: (i, 0)),
        out_shape=jax.ShapeDtypeStruct((t, d), F32),
        compiler_params=_cparams(("arbitrary",)),
        name="proj_out",
    )(a1, a2, w, w, x, modc)


def _ffn_kernel(final_norm, x_ref, g_ref, sc_ref, sh_ref, gate_ref, wg_ref, wu_ref, wd_ref, gf_ref,
                *rest):
    h_s, acc_s = rest[-2:]
    i = pl.program_id(0)
    f = pl.program_id(1)

    @pl.when(f == 0)
    def _():
        h_s[...] = _modulate(x_ref[...], g_ref[...], sc_ref[...], sh_ref[...]).astype(BF16)
        acc_s[...] = jnp.zeros(acc_s.shape, F32)

    h = h_s[...]
    a = _dot(h, wg_ref[...])
    u = _dot(h, wu_ref[...])
    acc_s[...] += _dot((a * jax.nn.sigmoid(a) * u).astype(BF16), wd_ref[...])

    @pl.when(f == pl.num_programs(1) - 1)
    def _():
        y = x_ref[...] + _gate_rows(acc_s[...], gate_ref[...])
        if final_norm is None:
            rest[0][...] = y
        else:
            y = y * lax.rsqrt(jnp.mean(y * y, axis=-1, keepdims=True) + EPS) * gf_ref[...]
            yp_ref, ys_ref = rest[:2]

            @pl.when(i < final_norm)
            def _():
                yp_ref[...] = y

            @pl.when(i >= final_norm)
            def _():
                ys_ref[...] = y


def _ffn(x, modc, layer, g, wg, wu, wd, g_final, prompt_rows=None):
    t, d = x.shape
    f = wg.shape[2]
    tm = _token_tile(t)
    tf = _pick(f, (512, 256, 128))
    rows = tm // CHUNK
    modspec = lambda comp: pl.BlockSpec((None, rows, d), lambda i, j: (layer, i, comp))
    if prompt_rows is None:
        final_norm = None
        out_specs = pl.BlockSpec((tm, d), lambda i, j: (i, 0))
        out_shape = jax.ShapeDtypeStruct((t, d), F32)
    else:
        assert prompt_rows % tm == 0
        final_norm = prompt_rows // tm
        out_specs = [pl.BlockSpec((tm, d), lambda i, j: (jnp.minimum(i, final_norm - 1), 0)),
                     pl.BlockSpec((tm, d), lambda i, j: (jnp.maximum(i - final_norm, 0), 0))]
        out_shape = [jax.ShapeDtypeStruct((prompt_rows, d), F32),
                     jax.ShapeDtypeStruct((t - prompt_rows, d), F32)]
    return pl.pallas_call(
        functools.partial(_ffn_kernel, final_norm),
        grid=(t // tm, f // tf),
        in_specs=[pl.BlockSpec((tm, d), lambda i, j: (i, 0)),
                  _resident((1, d), lambda i, j: (0, 0)),
                  modspec(4), modspec(3), modspec(5),
                  pl.BlockSpec((d, tf), lambda i, j: (0, j)),
                  pl.BlockSpec((d, tf), lambda i, j: (0, j)),
                  pl.BlockSpec((tf, d), lambda i, j: (j, 0)),
                  _resident((1, d), lambda i, j: (0, 0))],
        out_specs=out_specs,
        out_shape=out_shape,
        scratch_shapes=[pltpu.VMEM((tm, d), BF16), pltpu.VMEM((tm, d), F32)],
        compiler_params=_cparams(("arbitrary", "arbitrary")),
        name="ffn",
    )(x, g, modc, modc, modc, _to_bf16(wg, layer), _to_bf16(wu, layer), _to_bf16(wd, layer), g_final)


def _round_up(n, m):
    return (n + m - 1) // m * m


def kernel(x_prompt, x_sample, c_prompt, c_sample, cache_a_k, cache_a_v, state_b_c, state_b_n, state_b_m,
           cache_c_k, cache_c_v, cache_c_idx, cache_d_ckv, cache_d_kpe, w_ada, b_ada, g_norm_mix, g_norm_ffn,
           w_in_even, w_out_even, sinks_a, b_igate, b_fgate, g_mlstm, w_in_odd, w_out_odd, g_ckv, w_uk, w_uv,
           w_ffn_gate, w_ffn_up, w_ffn_down, g_final):
    nb, s, d = x_prompt.shape
    db, ds, _ = x_sample.shape
    assert ds == CHUNK and s % CHUNK == 0
    depth = w_ada.shape[0]
    past = cache_c_k.shape[2]
    tp, ts = nb * s, db * ds
    t = tp + ts
    tm = _token_tile(t)
    assert s % tm == 0 and tp % tm == 0
    seq_chunks = s // CHUNK
    npc = tp // CHUNK

    x = jnp.concatenate([x_prompt.reshape(tp, d), x_sample.reshape(ts, d)], axis=0)
    mod = _ada(jnp.concatenate([c_prompt, c_sample], axis=0), w_ada, b_ada)
    mod_p = jnp.broadcast_to(mod[:, :nb, None, :], (depth, nb, seq_chunks, 6 * d)).reshape(depth, npc, 6 * d)
    modc = jnp.concatenate([mod_p, mod[:, nb:]], axis=1)

    pos = jnp.concatenate([jnp.arange(s, dtype=I32), jnp.tile(past + jnp.arange(ds, dtype=I32), tm // ds)])
    tab_a = _rope_tables(pos, ROT_A, DA)
    tab_c = _rope_tables(pos, ROT_C, DC)
    tab_d = _rope_tables(pos, DROPE, DROPE)
    prompt_tiles, seq_tiles = tp // tm, s // tm
    tab_map = lambda i: jnp.where(i < prompt_tiles, i % seq_tiles, seq_tiles)

    w_ukv_all = jnp.concatenate([w_uk, w_uv], axis=2).astype(BF16)
    l_s = past + ds
    lp_s = _round_up(l_s, 512)

    def cat_past(cache, new, width):
        full = jnp.concatenate([cache.reshape(db, past, width), new.reshape(db, ds, width)], axis=1)
        full = jnp.pad(full, ((0, 0), (0, lp_s - l_s), (0, 0)))
        return full.astype(BF16).reshape(db * lp_s, width)

    even_p, even_s, odd_p, odd_s = [], [], [], []
    for l in range(depth):
        i = l // 2
        g_mix = g_norm_mix[l].reshape(1, d)
        if l % 2 == 0:
            gate_bias = _pad_cols(jnp.concatenate([b_igate[i], b_fgate[i]]).reshape(1, 2 * HB), LANES)
            qa, ka, va, qb, kb, vb, gates, og = _proj_in(
                _proj_even_kernel, "proj_even", x, modc, l, g_mix, _even_weight(w_in_even[i]), gate_bias,
                list(tab_a), tab_map,
                [(HA * DA, BF16), (KVA * DA, F32), (KVA * DA, F32), (HB * DKB, BF16), (HB * DKB, BF16),
                 (HB * DVB, BF16), (LANES, F32), (HB * DVB, BF16)])
            ka_s = ka[tp:].reshape(db, ds, KVA * DA)
            va_s = va[tp:].reshape(db, ds, KVA * DA)
            win_k = cache_a_k[i].reshape(db, WINDOW, KVA * DA)
            win_v = cache_a_v[i].reshape(db, WINDOW, KVA * DA)
            kband = jnp.concatenate([win_k, ka_s], axis=1)
            vband = jnp.concatenate([win_v, va_s], axis=1)
            oa = _swa(qa, ka, va, win_k.reshape(db * WINDOW, KVA * DA), win_v.reshape(db * WINDOW, KVA * DA),
                      sinks_a[i], tp, s)
            c0 = jnp.concatenate([jnp.zeros((nb, HB, DKB, DVB), F32), state_b_c[i]], axis=0)
            n0 = jnp.concatenate([jnp.zeros((nb, HB, DKB), F32), state_b_n[i]], axis=0)
            m0 = jnp.concatenate([jnp.zeros((nb, HB), F32), state_b_m[i]], axis=0).reshape(nb + db, 1, HB)
            hb, c_new, n_new, m_new = _mlstm(qb, kb, vb, gates, og, g_mlstm[i].reshape(1, HB * DVB),
                                             c0, n0, m0, npc, seq_chunks, nb)
            m_new = m_new.reshape(nb + db, HB)
            ka_p = ka[:tp].reshape(nb, s, KVA, DA)
            va_p = va[:tp].reshape(nb, s, KVA, DA)
            even_p.append((ka_p[:, -WINDOW:], va_p[:, -WINDOW:], c_new[:nb], n_new[:nb], m_new[:nb]))
            even_s.append((kband[:, -WINDOW:].reshape(db, WINDOW, KVA, DA),
                           vband[:, -WINDOW:].reshape(db, WINDOW, KVA, DA), c_new[nb:], n_new[nb:], m_new[nb:]))
            x = _proj_out(oa, hb, w_out_even, i, x, modc, l)
        else:
            qc, kc, vc, qi, ki, wi, qd, ckv, kpe, kcb, vcb, kib, ckvb, kpeb = _proj_in(
                _proj_odd_kernel, "proj_odd", x, modc, l, g_mix, _odd_weight(w_in_odd[i]),
                g_ckv[i].reshape(1, KV_LORA), list(tab_a) + list(tab_c) + list(tab_d), tab_map,
                [(HC * DC, BF16), (KVC * DC, F32), (KVC * DC, F32), (HI * DI, BF16), (DI, F32), (LANES, F32),
                 (HD * QD_SLOT, BF16), (KV_LORA, F32), (DROPE, F32),
                 (KVC * DC, BF16), (KVC * DC, BF16), (DI, BF16), (KV_LORA, BF16), (DROPE, BF16)])
            oc = _dsa(qc, qi, wi, kcb, vcb, kib, None, nb, s, s, s, True, 128, 0, "dsa_prompt")
            oc = _dsa(qc, qi, wi, cat_past(cache_c_k[i], kcb[tp:], KVC * DC),
                      cat_past(cache_c_v[i], vcb[tp:], KVC * DC), cat_past(cache_c_idx[i], kib[tp:], DI),
                      oc, db, ds, l_s, lp_s, False, ds, tp, "dsa_sample")
            k_cat, v_t = _latent_up_t(ckvb, kpeb, w_uk[i].astype(BF16), w_uv[i].T.astype(BF16), tp)
            od = _mla_t(qd, k_cat, v_t, nb, s, _pick(s, (1024, 512, 256, 128)))
            kv_up = _latent_up(cat_past(cache_d_ckv[i], ckvb[tp:], KV_LORA), w_ukv_all[i])
            od = _mla(qd, kv_up, cat_past(cache_d_kpe[i], kpeb[tp:], DROPE), od,
                      db, ds, l_s, lp_s, False, ds, tp, 0, "mla_sample")
            odd_p.append((kc[:tp].reshape(nb, s, KVC, DC), vc[:tp].reshape(nb, s, KVC, DC),
                          ki[:tp].reshape(nb, s, DI), ckv[:tp].reshape(nb, s, KV_LORA),
                          kpe[:tp].reshape(nb, s, DROPE)))
            odd_s.append((kc[tp:].reshape(db, ds, KVC, DC), vc[tp:].reshape(db, ds, KVC, DC),
                          ki[tp:].reshape(db, ds, DI), ckv[tp:].reshape(db, ds, KV_LORA),
                          kpe[tp:].reshape(db, ds, DROPE)))
            x = _proj_out(oc, od, w_out_odd, i, x, modc, l)
        x = _ffn(x, modc, l, g_norm_ffn[l].reshape(1, d), w_ffn_gate, w_ffn_up, w_ffn_down,
                 g_final.reshape(1, d), tp if l == depth - 1 else None)

    y_prompt = x[0].reshape(nb, s, d)
    y_sample = x[1].reshape(db, ds, d)
    st_p = [jnp.stack([e[j] for e in even_p]) for j in range(5)] + [jnp.stack([o[j] for o in odd_p]) for j in range(5)]
    st_s = [jnp.stack([e[j] for e in even_s]) for j in range(5)] + [jnp.stack([o[j] for o in odd_s]) for j in range(5)]
    return (y_prompt, y_sample, *st_p, *st_s)
```

```python
import functools
import math

import jax
import jax.numpy as jnp
import numpy as np
from jax import lax
from jax.experimental import pallas as pl
from jax.experimental.pallas import tpu as pltpu

F32 = jnp.float32
BF16 = jnp.bfloat16
I32 = jnp.int32

CHUNK = 64
ROPE_THETA = 500000.0
EPS = 1e-6
NEG = -1e30
LANES = 128

HA, KVA, DA = 16, 2, 64
ROT_A = DA // 4
WINDOW = 128
WIN_CHUNKS = WINDOW // CHUNK
HB, DKB, DVB = 4, 128, 256
HC, KVC, DC = 8, 2, 128
ROT_C = DC // 4
HI, DI = 16, 64
ROT_I = DI // 4
INDEX_TOPK = 256
HD, DNOPE, DROPE, DVD, KV_LORA = 8, 128, 64, 128, 512
LOG2E = math.log2(math.e)
DSA_SCALE = DC ** -0.5 * LOG2E
MLA_SCALE = (DNOPE + DROPE) ** -0.5 * LOG2E
QD_SLOT = 2 * LANES

VMEM_LIMIT = 56 * 1024 * 1024
SPLIT_OUT_VMEM_LIMIT = 62 * 1024 * 1024

def _cparams(sem, vmem=VMEM_LIMIT):
    return pltpu.CompilerParams(dimension_semantics=sem, vmem_limit_bytes=vmem)


def _pick(n, options):
    for o in options:
        if n % o == 0:
            return o
    raise ValueError(f"no tile in {options} divides {n}")


def _resident(shape, index_map):
    return pl.BlockSpec(shape, index_map, pipeline_mode=pl.Buffered(1))


def _dot(a, b):
    return jnp.dot(a, b, preferred_element_type=F32)


def _dot_nt(a, b):
    return lax.dot_general(a, b, (((1,), (1,)), ((), ())), preferred_element_type=F32)


def _dot_tn(a, b):
    return lax.dot_general(a, b, (((0,), (0,)), ((), ())), preferred_element_type=F32)


def _modulate(x, g, sc, sh):
    tm, d = x.shape
    y = x * lax.rsqrt(jnp.mean(x * x, axis=-1, keepdims=True) + EPS) * g
    y = y.reshape(tm // CHUNK, CHUNK, d) * (1.0 + sc[:, None, :]) + sh[:, None, :]
    return y.reshape(tm, d)


def _gate_rows(y, g):
    tm, d = y.shape
    return (y.reshape(tm // CHUNK, CHUNK, d) * g[:, None, :]).reshape(tm, d)


def _rope(x, cos, sp, sm, half):
    blocks = []
    for j in range(x.shape[1] // LANES):
        xb = x[:, j * LANES:(j + 1) * LANES]
        blocks.append(xb * cos + pltpu.roll(xb, half, 1) * sp + pltpu.roll(xb, LANES - half, 1) * sm)
    return blocks[0] if len(blocks) == 1 else jnp.concatenate(blocks, axis=1)


def _rope_tables(pos, rot, head):
    half = rot // 2
    inv = ROPE_THETA ** (-jnp.arange(half, dtype=F32) * 2.0 / rot)
    ang = pos.astype(F32)[:, None] * inv[None, :]
    cos, sin = jnp.cos(ang), jnp.sin(ang)
    p = pos.shape[0]
    one = jnp.ones((p, head - rot), F32)
    zero = jnp.zeros((p, head - rot), F32)
    zh = jnp.zeros((p, half), F32)
    c = jnp.concatenate([cos, cos, one], axis=1)
    s_plus = jnp.concatenate([zh, sin, zero], axis=1)
    s_minus = jnp.concatenate([-sin, zh, zero], axis=1)
    rep = LANES // head
    return tuple(jnp.tile(t, (1, rep)) for t in (c, s_plus, s_minus))


CAST_BLOCK_BYTES = 6 * 1024 * 1024


def _cast_kernel(w_ref, o_ref):
    o_ref[...] = w_ref[...].astype(o_ref.dtype)


def _to_bf16(w, index):
    _, r, c = w.shape
    tr = next(t for t in (2048, 1024, 512, 256, 128, 64, 32, 16) if r % t == 0 and t * c * 4 <= CAST_BLOCK_BYTES)
    return pl.pallas_call(
        _cast_kernel,
        grid=(r // tr,),
        in_specs=[pl.BlockSpec((None, tr, c), lambda i: (index, i, 0))],
        out_specs=pl.BlockSpec((tr, c), lambda i: (i, 0)),
        out_shape=jax.ShapeDtypeStruct((r, c), BF16),
        compiler_params=_cparams(("arbitrary",)),
        name="to_bf16",
    )(w)


def _ada_kernel(c_ref, w_ref, b_ref, o_ref):
    c = c_ref[...]
    h = (c * jax.nn.sigmoid(c)).astype(BF16)
    o_ref[...] = _dot(h, w_ref[...].astype(BF16)) + b_ref[...]


def _ada(c, w_ada, b_ada):
    depth, d, n = w_ada.shape
    ns = c.shape[0]
    tn = _pick(n, (1024, 512, 256, 128))
    return pl.pallas_call(
        _ada_kernel,
        grid=(depth, n // tn),
        in_specs=[pl.BlockSpec((ns, d), lambda l, j: (0, 0)),
                  pl.BlockSpec((None, d, tn), lambda l, j: (l, 0, j)),
                  pl.BlockSpec((None, 1, tn), lambda l, j: (l, 0, j))],
        out_specs=pl.BlockSpec((None, ns, tn), lambda l, j: (l, 0, j)),
        out_shape=jax.ShapeDtypeStruct((depth, ns, n), F32),
        compiler_params=_cparams(("arbitrary", "arbitrary")),
        name="ada",
    )(c, w_ada, b_ada.reshape(depth, 1, n))


def _col_tiles(lo, hi, step=512):
    return [(a, min(a + step, hi)) for a in range(lo, hi, step)]


E_QA, E_KA, E_VA, E_QB, E_KB, E_VB, E_GT, E_OB, E_END = 0, 1024, 1152, 1280, 1792, 2304, 3328, 3456, 4480


def _pad_cols(w, width):
    return jnp.pad(w, ((0, 0), (0, width - w.shape[1])))


def _even_weight(w):
    qa, ka, va, qb, kb, vb, ib, fb, ob = jnp.split(
        w, np.cumsum((HA * DA, KVA * DA, KVA * DA, HB * DKB, HB * DKB, HB * DVB, HB, HB))[:].tolist(), axis=1)
    gates = _pad_cols(jnp.concatenate([ib, fb], axis=1), LANES)
    return jnp.concatenate([qa, ka, va, qb, kb, vb, gates, ob], axis=1).astype(BF16)


def _proj_even_kernel(x_ref, g_ref, sc_ref, sh_ref, w_ref, gb_ref, cos_ref, sp_ref, sm_ref,
                      qa_o, ka_o, va_o, qb_o, kb_o, vb_o, gt_o, og_o):
    h = _modulate(x_ref[...], g_ref[...], sc_ref[...], sh_ref[...]).astype(BF16)
    cos, sp, sm = cos_ref[...], sp_ref[...], sm_ref[...]
    half = ROT_A // 2

    def mm(lo, hi):
        return _dot(h, w_ref[:, lo:hi])

    for lo, hi in _col_tiles(E_QA, E_KA):
        qa_o[:, lo - E_QA:hi - E_QA] = _rope(mm(lo, hi), cos, sp, sm, half).astype(qa_o.dtype)
    ka_o[...] = _rope(mm(E_KA, E_VA), cos, sp, sm, half)
    va_o[...] = mm(E_VA, E_QB)
    for lo, hi in _col_tiles(E_QB, E_KB):
        qb_o[:, lo - E_QB:hi - E_QB] = (mm(lo, hi) * (DKB ** -0.5)).astype(qb_o.dtype)
    for lo, hi in _col_tiles(E_KB, E_VB):
        kb_o[:, lo - E_KB:hi - E_KB] = mm(lo, hi).astype(kb_o.dtype)
    for lo, hi in _col_tiles(E_VB, E_GT):
        vb_o[:, lo - E_VB:hi - E_VB] = mm(lo, hi).astype(vb_o.dtype)
    t = mm(E_GT, E_OB) + gb_ref[...]
    lane = lax.broadcasted_iota(I32, t.shape, 1)
    gt_o[...] = jnp.where(lane < HB, t, jax.nn.log_sigmoid(t))
    for lo, hi in _col_tiles(E_OB, E_END):
        og_o[:, lo - E_OB:hi - E_OB] = jax.nn.sigmoid(mm(lo, hi)).astype(og_o.dtype)


(O_QC, O_KC, O_VC, O_QI, O_KI, O_WI, O_QDN, O_QDP, O_CKV, O_KPE, O_END) = (
    0, 1024, 1280, 1536, 2560, 2688, 2816, 3840, 4352, 4864, 4992)


def _odd_weight(w):
    qc, kc, vc, qi, ki, wi, qd, ckv, kpe = jnp.split(
        w, np.cumsum((HC * DC, KVC * DC, KVC * DC, HI * DI, DI, HI, HD * (DNOPE + DROPE), KV_LORA)).tolist(), axis=1)
    d = w.shape[0]
    qd = qd.reshape(d, HD, DNOPE + DROPE)
    qdn = qd[:, :, :DNOPE].reshape(d, HD * DNOPE)
    qdp = qd[:, :, DNOPE:].reshape(d, HD * DROPE)
    return jnp.concatenate([qc, kc, vc, qi, _pad_cols(ki, LANES), _pad_cols(wi, LANES), qdn, qdp, ckv,
                            _pad_cols(kpe, LANES)], axis=1).astype(BF16)


def _proj_odd_kernel(x_ref, g_ref, sc_ref, sh_ref, w_ref, gckv_ref,
                     ci_ref, spi_ref, smi_ref, cc_ref, spc_ref, smc_ref, cd_ref, spd_ref, smd_ref,
                     qc_o, kc_o, vc_o, qi_o, ki_o, wi_o, qd_o, ckv_o, kpe_o,
                     kcb_o, vcb_o, kib_o, ckvb_o, kpeb_o):
    h = _modulate(x_ref[...], g_ref[...], sc_ref[...], sh_ref[...]).astype(BF16)
    tab_i = (ci_ref[...], spi_ref[...], smi_ref[...], ROT_I // 2)
    tab_c = (cc_ref[...], spc_ref[...], smc_ref[...], ROT_C // 2)
    tab_d = (cd_ref[...], spd_ref[...], smd_ref[...], DROPE // 2)

    def mm(lo, hi):
        return _dot(h, w_ref[:, lo:hi])

    for lo, hi in _col_tiles(O_QC, O_KC):
        qc_o[:, lo - O_QC:hi - O_QC] = (_rope(mm(lo, hi), *tab_c) * DSA_SCALE).astype(qc_o.dtype)
    kc = _rope(mm(O_KC, O_VC), *tab_c)
    kc_o[...] = kc
    kcb_o[...] = kc.astype(BF16)
    vc = mm(O_VC, O_QI)
    vc_o[...] = vc
    vcb_o[...] = vc.astype(BF16)
    for lo, hi in _col_tiles(O_QI, O_KI):
        qi_o[:, lo - O_QI:hi - O_QI] = _rope(mm(lo, hi), *tab_i).astype(qi_o.dtype)
    ki = _rope(mm(O_KI, O_WI), *tab_i)[:, :DI]
    ki_o[...] = ki
    kib_o[...] = ki.astype(BF16)
    wi_o[...] = mm(O_WI, O_QDN) * (HI ** -0.5 * DI ** -0.5)
    for lo, hi in _col_tiles(O_QDN, O_QDP):
        qn = (mm(lo, hi) * MLA_SCALE).astype(qd_o.dtype)
        for k in range((hi - lo) // DNOPE):
            hh = (lo - O_QDN) // DNOPE + k
            qd_o[:, hh * QD_SLOT:hh * QD_SLOT + DNOPE] = qn[:, k * DNOPE:(k + 1) * DNOPE]
    qp = _rope(mm(O_QDP, O_CKV), *tab_d) * MLA_SCALE
    low_lanes = lax.broadcasted_iota(I32, (qp.shape[0], LANES), 1) < DROPE
    for hh in range(HD):
        pair = qp[:, (hh // 2) * LANES:(hh // 2 + 1) * LANES]
        if hh % 2:
            pair = pltpu.roll(pair, DROPE, 1)
        qd_o[:, hh * QD_SLOT + DNOPE:(hh + 1) * QD_SLOT] = jnp.where(low_lanes, pair, 0.0).astype(qd_o.dtype)
    c = mm(O_CKV, O_KPE)
    c = c * lax.rsqrt(jnp.mean(c * c, axis=-1, keepdims=True) + EPS) * gckv_ref[...]
    ckv_o[...] = c
    ckvb_o[...] = c.astype(BF16)
    kpe = _rope(mm(O_KPE, O_END), *tab_d)[:, :DROPE]
    kpe_o[...] = kpe
    kpeb_o[...] = kpe.astype(BF16)


def _token_tile(t):
    return _pick(t, (512,))


def _proj_in(kernel_fn, name, x, modc, layer, g, w, extra, tables, tab_map, outs):
    t, d = x.shape
    tm = _token_tile(t)
    rows = tm // CHUNK
    in_specs = [pl.BlockSpec((tm, d), lambda i: (i, 0)),
                _resident((1, d), lambda i: (0, 0)),
                pl.BlockSpec((None, rows, d), lambda i: (layer, i, 1)),
                pl.BlockSpec((None, rows, d), lambda i: (layer, i, 0)),
                _resident(w.shape, lambda i: (0, 0)),
                _resident(extra.shape, lambda i: (0, 0))]
    in_specs += [pl.BlockSpec((tm, LANES), lambda i: (tab_map(i), 0)) for _ in tables]
    return pl.pallas_call(
        kernel_fn,
        grid=(t // tm,),
        in_specs=in_specs,
        out_specs=[pl.BlockSpec((tm, wd), lambda i: (i, 0)) for wd, _ in outs],
        out_shape=[jax.ShapeDtypeStruct((t, wd), dt) for wd, dt in outs],
        compiler_params=_cparams(("arbitrary",)),
        name=name,
    )(x, g, modc, modc, w, extra, *tables)


def _swa_kernel(tq, tiles_per_seq, sink_ref, q_ref, kw_ref, kc_ref, vw_ref, vc_ref, *rest):
    o_ref = rest[-1]
    kb = jnp.concatenate([kw_ref[...], kc_ref[...]], axis=0).astype(BF16)
    vb = jnp.concatenate([vw_ref[...], vc_ref[...]], axis=0).astype(BF16)
    qchunk = lax.broadcasted_iota(I32, (tq, WINDOW + tq), 0) // CHUNK
    kchunk = lax.broadcasted_iota(I32, (tq, WINDOW + tq), 1) // CHUNK
    valid = jnp.logical_and(kchunk >= qchunk, kchunk <= qchunk + WIN_CHUNKS)
    if tiles_per_seq:
        has_window = pl.program_id(0) % tiles_per_seq != 0
        valid = jnp.logical_and(valid, jnp.logical_or(kchunk >= WIN_CHUNKS, has_window))
    bias = jnp.where(valid, 0.0, NEG)
    group = HA // KVA
    for hq in range(HA):
        kv = hq // group
        q = q_ref[:, hq * DA:(hq + 1) * DA]
        s = _dot_nt(q, kb[:, kv * DA:(kv + 1) * DA]) * (DA ** -0.5) + bias
        sink = sink_ref[hq]
        m = jnp.maximum(jnp.max(s, axis=-1, keepdims=True), sink)
        p = jnp.exp(s - m)
        den = jnp.sum(p, axis=-1, keepdims=True) + jnp.exp(sink - m)
        o = _dot(p.astype(BF16), vb[:, kv * DA:(kv + 1) * DA]) / den
        o_ref[:, hq * DA:(hq + 1) * DA] = o.astype(o_ref.dtype)


def _swa(qa, ka, va, win_k, win_v, sinks, tp, s):
    t = qa.shape[0]
    tq = _pick(s, (WINDOW,))
    wpt = tq // WINDOW
    qspec = lambda rows, off: pl.BlockSpec((rows, HA * DA), lambda g: (g + off, 0))
    kvspec = lambda rows, off: pl.BlockSpec((rows, KVA * DA), lambda g: (g + off, 0))
    wspec_p = pl.BlockSpec((WINDOW, KVA * DA), lambda g: (jnp.maximum(g * wpt - 1, 0), 0))
    smem = pl.BlockSpec(memory_space=pltpu.SMEM)
    out_shape = jax.ShapeDtypeStruct((t, HA * DA), BF16)
    oa = pl.pallas_call(
        functools.partial(_swa_kernel, tq, s // tq),
        grid=(tp // tq,),
        in_specs=[smem, qspec(tq, 0), wspec_p, kvspec(tq, 0), wspec_p, kvspec(tq, 0)],
        out_specs=qspec(tq, 0),
        out_shape=out_shape,
        compiler_params=_cparams(("arbitrary",)),
        name="swa_prompt",
    )(sinks, qa, ka, ka, va, va)
    off = tp // CHUNK
    return pl.pallas_call(
        functools.partial(_swa_kernel, CHUNK, 0),
        grid=((t - tp) // CHUNK,),
        in_specs=[smem, qspec(CHUNK, off), kvspec(WINDOW, 0), kvspec(CHUNK, off), kvspec(WINDOW, 0),
                  kvspec(CHUNK, off), pl.BlockSpec(memory_space=pl.ANY)],
        out_specs=qspec(CHUNK, off),
        out_shape=out_shape,
        input_output_aliases={6: 0},
        compiler_params=_cparams(("arbitrary",)),
        name="swa_sample",
    )(sinks, qa, win_k, ka, win_v, va, oa)


def _mlstm_kernel(n_prompt_chunks, seq_chunks, q_ref, k_ref, v_ref, gt_ref, og_ref, gmh_ref,
                  c0_ref, n0_ref, m0_ref, h_o, c_o, n_o, m_o, c_s, n_s, m_s):
    g = pl.program_id(0)
    is_prompt = g < n_prompt_chunks
    c_idx = g % seq_chunks
    first = jnp.logical_or(jnp.logical_not(is_prompt), c_idx == 0)
    last = jnp.logical_or(jnp.logical_not(is_prompt), c_idx == seq_chunks - 1)

    @pl.when(first)
    def _():
        c_s[...] = c0_ref[...]
        n_s[...] = n0_ref[...]
        m_s[...] = m0_ref[...]

    gt = gt_ref[...]
    gt_t = gt.T
    row = lax.broadcasted_iota(I32, (CHUNK, CHUNK), 0)
    col = lax.broadcasted_iota(I32, (CHUNK, CHUNK), 1)
    causal = col <= row
    for hh in range(HB):
        ig_row = gt_t[hh:hh + 1, :]
        lf_row = gt_t[HB + hh:HB + hh + 1, :]
        ig_col = gt[:, hh:hh + 1]
        lf_col = gt[:, HB + hh:HB + hh + 1]
        b_col = jnp.sum(jnp.where(causal, lf_row, 0.0), axis=1, keepdims=True)
        b_row = jnp.sum(jnp.where(row <= col, lf_col, 0.0), axis=0, keepdims=True)
        m_prev = m_s[:, hh:hh + 1]
        d = jnp.where(causal, b_col - b_row + ig_row, NEG)
        inter = b_col + m_prev
        m_t = jnp.maximum(inter, jnp.max(d, axis=1, keepdims=True))
        w_intra = jnp.exp(d - m_t)
        w_inter = jnp.exp(inter - m_t)
        q = q_ref[:, hh * DKB:(hh + 1) * DKB]
        k = k_ref[:, hh * DKB:(hh + 1) * DKB]
        v = v_ref[:, hh * DVB:(hh + 1) * DVB]
        c_prev = c_s[hh]
        n_prev = n_s[hh:hh + 1, :]
        a = w_intra * _dot_nt(q, k)
        num = _dot(a.astype(BF16), v) + w_inter * _dot(q, c_prev.astype(BF16))
        den = (jnp.sum(a, axis=1, keepdims=True)
               + w_inter * jnp.sum(q.astype(F32) * n_prev, axis=1, keepdims=True))
        hv = num / jnp.maximum(jnp.abs(den), jnp.exp(-m_t))
        b_end = b_col[CHUNK - 1:CHUNK, :]
        g_row = b_end - b_row + ig_row
        g_col = b_end - b_col + ig_col
        m_new = jnp.maximum(b_end + m_prev, jnp.max(g_row, axis=1, keepdims=True))
        w_s = jnp.exp(g_col - m_new)
        w_c = jnp.exp(b_end + m_prev - m_new)
        c_s[hh] = w_c * c_prev + _dot_tn(k, (w_s * v.astype(F32)).astype(BF16))
        n_s[hh:hh + 1, :] = w_c * n_prev + jnp.sum(w_s * k.astype(F32), axis=0, keepdims=True)
        m_s[:, hh:hh + 1] = m_new
        y = hv * lax.rsqrt(jnp.mean(hv * hv, axis=-1, keepdims=True) + EPS)
        y = y * gmh_ref[:, hh * DVB:(hh + 1) * DVB] * og_ref[:, hh * DVB:(hh + 1) * DVB].astype(F32)
        h_o[:, hh * DVB:(hh + 1) * DVB] = y.astype(h_o.dtype)

    @pl.when(last)
    def _():
        c_o[...] = c_s[...]
        n_o[...] = n_s[...]
        m_o[...] = m_s[...]


def _mlstm(qb, kb, vb, gates, og, g_mh, c0, n0, m0, n_prompt_chunks, seq_chunks, n_prompt_seqs):
    t = qb.shape[0]
    nch = t // CHUNK
    ns = c0.shape[0]

    def seq(g):
        return jnp.where(g < n_prompt_chunks, g // seq_chunks, n_prompt_seqs + g - n_prompt_chunks)

    tok = lambda w: pl.BlockSpec((CHUNK, w), lambda g: (g, 0))
    st_specs = [pl.BlockSpec((None, HB, DKB, DVB), lambda g: (seq(g), 0, 0, 0)),
                pl.BlockSpec((None, HB, DKB), lambda g: (seq(g), 0, 0)),
                pl.BlockSpec((None, 1, HB), lambda g: (seq(g), 0, 0))]
    return pl.pallas_call(
        functools.partial(_mlstm_kernel, n_prompt_chunks, seq_chunks),
        grid=(nch,),
        in_specs=[tok(HB * DKB), tok(HB * DKB), tok(HB * DVB), tok(LANES), tok(HB * DVB),
                  _resident((1, HB * DVB), lambda g: (0, 0))] + st_specs,
        out_specs=[tok(HB * DVB)] + st_specs,
        out_shape=[jax.ShapeDtypeStruct((t, HB * DVB), BF16),
                   jax.ShapeDtypeStruct((ns, HB, DKB, DVB), F32),
                   jax.ShapeDtypeStruct((ns, HB, DKB), F32),
                   jax.ShapeDtypeStruct((ns, 1, HB), F32)],
        scratch_shapes=[pltpu.VMEM((HB, DKB, DVB), F32), pltpu.VMEM((HB, DKB), F32),
                        pltpu.VMEM((1, HB), F32)],
        compiler_params=_cparams(("arbitrary",)),
        name="mlstm",
    )(qb, kb, vb, gates, og, g_mh, c0, n0, m0)


INT_MIN = -2 ** 31


def _order_key(x):
    b = pltpu.bitcast(x, I32)
    return jnp.where(b >= 0, b, b ^ jnp.int32(0x7FFFFFFF))


def _dsa_kernel(tq, kblk, l_valid, l_pad, topk, causal,
                qc_ref, qi_ref, wi_ref, k_ref, v_ref, ki_ref, *rest):
    o_ref, key_s = rest[-2:]
    i = pl.program_id(1)
    q0 = i * tq
    rowq = lax.broadcasted_iota(I32, (tq, 1), 0)
    if causal:
        lim = ((q0 + rowq) // CHUNK + 1) * CHUNK
        nkb = (q0 + tq + kblk - 1) // kblk
    else:
        lim = jnp.full((tq, 1), l_valid, I32)
        nkb = l_pad // kblk
    lane = lax.broadcasted_iota(I32, (tq, kblk), 1)

    wi = wi_ref[...]

    def score_block(kb, carry):
        start = pl.multiple_of(kb * kblk, kblk)
        kib = ki_ref[pl.ds(start, kblk), :]
        acc = jnp.zeros((tq, kblk), F32)
        for hh in range(HI):
            s = _dot_nt(qi_ref[:, hh * DI:(hh + 1) * DI], kib)
            acc = acc + wi[:, hh:hh + 1] * jnp.maximum(s, 0.0)
        acc = acc + 0.0
        acc = jnp.where(start + lane < lim, acc, NEG)
        key_s[kb] = _order_key(acc)
        return carry

    lax.fori_loop(0, nkb, score_block, 0)

    lane1 = lax.broadcasted_iota(I32, (tq, LANES), 1)

    def count(pred_fn):
        def blk(kb, part):
            start = kb * kblk
            for c in range(kblk // LANES):
                keys = key_s[kb, :, c * LANES:(c + 1) * LANES]
                part = part + pred_fn(keys, start + c * LANES + lane1).astype(I32)
            return part
        part = lax.fori_loop(0, nkb, blk, jnp.zeros((tq, LANES), I32))
        return jnp.sum(part, axis=1, keepdims=True)

    def thr_bit(carry):
        b, thr, n_ge, _ = carry
        cand = thr + lax.shift_left(jnp.int32(1), 31 - b)
        cand_b = jnp.broadcast_to(cand, (tq, LANES))
        cnt = count(lambda keys, idx: keys >= cand_b)
        keep = cnt >= topk
        n_ge = jnp.where(keep, cnt, n_ge)
        return b + 1, jnp.where(keep, cand, thr), n_ge, jnp.max(n_ge)

    _, thr, n_ge, _ = lax.while_loop(
        lambda c: jnp.logical_and(c[0] < 32, c[3] > topk), thr_bit,
        (jnp.int32(0), jnp.full((tq, 1), INT_MIN, I32), jnp.full((tq, 1), l_pad, I32), jnp.int32(l_pad)))

    nbits = max(1, (l_pad - 1).bit_length())

    def tie_cut():
        thr_b = jnp.broadcast_to(thr, (tq, LANES))
        need = topk - count(lambda keys, idx: keys > thr_b)

        def cut_bit(b, cut):
            cand = cut + lax.shift_left(jnp.int32(1), nbits - 1 - b)
            cand_b = jnp.broadcast_to(cand, (tq, LANES))
            cnt = count(lambda keys, idx: jnp.logical_and(keys == thr_b, idx < cand_b))
            return jnp.where(cnt < need, cand, cut)

        return lax.fori_loop(0, nbits, cut_bit, jnp.zeros((tq, 1), I32))

    cut = lax.cond(jnp.max(n_ge) > topk, tie_cut, lambda: jnp.full((tq, 1), l_pad, I32))

    group = HC // KVC
    qs = [jnp.concatenate([qc_ref[:, (kv * group + j) * DC:(kv * group + j + 1) * DC]
                           for j in range(group)], axis=0) for kv in range(KVC)]

    def attend_block(kb, carry):
        start = pl.multiple_of(kb * kblk, kblk)
        keys = key_s[kb]
        idx = start + lane
        sel = jnp.logical_or(keys > thr, jnp.logical_and(keys == thr, idx <= cut))
        sel = jnp.logical_and(sel, idx < lim)
        bias = jnp.where(sel, 0.0, NEG)
        new = []
        for kv in range(KVC):
            m, l, acc = carry[kv]
            kk = k_ref[pl.ds(start, kblk), kv * DC:(kv + 1) * DC]
            vv = v_ref[pl.ds(start, kblk), kv * DC:(kv + 1) * DC]
            s = _dot_nt(qs[kv], kk)
            s = jnp.concatenate([s[j * tq:(j + 1) * tq] + bias for j in range(group)], axis=0)
            m_new = jnp.maximum(m, jnp.max(s, axis=1, keepdims=True))
            alpha = jnp.exp2(m - m_new)
            p = jnp.exp2(s - m_new)
            l = alpha * l + jnp.sum(p, axis=1, keepdims=True)
            acc = alpha * acc + _dot(p.astype(BF16), vv)
            new.append((m_new, l, acc))
        return tuple(new)

    init = tuple((jnp.full((group * tq, 1), NEG, F32), jnp.zeros((group * tq, 1), F32),
                  jnp.zeros((group * tq, DC), F32)) for _ in range(KVC))
    res = lax.fori_loop(0, nkb, attend_block, init)
    for kv in range(KVC):
        _, l, acc = res[kv]
        o = acc / l
        for j in range(group):
            hq = kv * group + j
            o_ref[:, hq * DC:(hq + 1) * DC] = o[j * tq:(j + 1) * tq, :].astype(o_ref.dtype)


def _dsa(qc, qi, wi, k, v, ki, prev_out, nseq, sq, l_valid, l_pad, causal, tq, q_off, name):
    topk = min(INDEX_TOPK, l_valid // 4)
    kblk = _pick(l_pad, (512, 256, 128))
    assert kblk >= topk and sq % tq == 0 and q_off % tq == 0
    nq = sq // tq
    qmap = lambda b, i: (q_off // tq + b * nq + i, 0)
    kmap = lambda b, i: (b, 0)
    in_specs = [pl.BlockSpec((tq, HC * DC), qmap), pl.BlockSpec((tq, HI * DI), qmap),
                pl.BlockSpec((tq, LANES), qmap),
                _resident((l_pad, KVC * DC), kmap), _resident((l_pad, KVC * DC), kmap),
                _resident((l_pad, DI), kmap)]
    args = [qc, qi, wi, k, v, ki]
    aliases = {}
    if prev_out is not None:
        in_specs.append(pl.BlockSpec(memory_space=pl.ANY))
        aliases = {len(args): 0}
        args.append(prev_out)
    return pl.pallas_call(
        functools.partial(_dsa_kernel, tq, kblk, l_valid, l_pad, topk, causal),
        grid=(nseq, nq),
        in_specs=in_specs,
        out_specs=pl.BlockSpec((tq, HC * DC), qmap),
        out_shape=jax.ShapeDtypeStruct((qc.shape[0], HC * DC), BF16),
        scratch_shapes=[pltpu.VMEM((l_pad // kblk, tq, kblk), I32)],
        input_output_aliases=aliases,
        compiler_params=_cparams(("arbitrary", "arbitrary")),
        name=name,
    )(*args)


def _matmul_kernel(a_ref, w_ref, o_ref):
    o_ref[...] = _dot(a_ref[...], w_ref[...]).astype(o_ref.dtype)


def _latent_up(ckv, w_ukv):
    r = ckv.shape[0]
    tm = _pick(r, (1024, 512, 256, 128, 64))
    n = w_ukv.shape[1]
    return pl.pallas_call(
        _matmul_kernel,
        grid=(r // tm,),
        in_specs=[pl.BlockSpec((tm, KV_LORA), lambda i: (i, 0)), _resident(w_ukv.shape, lambda i: (0, 0))],
        out_specs=pl.BlockSpec((tm, n), lambda i: (i, 0)),
        out_shape=jax.ShapeDtypeStruct((r, n), BF16),
        compiler_params=_cparams(("arbitrary",)),
        name="latent_up",
    )(ckv, w_ukv)


def _mla_last_block(i, tq, kblk):
    return (i * tq + tq - 1) // kblk


def _latent_up_t_kernel(ckv_ref, kpe_ref, wk_ref, wvt_ref, k_o, vt_o):
    ckv = ckv_ref[...]
    kn = _dot(ckv, wk_ref[...]).astype(k_o.dtype)
    kpe = jnp.concatenate([kpe_ref[...], jnp.zeros((kpe_ref.shape[0], LANES - DROPE), k_o.dtype)], axis=1)
    for hh in range(HD):
        k_o[:, hh * QD_SLOT:hh * QD_SLOT + DNOPE] = kn[:, hh * DNOPE:(hh + 1) * DNOPE]
        k_o[:, hh * QD_SLOT + DNOPE:(hh + 1) * QD_SLOT] = kpe
    vt_o[...] = _dot_nt(wvt_ref[...], ckv).astype(vt_o.dtype)


def _latent_up_t(ckv, kpe, w_uk, w_uv_t, rows):
    tm = _pick(rows, (512, 256, 128))
    return pl.pallas_call(
        _latent_up_t_kernel,
        grid=(rows // tm,),
        in_specs=[pl.BlockSpec((tm, KV_LORA), lambda i: (i, 0)), pl.BlockSpec((tm, DROPE), lambda i: (i, 0)),
                  _resident(w_uk.shape, lambda i: (0, 0)), _resident(w_uv_t.shape, lambda i: (0, 0))],
        out_specs=[pl.BlockSpec((tm, HD * QD_SLOT), lambda i: (i, 0)),
                   pl.BlockSpec((HD * DVD, tm), lambda i: (0, i))],
        out_shape=[jax.ShapeDtypeStruct((rows, HD * QD_SLOT), BF16),
                   jax.ShapeDtypeStruct((HD * DVD, rows), BF16)],
        compiler_params=_cparams(("arbitrary",)),
        name="latent_up_t",
    )(ckv, kpe, w_uk, w_uv_t)


def _mla_t_kernel(tq, kblk, it_ref, jt_ref, q_ref, k_ref, vt_ref, o_ref, m_s, l_s, acc_s):
    step_id = pl.program_id(1)
    i = it_ref[step_id]
    j = jt_ref[step_id]
    last = _mla_last_block(i, tq, kblk)

    @pl.when(j == 0)
    def _():
        m_s[...] = jnp.full(m_s.shape, NEG, F32)
        l_s[...] = jnp.zeros(l_s.shape, F32)
        acc_s[...] = jnp.zeros(acc_s.shape, F32)

    def step(masked):
        if masked:
            kidx = j * kblk + lax.broadcasted_iota(I32, (kblk, tq), 0)
            qpos = i * tq + lax.broadcasted_iota(I32, (kblk, tq), 1)
            bias = jnp.where(kidx < (qpos // CHUNK + 1) * CHUNK, 0.0, NEG)
        for hh in range(HD):
            s = _dot_nt(k_ref[:, hh * QD_SLOT:(hh + 1) * QD_SLOT], q_ref[:, hh * QD_SLOT:(hh + 1) * QD_SLOT])
            if masked:
                s = s + bias
            m = m_s[hh]
            m_new = jnp.maximum(m, jnp.max(s, axis=0, keepdims=True))
            alpha = jnp.exp2(m - m_new)
            p = jnp.exp2(s - m_new)
            l_s[hh] = alpha * l_s[hh] + jnp.sum(p, axis=0, keepdims=True)
            acc_s[hh] = alpha * acc_s[hh] + _dot(vt_ref[hh * DVD:(hh + 1) * DVD, :], p.astype(BF16))
            m_s[hh] = m_new

    n_full = (i * tq + CHUNK) // kblk
    pl.when(j < n_full)(functools.partial(step, False))
    pl.when(j >= n_full)(functools.partial(step, True))

    @pl.when(j == last)
    def _():
        for hh in range(HD):
            o_ref[:, hh * DVD:(hh + 1) * DVD] = (acc_s[hh] / l_s[hh]).T.astype(o_ref.dtype)


def _mla_t(qd, k_cat, v_t, nseq, sq, tq):
    kblk = _pick(sq, (512, 256, 128))
    nq, nk = sq // tq, sq // kblk
    pairs = [(i, j) for i in range(nq) for j in range(_mla_last_block(i, tq, kblk) + 1)]
    it = jnp.asarray([p[0] for p in pairs], I32)
    jt = jnp.asarray([p[1] for p in pairs], I32)
    qmap = lambda b, t, it, jt: (b * nq + it[t], 0)
    return pl.pallas_call(
        functools.partial(_mla_t_kernel, tq, kblk),
        grid_spec=pltpu.PrefetchScalarGridSpec(
            num_scalar_prefetch=2,
            grid=(nseq, len(pairs)),
            in_specs=[pl.BlockSpec((tq, HD * QD_SLOT), qmap),
                      pl.BlockSpec((kblk, HD * QD_SLOT), lambda b, t, it, jt: (b * nk + jt[t], 0)),
                      pl.BlockSpec((HD * DVD, kblk), lambda b, t, it, jt: (0, b * nk + jt[t]))],
            out_specs=pl.BlockSpec((tq, HD * DVD), qmap),
            scratch_shapes=[pltpu.VMEM((HD, 1, tq), F32), pltpu.VMEM((HD, 1, tq), F32),
                            pltpu.VMEM((HD, DVD, tq), F32)]),
        out_shape=jax.ShapeDtypeStruct((qd.shape[0], HD * DVD), BF16),
        compiler_params=_cparams(("arbitrary", "arbitrary")),
        name="mla_prompt",
    )(it, jt, qd, k_cat, v_t)


def _mla_kernel(tq, kblk, l_valid, causal, it_ref, jt_ref, q_ref, kv_ref, kpe_ref, *rest):
    o_ref, m_s, l_s, acc_s = rest[-4:]
    step_id = pl.program_id(1)
    i = it_ref[step_id]
    j = jt_ref[step_id]
    last = _mla_last_block(i, tq, kblk) if causal else pl.cdiv(l_valid, kblk) - 1

    @pl.when(j == 0)
    def _():
        m_s[...] = jnp.full(m_s.shape, NEG, F32)
        l_s[...] = jnp.zeros(l_s.shape, F32)
        acc_s[...] = jnp.zeros(acc_s.shape, F32)

    def step(masked):
        if masked:
            idx = j * kblk + lax.broadcasted_iota(I32, (tq, kblk), 1)
            if causal:
                rowq = i * tq + lax.broadcasted_iota(I32, (tq, kblk), 0)
                valid = idx < (rowq // CHUNK + 1) * CHUNK
            else:
                valid = idx < l_valid
            bias = jnp.where(valid, 0.0, NEG)
        kpe = kpe_ref[...]
        for hh in range(HD):
            kn = kv_ref[:, hh * DNOPE:(hh + 1) * DNOPE]
            vv = kv_ref[:, HD * DNOPE + hh * DVD:HD * DNOPE + (hh + 1) * DVD]
            s = (_dot_nt(q_ref[:, hh * QD_SLOT:hh * QD_SLOT + DNOPE], kn)
                 + _dot_nt(q_ref[:, hh * QD_SLOT + DNOPE:hh * QD_SLOT + DNOPE + DROPE], kpe))
            if masked:
                s = s + bias
            m = m_s[hh]
            m_new = jnp.maximum(m, jnp.max(s, axis=1, keepdims=True))
            alpha = jnp.exp2(m - m_new)
            p = jnp.exp2(s - m_new)
            l_s[hh] = alpha * l_s[hh] + jnp.sum(p, axis=1, keepdims=True)
            acc_s[hh] = alpha * acc_s[hh] + _dot(p.astype(BF16), vv)
            m_s[hh] = m_new

    if causal:
        n_full = (i * tq + CHUNK) // kblk
    else:
        n_full = l_valid // kblk
    pl.when(j < n_full)(functools.partial(step, False))
    pl.when(j >= n_full)(functools.partial(step, True))

    @pl.when(j == last)
    def _():
        for hh in range(HD):
            o_ref[:, hh * DVD:(hh + 1) * DVD] = (acc_s[hh] / l_s[hh]).astype(o_ref.dtype)


def _mla(qd, kv_up, kpe, prev_out, nseq, sq, l_valid, l_pad, causal, tq, q_off, kv_off, name):
    kblk = _pick(l_pad, (512, 256, 128))
    assert q_off % tq == 0 and kv_off % kblk == 0
    nq, nk = sq // tq, l_pad // kblk
    n_blocks = (lambda i: (i * tq + tq - 1) // kblk + 1) if causal else (lambda i: -(-l_valid // kblk))
    pairs = [(i, j) for i in range(nq) for j in range(n_blocks(i))]
    it = jnp.asarray([p[0] for p in pairs], I32)
    jt = jnp.asarray([p[1] for p in pairs], I32)
    qmap = lambda b, t, it, jt: (q_off // tq + b * nq + it[t], 0)
    kvmap = lambda b, t, it, jt: (kv_off // kblk + b * nk + jt[t], 0)
    kpemap = lambda b, t, it, jt: (b * nk + jt[t], 0)
    in_specs = [pl.BlockSpec((tq, HD * QD_SLOT), qmap),
                pl.BlockSpec((kblk, HD * (DNOPE + DVD)), kvmap), pl.BlockSpec((kblk, DROPE), kpemap)]
    args = [it, jt, qd, kv_up, kpe]
    aliases = {}
    if prev_out is not None:
        in_specs.append(pl.BlockSpec(memory_space=pl.ANY))
        aliases = {len(args): 0}
        args.append(prev_out)
    return pl.pallas_call(
        functools.partial(_mla_kernel, tq, kblk, l_valid, causal),
        grid_spec=pltpu.PrefetchScalarGridSpec(
            num_scalar_prefetch=2,
            grid=(nseq, len(pairs)),
            in_specs=in_specs,
            out_specs=pl.BlockSpec((tq, HD * DVD), qmap),
            scratch_shapes=[pltpu.VMEM((HD, tq, 1), F32), pltpu.VMEM((HD, tq, 1), F32),
                            pltpu.VMEM((HD, tq, DVD), F32)]),
        out_shape=jax.ShapeDtypeStruct((qd.shape[0], HD * DVD), BF16),
        input_output_aliases=aliases,
        compiler_params=_cparams(("arbitrary", "arbitrary")),
        name=name,
    )(*args)


def _mix_ffn_kernel(final_norm, a1_ref, a2_ref, w1_ref, w2_ref, x_ref, gate1_ref,
                    g_ref, sc_ref, sh_ref, gate2_ref, wg_ref, wu_ref, wd_ref, gf_ref, *rest):
    h_s, acc_s = rest[-2:]
    i = pl.program_id(0)
    f = pl.program_id(1)

    def on_own_output(fn):
        if final_norm is None:
            fn(rest[0])
        else:
            pl.when(i < final_norm)(functools.partial(fn, rest[0]))
            pl.when(i >= final_norm)(functools.partial(fn, rest[1]))

    @pl.when(f == 0)
    def _():
        y = _dot(a1_ref[...], w1_ref[...]) + _dot(a2_ref[...], w2_ref[...])
        x1 = x_ref[...] + _gate_rows(y, gate1_ref[...])
        h_s[...] = _modulate(x1, g_ref[...], sc_ref[...], sh_ref[...]).astype(BF16)
        acc_s[...] = jnp.zeros(acc_s.shape, F32)

        def keep(o_ref):
            o_ref[...] = x1

        on_own_output(keep)

    h = h_s[...]
    a = _dot(h, wg_ref[...])
    u = _dot(h, wu_ref[...])
    acc_s[...] += _dot((a * jax.nn.sigmoid(a) * u).astype(BF16), wd_ref[...])

    @pl.when(f == pl.num_programs(1) - 1)
    def _():
        def finish(o_ref):
            y = o_ref[...] + _gate_rows(acc_s[...], gate2_ref[...])
            if final_norm is not None:
                y = y * lax.rsqrt(jnp.mean(y * y, axis=-1, keepdims=True) + EPS) * gf_ref[...]
            o_ref[...] = y

        on_own_output(finish)


def _mix_ffn(a1, a2, w_out, pair, x, modc, layer, g, wg, wu, wd, g_final, prompt_rows=None):
    t, d = x.shape
    f = wg.shape[2]
    tm = _token_tile(t)
    tf = _pick(f, (512, 256, 128))
    rows = tm // CHUNK
    k1, k2 = a1.shape[1], a2.shape[1]
    assert k1 == k2
    w_mix = _to_bf16(w_out, pair)
    modspec = lambda comp: pl.BlockSpec((None, rows, d), lambda i, j: (layer, i, comp))
    if prompt_rows is None:
        final_norm = None
        out_specs = pl.BlockSpec((tm, d), lambda i, j: (i, 0))
        out_shape = jax.ShapeDtypeStruct((t, d), F32)
    else:
        assert prompt_rows % tm == 0
        final_norm = prompt_rows // tm
        out_specs = [pl.BlockSpec((tm, d), lambda i, j: (jnp.minimum(i, final_norm - 1), 0)),
                     pl.BlockSpec((tm, d), lambda i, j: (jnp.maximum(i - final_norm, 0), 0))]
        out_shape = [jax.ShapeDtypeStruct((prompt_rows, d), F32),
                     jax.ShapeDtypeStruct((t - prompt_rows, d), F32)]
    return pl.pallas_call(
        functools.partial(_mix_ffn_kernel, final_norm),
        grid=(t // tm, f // tf),
        in_specs=[pl.BlockSpec((tm, k1), lambda i, j: (i, 0)), pl.BlockSpec((tm, k2), lambda i, j: (i, 0)),
                  _resident((k1, d), lambda i, j: (0, 0)), _resident((k2, d), lambda i, j: (1, 0)),
                  pl.BlockSpec((tm, d), lambda i, j: (i, 0)),
                  modspec(2),
                  _resident((1, d), lambda i, j: (0, 0)),
                  modspec(4), modspec(3), modspec(5),
                  pl.BlockSpec((d, tf), lambda i, j: (0, j)),
                  pl.BlockSpec((d, tf), lambda i, j: (0, j)),
                  pl.BlockSpec((tf, d), lambda i, j: (j, 0)),
                  _resident((1, d), lambda i, j: (0, 0))],
        out_specs=out_specs,
        out_shape=out_shape,
        scratch_shapes=[pltpu.VMEM((tm, d), BF16), pltpu.VMEM((tm, d), F32)],
        compiler_params=_cparams(("arbitrary", "arbitrary"),
                                 VMEM_LIMIT if prompt_rows is None else SPLIT_OUT_VMEM_LIMIT),
        name="mix_ffn",
    )(a1, a2, w_mix, w_mix, x, modc, g, modc, modc, modc,
      _to_bf16(wg, layer), _to_bf16(wu, layer), _to_bf16(wd, layer), g_final)


def _round_up(n, m):
    return (n + m - 1) // m * m


def kernel(x_prompt, x_sample, c_prompt, c_sample, cache_a_k, cache_a_v, state_b_c, state_b_n, state_b_m,
           cache_c_k, cache_c_v, cache_c_idx, cache_d_ckv, cache_d_kpe, w_ada, b_ada, g_norm_mix, g_norm_ffn,
           w_in_even, w_out_even, sinks_a, b_igate, b_fgate, g_mlstm, w_in_odd, w_out_odd, g_ckv, w_uk, w_uv,
           w_ffn_gate, w_ffn_up, w_ffn_down, g_final):
    nb, s, d = x_prompt.shape
    db, ds, _ = x_sample.shape
    assert ds == CHUNK and s % CHUNK == 0
    depth = w_ada.shape[0]
    past = cache_c_k.shape[2]
    tp, ts = nb * s, db * ds
    t = tp + ts
    tm = _token_tile(t)
    assert s % tm == 0 and tp % tm == 0
    seq_chunks = s // CHUNK
    npc = tp // CHUNK

    x = jnp.concatenate([x_prompt.reshape(tp, d), x_sample.reshape(ts, d)], axis=0)
    mod = _ada(jnp.concatenate([c_prompt, c_sample], axis=0), w_ada, b_ada)
    mod_p = jnp.broadcast_to(mod[:, :nb, None, :], (depth, nb, seq_chunks, 6 * d)).reshape(depth, npc, 6 * d)
    modc = jnp.concatenate([mod_p, mod[:, nb:]], axis=1)

    pos = jnp.concatenate([jnp.arange(s, dtype=I32), jnp.tile(past + jnp.arange(ds, dtype=I32), tm // ds)])
    tab_a = _rope_tables(pos, ROT_A, DA)
    tab_c = _rope_tables(pos, ROT_C, DC)
    tab_d = _rope_tables(pos, DROPE, DROPE)
    prompt_tiles, seq_tiles = tp // tm, s // tm
    tab_map = lambda i: jnp.where(i < prompt_tiles, i % seq_tiles, seq_tiles)

    w_ukv_all = jnp.concatenate([w_uk, w_uv], axis=2).astype(BF16)
    l_s = past + ds
    lp_s = _round_up(l_s, 512)

    def cat_past(cache, new, width):
        full = jnp.concatenate([cache.reshape(db, past, width), new.reshape(db, ds, width)], axis=1)
        full = jnp.pad(full, ((0, 0), (0, lp_s - l_s), (0, 0)))
        return full.astype(BF16).reshape(db * lp_s, width)

    even_p, even_s, odd_p, odd_s = [], [], [], []
    for l in range(depth):
        i = l // 2
        g_mix = g_norm_mix[l].reshape(1, d)
        if l % 2 == 0:
            gate_bias = _pad_cols(jnp.concatenate([b_igate[i], b_fgate[i]]).reshape(1, 2 * HB), LANES)
            qa, ka, va, qb, kb, vb, gates, og = _proj_in(
                _proj_even_kernel, "proj_even", x, modc, l, g_mix, _even_weight(w_in_even[i]), gate_bias,
                list(tab_a), tab_map,
                [(HA * DA, BF16), (KVA * DA, F32), (KVA * DA, F32), (HB * DKB, BF16), (HB * DKB, BF16),
                 (HB * DVB, BF16), (LANES, F32), (HB * DVB, BF16)])
            ka_s = ka[tp:].reshape(db, ds, KVA * DA)
            va_s = va[tp:].reshape(db, ds, KVA * DA)
            win_k = cache_a_k[i].reshape(db, WINDOW, KVA * DA)
            win_v = cache_a_v[i].reshape(db, WINDOW, KVA * DA)
            kband = jnp.concatenate([win_k, ka_s], axis=1)
            vband = jnp.concatenate([win_v, va_s], axis=1)
            oa = _swa(qa, ka, va, win_k.reshape(db * WINDOW, KVA * DA), win_v.reshape(db * WINDOW, KVA * DA),
                      sinks_a[i], tp, s)
            c0 = jnp.concatenate([jnp.zeros((nb, HB, DKB, DVB), F32), state_b_c[i]], axis=0)
            n0 = jnp.concatenate([jnp.zeros((nb, HB, DKB), F32), state_b_n[i]], axis=0)
            m0 = jnp.concatenate([jnp.zeros((nb, HB), F32), state_b_m[i]], axis=0).reshape(nb + db, 1, HB)
            hb, c_new, n_new, m_new = _mlstm(qb, kb, vb, gates, og, g_mlstm[i].reshape(1, HB * DVB),
                                             c0, n0, m0, npc, seq_chunks, nb)
            m_new = m_new.reshape(nb + db, HB)
            ka_p = ka[:tp].reshape(nb, s, KVA, DA)
            va_p = va[:tp].reshape(nb, s, KVA, DA)
            even_p.append((ka_p[:, -WINDOW:], va_p[:, -WINDOW:], c_new[:nb], n_new[:nb], m_new[:nb]))
            even_s.append((kband[:, -WINDOW:].reshape(db, WINDOW, KVA, DA),
                           vband[:, -WINDOW:].reshape(db, WINDOW, KVA, DA), c_new[nb:], n_new[nb:], m_new[nb:]))
            mix = (oa, hb, w_out_even)
        else:
            qc, kc, vc, qi, ki, wi, qd, ckv, kpe, kcb, vcb, kib, ckvb, kpeb = _proj_in(
                _proj_odd_kernel, "proj_odd", x, modc, l, g_mix, _odd_weight(w_in_odd[i]),
                g_ckv[i].reshape(1, KV_LORA), list(tab_a) + list(tab_c) + list(tab_d), tab_map,
                [(HC * DC, BF16), (KVC * DC, F32), (KVC * DC, F32), (HI * DI, BF16), (DI, F32), (LANES, F32),
                 (HD * QD_SLOT, BF16), (KV_LORA, F32), (DROPE, F32),
                 (KVC * DC, BF16), (KVC * DC, BF16), (DI, BF16), (KV_LORA, BF16), (DROPE, BF16)])
            oc = _dsa(qc, qi, wi, kcb, vcb, kib, None, nb, s, s, s, True, 128, 0, "dsa_prompt")
            oc = _dsa(qc, qi, wi, cat_past(cache_c_k[i], kcb[tp:], KVC * DC),
                      cat_past(cache_c_v[i], vcb[tp:], KVC * DC), cat_past(cache_c_idx[i], kib[tp:], DI),
                      oc, db, ds, l_s, lp_s, False, ds, tp, "dsa_sample")
            k_cat, v_t = _latent_up_t(ckvb, kpeb, w_uk[i].astype(BF16), w_uv[i].T.astype(BF16), tp)
            od = _mla_t(qd, k_cat, v_t, nb, s, _pick(s, (1024, 512, 256, 128)))
            kv_up = _latent_up(cat_past(cache_d_ckv[i], ckvb[tp:], KV_LORA), w_ukv_all[i])
            od = _mla(qd, kv_up, cat_past(cache_d_kpe[i], kpeb[tp:], DROPE), od,
                      db, ds, l_s, lp_s, False, ds, tp, 0, "mla_sample")
            odd_p.append((kc[:tp].reshape(nb, s, KVC, DC), vc[:tp].reshape(nb, s, KVC, DC),
                          ki[:tp].reshape(nb, s, DI), ckv[:tp].reshape(nb, s, KV_LORA),
                          kpe[:tp].reshape(nb, s, DROPE)))
            odd_s.append((kc[tp:].reshape(db, ds, KVC, DC), vc[tp:].reshape(db, ds, KVC, DC),
                          ki[tp:].reshape(db, ds, DI), ckv[tp:].reshape(db, ds, KV_LORA),
                          kpe[tp:].reshape(db, ds, DROPE)))
            mix = (oc, od, w_out_odd)
        x = _mix_ffn(*mix, i, x, modc, l, g_norm_ffn[l].reshape(1, d), w_ffn_gate, w_ffn_up, w_ffn_down,
                     g_final.reshape(1, d), tp if l == depth - 1 else None)

    y_prompt = x[0].reshape(nb, s, d)
    y_sample = x[1].reshape(db, ds, d)
    st_p = [jnp.stack([e[j] for e in even_p]) for j in range(5)] + [jnp.stack([o[j] for o in odd_p]) for j in range(5)]
    st_s = [jnp.stack([e[j] for e in even_s]) for j in range(5)] + [jnp.stack([o[j] for o in odd_s]) for j in range(5)]
    return (y_prompt, y_sample, *st_p, *st_s)
```

```python
import functools
import math

import jax
import jax.numpy as jnp
import numpy as np
from jax import lax
from jax.experimental import pallas as pl
from jax.experimental.pallas import tpu as pltpu

F32 = jnp.float32
BF16 = jnp.bfloat16
I32 = jnp.int32

CHUNK = 64
ROPE_THETA = 500000.0
EPS = 1e-6
NEG = -1e30
LANES = 128

HA, KVA, DA = 16, 2, 64
ROT_A = DA // 4
WINDOW = 128
WIN_CHUNKS = WINDOW // CHUNK
HB, DKB, DVB = 4, 128, 256
HC, KVC, DC = 8, 2, 128
ROT_C = DC // 4
HI, DI = 16, 64
ROT_I = DI // 4
INDEX_TOPK = 256
HD, DNOPE, DROPE, DVD, KV_LORA = 8, 128, 64, 128, 512
LOG2E = math.log2(math.e)
DSA_SCALE = DC ** -0.5 * LOG2E
MLA_SCALE = (DNOPE + DROPE) ** -0.5 * LOG2E
QD_SLOT = 2 * LANES

VMEM_LIMIT = 56 * 1024 * 1024
SPLIT_OUT_VMEM_LIMIT = 62 * 1024 * 1024

def _cparams(sem, vmem=VMEM_LIMIT):
    return pltpu.CompilerParams(dimension_semantics=sem, vmem_limit_bytes=vmem)


def _pick(n, options):
    for o in options:
        if n % o == 0:
            return o
    raise ValueError(f"no tile in {options} divides {n}")


def _resident(shape, index_map):
    return pl.BlockSpec(shape, index_map, pipeline_mode=pl.Buffered(1))


def _dot(a, b):
    return jnp.dot(a, b, preferred_element_type=F32)


def _dot_nt(a, b):
    return lax.dot_general(a, b, (((1,), (1,)), ((), ())), preferred_element_type=F32)


def _dot_tn(a, b):
    return lax.dot_general(a, b, (((0,), (0,)), ((), ())), preferred_element_type=F32)


def _modulate(x, g, sc, sh):
    tm, d = x.shape
    y = x * lax.rsqrt(jnp.mean(x * x, axis=-1, keepdims=True) + EPS) * g
    y = y.reshape(tm // CHUNK, CHUNK, d) * (1.0 + sc[:, None, :]) + sh[:, None, :]
    return y.reshape(tm, d)


def _gate_rows(y, g):
    tm, d = y.shape
    return (y.reshape(tm // CHUNK, CHUNK, d) * g[:, None, :]).reshape(tm, d)


def _rope(x, cos, sp, sm, half):
    blocks = []
    for j in range(x.shape[1] // LANES):
        xb = x[:, j * LANES:(j + 1) * LANES]
        blocks.append(xb * cos + pltpu.roll(xb, half, 1) * sp + pltpu.roll(xb, LANES - half, 1) * sm)
    return blocks[0] if len(blocks) == 1 else jnp.concatenate(blocks, axis=1)


def _rope_tables(pos, rot, head):
    half = rot // 2
    inv = ROPE_THETA ** (-jnp.arange(half, dtype=F32) * 2.0 / rot)
    ang = pos.astype(F32)[:, None] * inv[None, :]
    cos, sin = jnp.cos(ang), jnp.sin(ang)
    p = pos.shape[0]
    one = jnp.ones((p, head - rot), F32)
    zero = jnp.zeros((p, head - rot), F32)
    zh = jnp.zeros((p, half), F32)
    c = jnp.concatenate([cos, cos, one], axis=1)
    s_plus = jnp.concatenate([zh, sin, zero], axis=1)
    s_minus = jnp.concatenate([-sin, zh, zero], axis=1)
    rep = LANES // head
    return tuple(jnp.tile(t, (1, rep)) for t in (c, s_plus, s_minus))


CAST_BLOCK_BYTES = 6 * 1024 * 1024


def _cast_kernel(w_ref, o_ref):
    o_ref[...] = w_ref[...].astype(o_ref.dtype)


def _to_bf16(w, index):
    _, r, c = w.shape
    tr = next(t for t in (2048, 1024, 512, 256, 128, 64, 32, 16) if r % t == 0 and t * c * 4 <= CAST_BLOCK_BYTES)
    return pl.pallas_call(
        _cast_kernel,
        grid=(r // tr,),
        in_specs=[pl.BlockSpec((None, tr, c), lambda i: (index, i, 0))],
        out_specs=pl.BlockSpec((tr, c), lambda i: (i, 0)),
        out_shape=jax.ShapeDtypeStruct((r, c), BF16),
        compiler_params=_cparams(("arbitrary",)),
        name="to_bf16",
    )(w)


def _ada_kernel(c_ref, w_ref, b_ref, o_ref):
    c = c_ref[...]
    h = (c * jax.nn.sigmoid(c)).astype(BF16)
    o_ref[...] = _dot(h, w_ref[...].astype(BF16)) + b_ref[...]


def _ada(c, w_ada, b_ada):
    depth, d, n = w_ada.shape
    ns = c.shape[0]
    tn = _pick(n, (1024, 512, 256, 128))
    return pl.pallas_call(
        _ada_kernel,
        grid=(depth, n // tn),
        in_specs=[pl.BlockSpec((ns, d), lambda l, j: (0, 0)),
                  pl.BlockSpec((None, d, tn), lambda l, j: (l, 0, j)),
                  pl.BlockSpec((None, 1, tn), lambda l, j: (l, 0, j))],
        out_specs=pl.BlockSpec((None, ns, tn), lambda l, j: (l, 0, j)),
        out_shape=jax.ShapeDtypeStruct((depth, ns, n), F32),
        compiler_params=_cparams(("arbitrary", "arbitrary")),
        name="ada",
    )(c, w_ada, b_ada.reshape(depth, 1, n))


def _col_tiles(lo, hi, step=512):
    return [(a, min(a + step, hi)) for a in range(lo, hi, step)]


E_QA, E_KA, E_VA, E_QB, E_KB, E_VB, E_GT, E_OB, E_END = 0, 1024, 1152, 1280, 1792, 2304, 3328, 3456, 4480


def _pad_cols(w, width):
    return jnp.pad(w, ((0, 0), (0, width - w.shape[1])))


def _even_weight(w):
    qa, ka, va, qb, kb, vb, ib, fb, ob = jnp.split(
        w, np.cumsum((HA * DA, KVA * DA, KVA * DA, HB * DKB, HB * DKB, HB * DVB, HB, HB))[:].tolist(), axis=1)
    gates = _pad_cols(jnp.concatenate([ib, fb], axis=1), LANES)
    return jnp.concatenate([qa, ka, va, qb, kb, vb, gates, ob], axis=1).astype(BF16)


def _proj_even_kernel(x_ref, g_ref, sc_ref, sh_ref, w_ref, gb_ref, cos_ref, sp_ref, sm_ref,
                      qa_o, ka_o, va_o, qb_o, kb_o, vb_o, gt_o, og_o):
    h = _modulate(x_ref[...], g_ref[...], sc_ref[...], sh_ref[...]).astype(BF16)
    cos, sp, sm = cos_ref[...], sp_ref[...], sm_ref[...]
    half = ROT_A // 2

    def mm(lo, hi):
        return _dot(h, w_ref[:, lo:hi])

    for lo, hi in _col_tiles(E_QA, E_KA):
        qa_o[:, lo - E_QA:hi - E_QA] = _rope(mm(lo, hi), cos, sp, sm, half).astype(qa_o.dtype)
    ka_o[...] = _rope(mm(E_KA, E_VA), cos, sp, sm, half)
    va_o[...] = mm(E_VA, E_QB)
    for lo, hi in _col_tiles(E_QB, E_KB):
        qb_o[:, lo - E_QB:hi - E_QB] = (mm(lo, hi) * (DKB ** -0.5)).astype(qb_o.dtype)
    for lo, hi in _col_tiles(E_KB, E_VB):
        kb_o[:, lo - E_KB:hi - E_KB] = mm(lo, hi).astype(kb_o.dtype)
    for lo, hi in _col_tiles(E_VB, E_GT):
        vb_o[:, lo - E_VB:hi - E_VB] = mm(lo, hi).astype(vb_o.dtype)
    t = mm(E_GT, E_OB) + gb_ref[...]
    lane = lax.broadcasted_iota(I32, t.shape, 1)
    gt_o[...] = jnp.where(lane < HB, t, jax.nn.log_sigmoid(t))
    for lo, hi in _col_tiles(E_OB, E_END):
        og_o[:, lo - E_OB:hi - E_OB] = jax.nn.sigmoid(mm(lo, hi)).astype(og_o.dtype)


(O_QC, O_KC, O_VC, O_QI, O_KI, O_WI, O_QDN, O_QDP, O_CKV, O_KPE, O_END) = (
    0, 1024, 1280, 1536, 2560, 2688, 2816, 3840, 4352, 4864, 4992)


def _odd_weight(w):
    qc, kc, vc, qi, ki, wi, qd, ckv, kpe = jnp.split(
        w, np.cumsum((HC * DC, KVC * DC, KVC * DC, HI * DI, DI, HI, HD * (DNOPE + DROPE), KV_LORA)).tolist(), axis=1)
    d = w.shape[0]
    qd = qd.reshape(d, HD, DNOPE + DROPE)
    qdn = qd[:, :, :DNOPE].reshape(d, HD * DNOPE)
    qdp = qd[:, :, DNOPE:].reshape(d, HD * DROPE)
    return jnp.concatenate([qc, kc, vc, qi, _pad_cols(ki, LANES), _pad_cols(wi, LANES), qdn, qdp, ckv,
                            _pad_cols(kpe, LANES)], axis=1).astype(BF16)


def _proj_odd_kernel(x_ref, g_ref, sc_ref, sh_ref, w_ref, gckv_ref,
                     ci_ref, spi_ref, smi_ref, cc_ref, spc_ref, smc_ref, cd_ref, spd_ref, smd_ref,
                     qc_o, kc_o, vc_o, qi_o, ki_o, wi_o, qd_o, ckv_o, kpe_o,
                     kcb_o, vcb_o, kib_o, ckvb_o, kpeb_o):
    h = _modulate(x_ref[...], g_ref[...], sc_ref[...], sh_ref[...]).astype(BF16)
    tab_i = (ci_ref[...], spi_ref[...], smi_ref[...], ROT_I // 2)
    tab_c = (cc_ref[...], spc_ref[...], smc_ref[...], ROT_C // 2)
    tab_d = (cd_ref[...], spd_ref[...], smd_ref[...], DROPE // 2)

    def mm(lo, hi):
        return _dot(h, w_ref[:, lo:hi])

    for lo, hi in _col_tiles(O_QC, O_KC):
        qc_o[:, lo - O_QC:hi - O_QC] = (_rope(mm(lo, hi), *tab_c) * DSA_SCALE).astype(qc_o.dtype)
    kc = _rope(mm(O_KC, O_VC), *tab_c)
    kc_o[...] = kc
    kcb_o[...] = kc.astype(BF16)
    vc = mm(O_VC, O_QI)
    vc_o[...] = vc
    vcb_o[...] = vc.astype(BF16)
    for lo, hi in _col_tiles(O_QI, O_KI):
        qi_o[:, lo - O_QI:hi - O_QI] = _rope(mm(lo, hi), *tab_i).astype(qi_o.dtype)
    ki = _rope(mm(O_KI, O_WI), *tab_i)[:, :DI]
    ki_o[...] = ki
    kib_o[...] = ki.astype(BF16)
    wi_o[...] = mm(O_WI, O_QDN) * (HI ** -0.5 * DI ** -0.5)
    for lo, hi in _col_tiles(O_QDN, O_QDP):
        qn = (mm(lo, hi) * MLA_SCALE).astype(qd_o.dtype)
        for k in range((hi - lo) // DNOPE):
            hh = (lo - O_QDN) // DNOPE + k
            qd_o[:, hh * QD_SLOT:hh * QD_SLOT + DNOPE] = qn[:, k * DNOPE:(k + 1) * DNOPE]
    qp = _rope(mm(O_QDP, O_CKV), *tab_d) * MLA_SCALE
    low_lanes = lax.broadcasted_iota(I32, (qp.shape[0], LANES), 1) < DROPE
    for hh in range(HD):
        pair = qp[:, (hh // 2) * LANES:(hh // 2 + 1) * LANES]
        if hh % 2:
            pair = pltpu.roll(pair, DROPE, 1)
        qd_o[:, hh * QD_SLOT + DNOPE:(hh + 1) * QD_SLOT] = jnp.where(low_lanes, pair, 0.0).astype(qd_o.dtype)
    c = mm(O_CKV, O_KPE)
    c = c * lax.rsqrt(jnp.mean(c * c, axis=-1, keepdims=True) + EPS) * gckv_ref[...]
    ckv_o[...] = c
    ckvb_o[...] = c.astype(BF16)
    kpe = _rope(mm(O_KPE, O_END), *tab_d)[:, :DROPE]
    kpe_o[...] = kpe
    kpeb_o[...] = kpe.astype(BF16)


def _token_tile(t):
    return _pick(t, (512,))


def _proj_in(kernel_fn, name, x, modc, layer, g, w, extra, tables, tab_map, outs):
    t, d = x.shape
    tm = _token_tile(t)
    rows = tm // CHUNK
    in_specs = [pl.BlockSpec((tm, d), lambda i: (i, 0)),
                _resident((1, d), lambda i: (0, 0)),
                pl.BlockSpec((None, rows, d), lambda i: (layer, i, 1)),
                pl.BlockSpec((None, rows, d), lambda i: (layer, i, 0)),
                _resident(w.shape, lambda i: (0, 0)),
                _resident(extra.shape, lambda i: (0, 0))]
    in_specs += [pl.BlockSpec((tm, LANES), lambda i: (tab_map(i), 0)) for _ in tables]
    return pl.pallas_call(
        kernel_fn,
        grid=(t // tm,),
        in_specs=in_specs,
        out_specs=[pl.BlockSpec((tm, wd), lambda i: (i, 0)) for wd, _ in outs],
        out_shape=[jax.ShapeDtypeStruct((t, wd), dt) for wd, dt in outs],
        compiler_params=_cparams(("arbitrary",)),
        name=name,
    )(x, g, modc, modc, w, extra, *tables)


def _swa_kernel(tq, tiles_per_seq, sink_ref, q_ref, kw_ref, kc_ref, vw_ref, vc_ref, *rest):
    o_ref = rest[-1]
    kb = jnp.concatenate([kw_ref[...], kc_ref[...]], axis=0).astype(BF16)
    vb = jnp.concatenate([vw_ref[...], vc_ref[...]], axis=0).astype(BF16)
    qchunk = lax.broadcasted_iota(I32, (tq, WINDOW + tq), 0) // CHUNK
    kchunk = lax.broadcasted_iota(I32, (tq, WINDOW + tq), 1) // CHUNK
    valid = jnp.logical_and(kchunk >= qchunk, kchunk <= qchunk + WIN_CHUNKS)
    if tiles_per_seq:
        has_window = pl.program_id(0) % tiles_per_seq != 0
        valid = jnp.logical_and(valid, jnp.logical_or(kchunk >= WIN_CHUNKS, has_window))
    bias = jnp.where(valid, 0.0, NEG)
    group = HA // KVA
    for hq in range(HA):
        kv = hq // group
        q = q_ref[:, hq * DA:(hq + 1) * DA]
        s = _dot_nt(q, kb[:, kv * DA:(kv + 1) * DA]) * (DA ** -0.5) + bias
        sink = sink_ref[hq]
        m = jnp.maximum(jnp.max(s, axis=-1, keepdims=True), sink)
        p = jnp.exp(s - m)
        den = jnp.sum(p, axis=-1, keepdims=True) + jnp.exp(sink - m)
        o = _dot(p.astype(BF16), vb[:, kv * DA:(kv + 1) * DA]) / den
        o_ref[:, hq * DA:(hq + 1) * DA] = o.astype(o_ref.dtype)


def _swa(qa, ka, va, win_k, win_v, sinks, tp, s):
    t = qa.shape[0]
    tq = _pick(s, (WINDOW,))
    wpt = tq // WINDOW
    qspec = lambda rows, off: pl.BlockSpec((rows, HA * DA), lambda g: (g + off, 0))
    kvspec = lambda rows, off: pl.BlockSpec((rows, KVA * DA), lambda g: (g + off, 0))
    wspec_p = pl.BlockSpec((WINDOW, KVA * DA), lambda g: (jnp.maximum(g * wpt - 1, 0), 0))
    smem = pl.BlockSpec(memory_space=pltpu.SMEM)
    out_shape = jax.ShapeDtypeStruct((t, HA * DA), BF16)
    oa = pl.pallas_call(
        functools.partial(_swa_kernel, tq, s // tq),
        grid=(tp // tq,),
        in_specs=[smem, qspec(tq, 0), wspec_p, kvspec(tq, 0), wspec_p, kvspec(tq, 0)],
        out_specs=qspec(tq, 0),
        out_shape=out_shape,
        compiler_params=_cparams(("arbitrary",)),
        name="swa_prompt",
    )(sinks, qa, ka, ka, va, va)
    off = tp // CHUNK
    return pl.pallas_call(
        functools.partial(_swa_kernel, CHUNK, 0),
        grid=((t - tp) // CHUNK,),
        in_specs=[smem, qspec(CHUNK, off), kvspec(WINDOW, 0), kvspec(CHUNK, off), kvspec(WINDOW, 0),
                  kvspec(CHUNK, off), pl.BlockSpec(memory_space=pl.ANY)],
        out_specs=qspec(CHUNK, off),
        out_shape=out_shape,
        input_output_aliases={6: 0},
        compiler_params=_cparams(("arbitrary",)),
        name="swa_sample",
    )(sinks, qa, win_k, ka, win_v, va, oa)


def _mlstm_kernel(n_prompt_chunks, seq_chunks, q_ref, k_ref, v_ref, gt_ref, og_ref, gmh_ref,
                  c0_ref, n0_ref, m0_ref, h_o, c_o, n_o, m_o, c_s, n_s, m_s):
    g = pl.program_id(0)
    is_prompt = g < n_prompt_chunks
    c_idx = g % seq_chunks
    first = jnp.logical_or(jnp.logical_not(is_prompt), c_idx == 0)
    last = jnp.logical_or(jnp.logical_not(is_prompt), c_idx == seq_chunks - 1)

    @pl.when(first)
    def _():
        c_s[...] = c0_ref[...]
        n_s[...] = n0_ref[...]
        m_s[...] = m0_ref[...]

    gt = gt_ref[...]
    gt_t = gt.T
    row = lax.broadcasted_iota(I32, (CHUNK, CHUNK), 0)
    col = lax.broadcasted_iota(I32, (CHUNK, CHUNK), 1)
    causal = col <= row
    for hh in range(HB):
        ig_row = gt_t[hh:hh + 1, :]
        lf_row = gt_t[HB + hh:HB + hh + 1, :]
        ig_col = gt[:, hh:hh + 1]
        lf_col = gt[:, HB + hh:HB + hh + 1]
        b_col = jnp.sum(jnp.where(causal, lf_row, 0.0), axis=1, keepdims=True)
        b_row = jnp.sum(jnp.where(row <= col, lf_col, 0.0), axis=0, keepdims=True)
        m_prev = m_s[:, hh:hh + 1]
        d = jnp.where(causal, b_col - b_row + ig_row, NEG)
        inter = b_col + m_prev
        m_t = jnp.maximum(inter, jnp.max(d, axis=1, keepdims=True))
        w_intra = jnp.exp(d - m_t)
        w_inter = jnp.exp(inter - m_t)
        q = q_ref[:, hh * DKB:(hh + 1) * DKB]
        k = k_ref[:, hh * DKB:(hh + 1) * DKB]
        v = v_ref[:, hh * DVB:(hh + 1) * DVB]
        c_prev = c_s[hh]
        n_prev = n_s[hh:hh + 1, :]
        a = w_intra * _dot_nt(q, k)
        num = _dot(a.astype(BF16), v) + w_inter * _dot(q, c_prev.astype(BF16))
        den = (jnp.sum(a, axis=1, keepdims=True)
               + w_inter * jnp.sum(q.astype(F32) * n_prev, axis=1, keepdims=True))
        hv = num / jnp.maximum(jnp.abs(den), jnp.exp(-m_t))
        b_end = b_col[CHUNK - 1:CHUNK, :]
        g_row = b_end - b_row + ig_row
        g_col = b_end - b_col + ig_col
        m_new = jnp.maximum(b_end + m_prev, jnp.max(g_row, axis=1, keepdims=True))
        w_s = jnp.exp(g_col - m_new)
        w_c = jnp.exp(b_end + m_prev - m_new)
        c_s[hh] = w_c * c_prev + _dot_tn(k, (w_s * v.astype(F32)).astype(BF16))
        n_s[hh:hh + 1, :] = w_c * n_prev + jnp.sum(w_s * k.astype(F32), axis=0, keepdims=True)
        m_s[:, hh:hh + 1] = m_new
        y = hv * lax.rsqrt(jnp.mean(hv * hv, axis=-1, keepdims=True) + EPS)
        y = y * gmh_ref[:, hh * DVB:(hh + 1) * DVB] * og_ref[:, hh * DVB:(hh + 1) * DVB].astype(F32)
        h_o[:, hh * DVB:(hh + 1) * DVB] = y.astype(h_o.dtype)

    @pl.when(last)
    def _():
        c_o[...] = c_s[...]
        n_o[...] = n_s[...]
        m_o[...] = m_s[...]


def _mlstm(qb, kb, vb, gates, og, g_mh, c0, n0, m0, n_prompt_chunks, seq_chunks, n_prompt_seqs):
    t = qb.shape[0]
    nch = t // CHUNK
    ns = c0.shape[0]

    def seq(g):
        return jnp.where(g < n_prompt_chunks, g // seq_chunks, n_prompt_seqs + g - n_prompt_chunks)

    tok = lambda w: pl.BlockSpec((CHUNK, w), lambda g: (g, 0))
    st_specs = [pl.BlockSpec((None, HB, DKB, DVB), lambda g: (seq(g), 0, 0, 0)),
                pl.BlockSpec((None, HB, DKB), lambda g: (seq(g), 0, 0)),
                pl.BlockSpec((None, 1, HB), lambda g: (seq(g), 0, 0))]
    return pl.pallas_call(
        functools.partial(_mlstm_kernel, n_prompt_chunks, seq_chunks),
        grid=(nch,),
        in_specs=[tok(HB * DKB), tok(HB * DKB), tok(HB * DVB), tok(LANES), tok(HB * DVB),
                  _resident((1, HB * DVB), lambda g: (0, 0))] + st_specs,
        out_specs=[tok(HB * DVB)] + st_specs,
        out_shape=[jax.ShapeDtypeStruct((t, HB * DVB), BF16),
                   jax.ShapeDtypeStruct((ns, HB, DKB, DVB), F32),
                   jax.ShapeDtypeStruct((ns, HB, DKB), F32),
                   jax.ShapeDtypeStruct((ns, 1, HB), F32)],
        scratch_shapes=[pltpu.VMEM((HB, DKB, DVB), F32), pltpu.VMEM((HB, DKB), F32),
                        pltpu.VMEM((1, HB), F32)],
        compiler_params=_cparams(("arbitrary",)),
        name="mlstm",
    )(qb, kb, vb, gates, og, g_mh, c0, n0, m0)


INT_MIN = -2 ** 31


def _order_key(x):
    b = pltpu.bitcast(x, I32)
    return jnp.where(b >= 0, b, b ^ jnp.int32(0x7FFFFFFF))


def _dsa_kernel(tq, kblk, l_valid, l_pad, topk, causal,
                qc_ref, qi_ref, wi_ref, k_ref, v_ref, ki_ref, *rest):
    o_ref, key_s = rest[-2:]
    i = pl.program_id(1)
    q0 = i * tq
    rowq = lax.broadcasted_iota(I32, (tq, 1), 0)
    if causal:
        lim = ((q0 + rowq) // CHUNK + 1) * CHUNK
        nkb = (q0 + tq + kblk - 1) // kblk
    else:
        lim = jnp.full((tq, 1), l_valid, I32)
        nkb = l_pad // kblk
    lane = lax.broadcasted_iota(I32, (tq, kblk), 1)

    wi = wi_ref[...]

    def score_block(kb, carry):
        start = pl.multiple_of(kb * kblk, kblk)
        kib = ki_ref[pl.ds(start, kblk), :]
        acc = jnp.zeros((tq, kblk), F32)
        for hh in range(HI):
            s = _dot_nt(qi_ref[:, hh * DI:(hh + 1) * DI], kib)
            acc = acc + wi[:, hh:hh + 1] * jnp.maximum(s, 0.0)
        acc = acc + 0.0
        acc = jnp.where(start + lane < lim, acc, NEG)
        key_s[kb] = _order_key(acc)
        return carry

    lax.fori_loop(0, nkb, score_block, 0)

    lane1 = lax.broadcasted_iota(I32, (tq, LANES), 1)

    def count(pred_fn):
        def blk(kb, part):
            start = kb * kblk
            for c in range(kblk // LANES):
                keys = key_s[kb, :, c * LANES:(c + 1) * LANES]
                part = part + pred_fn(keys, start + c * LANES + lane1).astype(I32)
            return part
        part = lax.fori_loop(0, nkb, blk, jnp.zeros((tq, LANES), I32))
        return jnp.sum(part, axis=1, keepdims=True)

    def thr_bit(carry):
        b, thr, n_ge, _ = carry
        cand = thr + lax.shift_left(jnp.int32(1), 31 - b)
        cand_b = jnp.broadcast_to(cand, (tq, LANES))
        cnt = count(lambda keys, idx: keys >= cand_b)
        keep = cnt >= topk
        n_ge = jnp.where(keep, cnt, n_ge)
        return b + 1, jnp.where(keep, cand, thr), n_ge, jnp.max(n_ge)

    _, thr, n_ge, _ = lax.while_loop(
        lambda c: jnp.logical_and(c[0] < 32, c[3] > topk), thr_bit,
        (jnp.int32(0), jnp.full((tq, 1), INT_MIN, I32), jnp.full((tq, 1), l_pad, I32), jnp.int32(l_pad)))

    nbits = max(1, (l_pad - 1).bit_length())

    def tie_cut():
        thr_b = jnp.broadcast_to(thr, (tq, LANES))
        need = topk - count(lambda keys, idx: keys > thr_b)

        def cut_bit(b, cut):
            cand = cut + lax.shift_left(jnp.int32(1), nbits - 1 - b)
            cand_b = jnp.broadcast_to(cand, (tq, LANES))
            cnt = count(lambda keys, idx: jnp.logical_and(keys == thr_b, idx < cand_b))
            return jnp.where(cnt < need, cand, cut)

        return lax.fori_loop(0, nbits, cut_bit, jnp.zeros((tq, 1), I32))

    cut = lax.cond(jnp.max(n_ge) > topk, tie_cut, lambda: jnp.full((tq, 1), l_pad, I32))

    group = HC // KVC
    qs = [jnp.concatenate([qc_ref[:, (kv * group + j) * DC:(kv * group + j + 1) * DC]
                           for j in range(group)], axis=0) for kv in range(KVC)]

    def attend_block(kb, carry):
        start = pl.multiple_of(kb * kblk, kblk)
        keys = key_s[kb]
        idx = start + lane
        sel = jnp.logical_or(keys > thr, jnp.logical_and(keys == thr, idx <= cut))
        sel = jnp.logical_and(sel, idx < lim)
        bias = jnp.where(sel, 0.0, NEG)
        new = []
        for kv in range(KVC):
            m, l, acc = carry[kv]
            kk = k_ref[pl.ds(start, kblk), kv * DC:(kv + 1) * DC]
            vv = v_ref[pl.ds(start, kblk), kv * DC:(kv + 1) * DC]
            s = _dot_nt(qs[kv], kk)
            s = jnp.concatenate([s[j * tq:(j + 1) * tq] + bias for j in range(group)], axis=0)
            m_new = jnp.maximum(m, jnp.max(s, axis=1, keepdims=True))
            alpha = jnp.exp2(m - m_new)
            p = jnp.exp2(s - m_new)
            l = alpha * l + jnp.sum(p, axis=1, keepdims=True)
            acc = alpha * acc + _dot(p.astype(BF16), vv)
            new.append((m_new, l, acc))
        return tuple(new)

    init = tuple((jnp.full((group * tq, 1), NEG, F32), jnp.zeros((group * tq, 1), F32),
                  jnp.zeros((group * tq, DC), F32)) for _ in range(KVC))
    res = lax.fori_loop(0, nkb, attend_block, init)
    for kv in range(KVC):
        _, l, acc = res[kv]
        o = acc / l
        for j in range(group):
            hq = kv * group + j
            o_ref[:, hq * DC:(hq + 1) * DC] = o[j * tq:(j + 1) * tq, :].astype(o_ref.dtype)


def _dsa(qc, qi, wi, k, v, ki, prev_out, nseq, sq, l_valid, l_pad, causal, tq, q_off, name):
    topk = min(INDEX_TOPK, l_valid // 4)
    kblk = _pick(l_pad, (512, 256, 128))
    assert kblk >= topk and sq % tq == 0 and q_off % tq == 0
    nq = sq // tq
    qmap = lambda b, i: (q_off // tq + b * nq + i, 0)
    kmap = lambda b, i: (b, 0)
    in_specs = [pl.BlockSpec((tq, HC * DC), qmap), pl.BlockSpec((tq, HI * DI), qmap),
                pl.BlockSpec((tq, LANES), qmap),
                _resident((l_pad, KVC * DC), kmap), _resident((l_pad, KVC * DC), kmap),
                _resident((l_pad, DI), kmap)]
    args = [qc, qi, wi, k, v, ki]
    aliases = {}
    if prev_out is not None:
        in_specs.append(pl.BlockSpec(memory_space=pl.ANY))
        aliases = {len(args): 0}
        args.append(prev_out)
    return pl.pallas_call(
        functools.partial(_dsa_kernel, tq, kblk, l_valid, l_pad, topk, causal),
        grid=(nseq, nq),
        in_specs=in_specs,
        out_specs=pl.BlockSpec((tq, HC * DC), qmap),
        out_shape=jax.ShapeDtypeStruct((qc.shape[0], HC * DC), BF16),
        scratch_shapes=[pltpu.VMEM((l_pad // kblk, tq, kblk), I32)],
        input_output_aliases=aliases,
        compiler_params=_cparams(("arbitrary", "arbitrary")),
        name=name,
    )(*args)


def _matmul_kernel(a_ref, w_ref, o_ref):
    o_ref[...] = _dot(a_ref[...], w_ref[...]).astype(o_ref.dtype)


def _latent_up(ckv, w_ukv):
    r = ckv.shape[0]
    tm = _pick(r, (1024, 512, 256, 128, 64))
    n = w_ukv.shape[1]
    return pl.pallas_call(
        _matmul_kernel,
        grid=(r // tm,),
        in_specs=[pl.BlockSpec((tm, KV_LORA), lambda i: (i, 0)), _resident(w_ukv.shape, lambda i: (0, 0))],
        out_specs=pl.BlockSpec((tm, n), lambda i: (i, 0)),
        out_shape=jax.ShapeDtypeStruct((r, n), BF16),
        compiler_params=_cparams(("arbitrary",)),
        name="latent_up",
    )(ckv, w_ukv)


def _mla_last_block(i, tq, kblk):
    return (i * tq + tq - 1) // kblk


def _latent_up_t_kernel(ckv_ref, kpe_ref, wk_ref, wvt_ref, k_o, vt_o):
    ckv = ckv_ref[...]
    kn = _dot(ckv, wk_ref[...]).astype(k_o.dtype)
    kpe = jnp.concatenate([kpe_ref[...], jnp.zeros((kpe_ref.shape[0], LANES - DROPE), k_o.dtype)], axis=1)
    for hh in range(HD):
        k_o[:, hh * QD_SLOT:hh * QD_SLOT + DNOPE] = kn[:, hh * DNOPE:(hh + 1) * DNOPE]
        k_o[:, hh * QD_SLOT + DNOPE:(hh + 1) * QD_SLOT] = kpe
    vt_o[...] = _dot_nt(wvt_ref[...], ckv).astype(vt_o.dtype)


def _latent_up_t(ckv, kpe, w_uk, w_uv_t, rows):
    tm = _pick(rows, (512, 256, 128))
    return pl.pallas_call(
        _latent_up_t_kernel,
        grid=(rows // tm,),
        in_specs=[pl.BlockSpec((tm, KV_LORA), lambda i: (i, 0)), pl.BlockSpec((tm, DROPE), lambda i: (i, 0)),
                  _resident(w_uk.shape, lambda i: (0, 0)), _resident(w_uv_t.shape, lambda i: (0, 0))],
        out_specs=[pl.BlockSpec((tm, HD * QD_SLOT), lambda i: (i, 0)),
                   pl.BlockSpec((HD * DVD, tm), lambda i: (0, i))],
        out_shape=[jax.ShapeDtypeStruct((rows, HD * QD_SLOT), BF16),
                   jax.ShapeDtypeStruct((HD * DVD, rows), BF16)],
        compiler_params=_cparams(("arbitrary",)),
        name="latent_up_t",
    )(ckv, kpe, w_uk, w_uv_t)


def _mla_t_kernel(tq, kblk, it_ref, jt_ref, q_ref, k_ref, vt_ref, o_ref, m_s, l_s, acc_s):
    step_id = pl.program_id(1)
    i = it_ref[step_id]
    j = jt_ref[step_id]
    last = _mla_last_block(i, tq, kblk)

    @pl.when(j == 0)
    def _():
        m_s[...] = jnp.full(m_s.shape, NEG, F32)
        l_s[...] = jnp.zeros(l_s.shape, F32)
        acc_s[...] = jnp.zeros(acc_s.shape, F32)

    def step(masked):
        if masked:
            kidx = j * kblk + lax.broadcasted_iota(I32, (kblk, tq), 0)
            qpos = i * tq + lax.broadcasted_iota(I32, (kblk, tq), 1)
            bias = jnp.where(kidx < (qpos // CHUNK + 1) * CHUNK, 0.0, NEG)
        for hh in range(HD):
            s = _dot_nt(k_ref[:, hh * QD_SLOT:(hh + 1) * QD_SLOT], q_ref[:, hh * QD_SLOT:(hh + 1) * QD_SLOT])
            if masked:
                s = s + bias
            m = m_s[hh]
            m_new = jnp.maximum(m, jnp.max(s, axis=0, keepdims=True))
            alpha = jnp.exp2(m - m_new)
            p = jnp.exp2(s - m_new)
            l_s[hh] = alpha * l_s[hh] + jnp.sum(p, axis=0, keepdims=True)
            acc_s[hh] = alpha * acc_s[hh] + _dot(vt_ref[hh * DVD:(hh + 1) * DVD, :], p.astype(BF16))
            m_s[hh] = m_new

    n_full = (i * tq + CHUNK) // kblk
    pl.when(j < n_full)(functools.partial(step, False))
    pl.when(j >= n_full)(functools.partial(step, True))

    @pl.when(j == last)
    def _():
        for hh in range(HD):
            o_ref[:, hh * DVD:(hh + 1) * DVD] = (acc_s[hh] / l_s[hh]).T.astype(o_ref.dtype)


def _mla_t(qd, k_cat, v_t, nseq, sq, tq):
    kblk = _pick(sq, (1024, 512, 256, 128))
    nq, nk = sq // tq, sq // kblk
    pairs = [(i, j) for i in range(nq) for j in range(_mla_last_block(i, tq, kblk) + 1)]
    it = jnp.asarray([p[0] for p in pairs], I32)
    jt = jnp.asarray([p[1] for p in pairs], I32)
    qmap = lambda b, t, it, jt: (b * nq + it[t], 0)
    return pl.pallas_call(
        functools.partial(_mla_t_kernel, tq, kblk),
        grid_spec=pltpu.PrefetchScalarGridSpec(
            num_scalar_prefetch=2,
            grid=(nseq, len(pairs)),
            in_specs=[pl.BlockSpec((tq, HD * QD_SLOT), qmap),
                      pl.BlockSpec((kblk, HD * QD_SLOT), lambda b, t, it, jt: (b * nk + jt[t], 0)),
                      pl.BlockSpec((HD * DVD, kblk), lambda b, t, it, jt: (0, b * nk + jt[t]))],
            out_specs=pl.BlockSpec((tq, HD * DVD), qmap),
            scratch_shapes=[pltpu.VMEM((HD, 1, tq), F32), pltpu.VMEM((HD, 1, tq), F32),
                            pltpu.VMEM((HD, DVD, tq), F32)]),
        out_shape=jax.ShapeDtypeStruct((qd.shape[0], HD * DVD), BF16),
        compiler_params=_cparams(("arbitrary", "arbitrary")),
        name="mla_prompt",
    )(it, jt, qd, k_cat, v_t)


def _mla_kernel(tq, kblk, l_valid, causal, it_ref, jt_ref, q_ref, kv_ref, kpe_ref, *rest):
    o_ref, m_s, l_s, acc_s = rest[-4:]
    step_id = pl.program_id(1)
    i = it_ref[step_id]
    j = jt_ref[step_id]
    last = _mla_last_block(i, tq, kblk) if causal else pl.cdiv(l_valid, kblk) - 1

    @pl.when(j == 0)
    def _():
        m_s[...] = jnp.full(m_s.shape, NEG, F32)
        l_s[...] = jnp.zeros(l_s.shape, F32)
        acc_s[...] = jnp.zeros(acc_s.shape, F32)

    def step(masked):
        if masked:
            idx = j * kblk + lax.broadcasted_iota(I32, (tq, kblk), 1)
            if causal:
                rowq = i * tq + lax.broadcasted_iota(I32, (tq, kblk), 0)
                valid = idx < (rowq // CHUNK + 1) * CHUNK
            else:
                valid = idx < l_valid
            bias = jnp.where(valid, 0.0, NEG)
        kpe = kpe_ref[...]
        for hh in range(HD):
            kn = kv_ref[:, hh * DNOPE:(hh + 1) * DNOPE]
            vv = kv_ref[:, HD * DNOPE + hh * DVD:HD * DNOPE + (hh + 1) * DVD]
            s = (_dot_nt(q_ref[:, hh * QD_SLOT:hh * QD_SLOT + DNOPE], kn)
                 + _dot_nt(q_ref[:, hh * QD_SLOT + DNOPE:hh * QD_SLOT + DNOPE + DROPE], kpe))
            if masked:
                s = s + bias
            m = m_s[hh]
            m_new = jnp.maximum(m, jnp.max(s, axis=1, keepdims=True))
            alpha = jnp.exp2(m - m_new)
            p = jnp.exp2(s - m_new)
            l_s[hh] = alpha * l_s[hh] + jnp.sum(p, axis=1, keepdims=True)
            acc_s[hh] = alpha * acc_s[hh] + _dot(p.astype(BF16), vv)
            m_s[hh] = m_new

    if causal:
        n_full = (i * tq + CHUNK) // kblk
    else:
        n_full = l_valid // kblk
    pl.when(j < n_full)(functools.partial(step, False))
    pl.when(j >= n_full)(functools.partial(step, True))

    @pl.when(j == last)
    def _():
        for hh in range(HD):
            o_ref[:, hh * DVD:(hh + 1) * DVD] = (acc_s[hh] / l_s[hh]).astype(o_ref.dtype)


def _mla(qd, kv_up, kpe, prev_out, nseq, sq, l_valid, l_pad, causal, tq, q_off, kv_off, name):
    kblk = _pick(l_pad, (512, 256, 128))
    assert q_off % tq == 0 and kv_off % kblk == 0
    nq, nk = sq // tq, l_pad // kblk
    n_blocks = (lambda i: (i * tq + tq - 1) // kblk + 1) if causal else (lambda i: -(-l_valid // kblk))
    pairs = [(i, j) for i in range(nq) for j in range(n_blocks(i))]
    it = jnp.asarray([p[0] for p in pairs], I32)
    jt = jnp.asarray([p[1] for p in pairs], I32)
    qmap = lambda b, t, it, jt: (q_off // tq + b * nq + it[t], 0)
    kvmap = lambda b, t, it, jt: (kv_off // kblk + b * nk + jt[t], 0)
    kpemap = lambda b, t, it, jt: (b * nk + jt[t], 0)
    in_specs = [pl.BlockSpec((tq, HD * QD_SLOT), qmap),
                pl.BlockSpec((kblk, HD * (DNOPE + DVD)), kvmap), pl.BlockSpec((kblk, DROPE), kpemap)]
    args = [it, jt, qd, kv_up, kpe]
    aliases = {}
    if prev_out is not None:
        in_specs.append(pl.BlockSpec(memory_space=pl.ANY))
        aliases = {len(args): 0}
        args.append(prev_out)
    return pl.pallas_call(
        functools.partial(_mla_kernel, tq, kblk, l_valid, causal),
        grid_spec=pltpu.PrefetchScalarGridSpec(
            num_scalar_prefetch=2,
            grid=(nseq, len(pairs)),
            in_specs=in_specs,
            out_specs=pl.BlockSpec((tq, HD * DVD), qmap),
            scratch_shapes=[pltpu.VMEM((HD, tq, 1), F32), pltpu.VMEM((HD, tq, 1), F32),
                            pltpu.VMEM((HD, tq, DVD), F32)]),
        out_shape=jax.ShapeDtypeStruct((qd.shape[0], HD * DVD), BF16),
        input_output_aliases=aliases,
        compiler_params=_cparams(("arbitrary", "arbitrary")),
        name=name,
    )(*args)


def _mix_ffn_kernel(final_norm, a1_ref, a2_ref, w1_ref, w2_ref, x_ref, gate1_ref,
                    g_ref, sc_ref, sh_ref, gate2_ref, wg_ref, wu_ref, wd_ref, gf_ref, *rest):
    h_s, acc_s = rest[-2:]
    i = pl.program_id(0)
    f = pl.program_id(1)

    def on_own_output(fn):
        if final_norm is None:
            fn(rest[0])
        else:
            pl.when(i < final_norm)(functools.partial(fn, rest[0]))
            pl.when(i >= final_norm)(functools.partial(fn, rest[1]))

    @pl.when(f == 0)
    def _():
        y = _dot(a1_ref[...], w1_ref[...]) + _dot(a2_ref[...], w2_ref[...])
        x1 = x_ref[...] + _gate_rows(y, gate1_ref[...])
        h_s[...] = _modulate(x1, g_ref[...], sc_ref[...], sh_ref[...]).astype(BF16)
        acc_s[...] = jnp.zeros(acc_s.shape, F32)

        def keep(o_ref):
            o_ref[...] = x1

        on_own_output(keep)

    h = h_s[...]
    a = _dot(h, wg_ref[...])
    u = _dot(h, wu_ref[...])
    acc_s[...] += _dot((a * jax.nn.sigmoid(a) * u).astype(BF16), wd_ref[...])

    @pl.when(f == pl.num_programs(1) - 1)
    def _():
        def finish(o_ref):
            y = o_ref[...] + _gate_rows(acc_s[...], gate2_ref[...])
            if final_norm is not None:
                y = y * lax.rsqrt(jnp.mean(y * y, axis=-1, keepdims=True) + EPS) * gf_ref[...]
            o_ref[...] = y

        on_own_output(finish)


def _mix_ffn(a1, a2, w_out, pair, x, modc, layer, g, wg, wu, wd, g_final, prompt_rows=None):
    t, d = x.shape
    f = wg.shape[2]
    tm = _token_tile(t)
    tf = _pick(f, (512, 256, 128))
    rows = tm // CHUNK
    k1, k2 = a1.shape[1], a2.shape[1]
    assert k1 == k2
    w_mix = _to_bf16(w_out, pair)
    modspec = lambda comp: pl.BlockSpec((None, rows, d), lambda i, j: (layer, i, comp))
    if prompt_rows is None:
        final_norm = None
        out_specs = pl.BlockSpec((tm, d), lambda i, j: (i, 0))
        out_shape = jax.ShapeDtypeStruct((t, d), F32)
    else:
        assert prompt_rows % tm == 0
        final_norm = prompt_rows // tm
        out_specs = [pl.BlockSpec((tm, d), lambda i, j: (jnp.minimum(i, final_norm - 1), 0)),
                     pl.BlockSpec((tm, d), lambda i, j: (jnp.maximum(i - final_norm, 0), 0))]
        out_shape = [jax.ShapeDtypeStruct((prompt_rows, d), F32),
                     jax.ShapeDtypeStruct((t - prompt_rows, d), F32)]
    return pl.pallas_call(
        functools.partial(_mix_ffn_kernel, final_norm),
        grid=(t // tm, f // tf),
        in_specs=[pl.BlockSpec((tm, k1), lambda i, j: (i, 0)), pl.BlockSpec((tm, k2), lambda i, j: (i, 0)),
                  _resident((k1, d), lambda i, j: (0, 0)), _resident((k2, d), lambda i, j: (1, 0)),
                  pl.BlockSpec((tm, d), lambda i, j: (i, 0)),
                  modspec(2),
                  _resident((1, d), lambda i, j: (0, 0)),
                  modspec(4), modspec(3), modspec(5),
                  pl.BlockSpec((d, tf), lambda i, j: (0, j)),
                  pl.BlockSpec((d, tf), lambda i, j: (0, j)),
                  pl.BlockSpec((tf, d), lambda i, j: (j, 0)),
                  _resident((1, d), lambda i, j: (0, 0))],
        out_specs=out_specs,
        out_shape=out_shape,
        scratch_shapes=[pltpu.VMEM((tm, d), BF16), pltpu.VMEM((tm, d), F32)],
        compiler_params=_cparams(("arbitrary", "arbitrary"),
                                 VMEM_LIMIT if prompt_rows is None else SPLIT_OUT_VMEM_LIMIT),
        name="mix_ffn",
    )(a1, a2, w_mix, w_mix, x, modc, g, modc, modc, modc,
      _to_bf16(wg, layer), _to_bf16(wu, layer), _to_bf16(wd, layer), g_final)


def _round_up(n, m):
    return (n + m - 1) // m * m


def kernel(x_prompt, x_sample, c_prompt, c_sample, cache_a_k, cache_a_v, state_b_c, state_b_n, state_b_m,
           cache_c_k, cache_c_v, cache_c_idx, cache_d_ckv, cache_d_kpe, w_ada, b_ada, g_norm_mix, g_norm_ffn,
           w_in_even, w_out_even, sinks_a, b_igate, b_fgate, g_mlstm, w_in_odd, w_out_odd, g_ckv, w_uk, w_uv,
           w_ffn_gate, w_ffn_up, w_ffn_down, g_final):
    nb, s, d = x_prompt.shape
    db, ds, _ = x_sample.shape
    assert ds == CHUNK and s % CHUNK == 0
    depth = w_ada.shape[0]
    past = cache_c_k.shape[2]
    tp, ts = nb * s, db * ds
    t = tp + ts
    tm = _token_tile(t)
    assert s % tm == 0 and tp % tm == 0
    seq_chunks = s // CHUNK
    npc = tp // CHUNK

    x = jnp.concatenate([x_prompt.reshape(tp, d), x_sample.reshape(ts, d)], axis=0)
    mod = _ada(jnp.concatenate([c_prompt, c_sample], axis=0), w_ada, b_ada)
    mod_p = jnp.broadcast_to(mod[:, :nb, None, :], (depth, nb, seq_chunks, 6 * d)).reshape(depth, npc, 6 * d)
    modc = jnp.concatenate([mod_p, mod[:, nb:]], axis=1)

    pos = jnp.concatenate([jnp.arange(s, dtype=I32), jnp.tile(past + jnp.arange(ds, dtype=I32), tm // ds)])
    tab_a = _rope_tables(pos, ROT_A, DA)
    tab_c = _rope_tables(pos, ROT_C, DC)
    tab_d = _rope_tables(pos, DROPE, DROPE)
    prompt_tiles, seq_tiles = tp // tm, s // tm
    tab_map = lambda i: jnp.where(i < prompt_tiles, i % seq_tiles, seq_tiles)

    w_ukv_all = jnp.concatenate([w_uk, w_uv], axis=2).astype(BF16)
    l_s = past + ds
    lp_s = _round_up(l_s, 512)

    def cat_past(cache, new, width):
        full = jnp.concatenate([cache.reshape(db, past, width), new.reshape(db, ds, width)], axis=1)
        full = jnp.pad(full, ((0, 0), (0, lp_s - l_s), (0, 0)))
        return full.astype(BF16).reshape(db * lp_s, width)

    even_p, even_s, odd_p, odd_s = [], [], [], []
    for l in range(depth):
        i = l // 2
        g_mix = g_norm_mix[l].reshape(1, d)
        if l % 2 == 0:
            gate_bias = _pad_cols(jnp.concatenate([b_igate[i], b_fgate[i]]).reshape(1, 2 * HB), LANES)
            qa, ka, va, qb, kb, vb, gates, og = _proj_in(
                _proj_even_kernel, "proj_even", x, modc, l, g_mix, _even_weight(w_in_even[i]), gate_bias,
                list(tab_a), tab_map,
                [(HA * DA, BF16), (KVA * DA, F32), (KVA * DA, F32), (HB * DKB, BF16), (HB * DKB, BF16),
                 (HB * DVB, BF16), (LANES, F32), (HB * DVB, BF16)])
            ka_s = ka[tp:].reshape(db, ds, KVA * DA)
            va_s = va[tp:].reshape(db, ds, KVA * DA)
            win_k = cache_a_k[i].reshape(db, WINDOW, KVA * DA)
            win_v = cache_a_v[i].reshape(db, WINDOW, KVA * DA)
            kband = jnp.concatenate([win_k, ka_s], axis=1)
            vband = jnp.concatenate([win_v, va_s], axis=1)
            oa = _swa(qa, ka, va, win_k.reshape(db * WINDOW, KVA * DA), win_v.reshape(db * WINDOW, KVA * DA),
                      sinks_a[i], tp, s)
            c0 = jnp.concatenate([jnp.zeros((nb, HB, DKB, DVB), F32), state_b_c[i]], axis=0)
            n0 = jnp.concatenate([jnp.zeros((nb, HB, DKB), F32), state_b_n[i]], axis=0)
            m0 = jnp.concatenate([jnp.zeros((nb, HB), F32), state_b_m[i]], axis=0).reshape(nb + db, 1, HB)
            hb, c_new, n_new, m_new = _mlstm(qb, kb, vb, gates, og, g_mlstm[i].reshape(1, HB * DVB),
                                             c0, n0, m0, npc, seq_chunks, nb)
            m_new = m_new.reshape(nb + db, HB)
            ka_p = ka[:tp].reshape(nb, s, KVA, DA)
            va_p = va[:tp].reshape(nb, s, KVA, DA)
            even_p.append((ka_p[:, -WINDOW:], va_p[:, -WINDOW:], c_new[:nb], n_new[:nb], m_new[:nb]))
            even_s.append((kband[:, -WINDOW:].reshape(db, WINDOW, KVA, DA),
                           vband[:, -WINDOW:].reshape(db, WINDOW, KVA, DA), c_new[nb:], n_new[nb:], m_new[nb:]))
            mix = (oa, hb, w_out_even)
        else:
            qc, kc, vc, qi, ki, wi, qd, ckv, kpe, kcb, vcb, kib, ckvb, kpeb = _proj_in(
                _proj_odd_kernel, "proj_odd", x, modc, l, g_mix, _odd_weight(w_in_odd[i]),
                g_ckv[i].reshape(1, KV_LORA), list(tab_a) + list(tab_c) + list(tab_d), tab_map,
                [(HC * DC, BF16), (KVC * DC, F32), (KVC * DC, F32), (HI * DI, BF16), (DI, F32), (LANES, F32),
                 (HD * QD_SLOT, BF16), (KV_LORA, F32), (DROPE, F32),
                 (KVC * DC, BF16), (KVC * DC, BF16), (DI, BF16), (KV_LORA, BF16), (DROPE, BF16)])
            oc = _dsa(qc, qi, wi, kcb, vcb, kib, None, nb, s, s, s, True, 128, 0, "dsa_prompt")
            oc = _dsa(qc, qi, wi, cat_past(cache_c_k[i], kcb[tp:], KVC * DC),
                      cat_past(cache_c_v[i], vcb[tp:], KVC * DC), cat_past(cache_c_idx[i], kib[tp:], DI),
                      oc, db, ds, l_s, lp_s, False, ds, tp, "dsa_sample")
            k_cat, v_t = _latent_up_t(ckvb, kpeb, w_uk[i].astype(BF16), w_uv[i].T.astype(BF16), tp)
            od = _mla_t(qd, k_cat, v_t, nb, s, _pick(s, (1024, 512, 256, 128)))
            kv_up = _latent_up(cat_past(cache_d_ckv[i], ckvb[tp:], KV_LORA), w_ukv_all[i])
            od = _mla(qd, kv_up, cat_past(cache_d_kpe[i], kpeb[tp:], DROPE), od,
                      db, ds, l_s, lp_s, False, ds, tp, 0, "mla_sample")
            odd_p.append((kc[:tp].reshape(nb, s, KVC, DC), vc[:tp].reshape(nb, s, KVC, DC),
                          ki[:tp].reshape(nb, s, DI), ckv[:tp].reshape(nb, s, KV_LORA),
                          kpe[:tp].reshape(nb, s, DROPE)))
            odd_s.append((kc[tp:].reshape(db, ds, KVC, DC), vc[tp:].reshape(db, ds, KVC, DC),
                          ki[tp:].reshape(db, ds, DI), ckv[tp:].reshape(db, ds, KV_LORA),
                          kpe[tp:].reshape(db, ds, DROPE)))
            mix = (oc, od, w_out_odd)
        x = _mix_ffn(*mix, i, x, modc, l, g_norm_ffn[l].reshape(1, d), w_ffn_gate, w_ffn_up, w_ffn_down,
                     g_final.reshape(1, d), tp if l == depth - 1 else None)

    y_prompt = x[0].reshape(nb, s, d)
    y_sample = x[1].reshape(db, ds, d)
    st_p = [jnp.stack([e[j] for e in even_p]) for j in range(5)] + [jnp.stack([o[j] for o in odd_p]) for j in range(5)]
    st_s = [jnp.stack([e[j] for e in even_s]) for j in range(5)] + [jnp.stack([o[j] for o in odd_s]) for j in range(5)]
    return (y_prompt, y_sample, *st_p, *st_s)
```
